```python
import math
import jax, jax.numpy as jnp
from jax import lax
import numpy as np

D_MODEL = 4096
BATCH = 8
SEQ = 2048
DEPTH = 4
DEC_BATCH = 2
DEC_SEQ = 8192
PAST_LEN = 128

N_MIXERS = 2
N_META = 16
GRID_W = 64
HEAD_DIM = 128
N_HEADS = D_MODEL // HEAD_DIM
WIN_ROWS = 8
WIN_COLS = 16
HY_EMB = 33
HY_BANDS = (HY_EMB - 1) // 2
HY_FILTER_WIDTH = 64
HY_FAST_DECAY = 0.3
HY_SLOW_DECAY = 1.5
HY_DECAY_TARGET = 1e-2
CONV_W = 3
D_FF = ((8 * D_MODEL // 3 + 255) // 256) * 256
RMS_EPS = 1e-6

kernel_name = "hyena_natten_hybrid_encoder"


def rmsnorm(x, g):
    xf = x.astype(jnp.float32)
    y = xf * lax.rsqrt(jnp.mean(xf * xf, axis=-1, keepdims=True) + RMS_EPS)
    return (y * g.astype(jnp.float32)).astype(x.dtype)


def dwconv_centred(x, w):
    xp = jnp.pad(x, ((0, 0), (1, 1), (0, 0)))
    return w[0] * xp[:, :-2] + w[1] * xp[:, 1:-1] + w[2] * xp[:, 2:]


def hyena_filter_spectrum(L, w1, w_inner, b, freq, w_out, delta):
    pos = jnp.arange(L, dtype=jnp.float32)[:, None]
    t = pos / max(L - 1, 1)
    f = jnp.linspace(1e-4, HY_BANDS - 1, HY_BANDS, dtype=jnp.float32)[None, :]
    ang = f * (2.0 * math.pi / L) * pos
    z = jnp.concatenate([t, jnp.cos(ang), -jnp.sin(ang)], axis=-1)
    h = jnp.sin(freq[0] * (z @ w1 + b[0]))
    h = jnp.sin(freq[1] * (h @ w_inner[0] + b[1]))
    h = jnp.sin(freq[2] * (h @ w_inner[1] + b[2]))
    h = (h @ w_out).astype(jnp.float32).reshape(L, 2, D_MODEL)
    h = h * jnp.exp(-t[:, :, None] * jnp.abs(delta.astype(jnp.float32))[None])
    k = jnp.concatenate([h[:, 0], jnp.zeros((1, D_MODEL), jnp.float32), h[:0:-1, 1]], axis=0)
    k = k / jnp.sum(jnp.abs(k), axis=0, keepdims=True)
    return jnp.fft.rfft(k, axis=0)


def fftconv_bidir(u, k_spec, skip):
    L = u.shape[1]
    uf = u.astype(jnp.float32)
    U = jnp.fft.rfft(uf, n=2 * L, axis=1)
    y = jnp.fft.irfft(U * k_spec[None], n=2 * L, axis=1)[:, :L]
    return (y + uf * skip.astype(jnp.float32)).astype(u.dtype)


def hyena_mixer(h, w_in, w_conv, f_w1, f_w_inner, f_b, f_freq, f_w_out, delta, skip, w_out):
    L = h.shape[1]
    p = dwconv_centred(h @ w_in, w_conv)
    x0, x1, v = jnp.split(p, 3, axis=-1)
    k_spec = hyena_filter_spectrum(L, f_w1, f_w_inner, f_b, f_freq, f_w_out, delta)
    z = fftconv_bidir(v * x1, k_spec, skip)
    return (z * x0) @ w_out


def neighbourhood_attention(h, w_qkv, rpb, w_out):
    B, L, _ = h.shape
    n_tok = L - N_META
    rows = n_tok // GRID_W
    kr = min(WIN_ROWS, rows)
    qkv = (h @ w_qkv).reshape(B, L, 3, N_HEADS, HEAD_DIM)
    q = qkv[:, :, 0] * (HEAD_DIM ** -0.5)
    k = qkv[:, :, 1]
    v = qkv[:, :, 2]
    qm, km, vm = q[:, :N_META], k[:, :N_META], v[:, :N_META]
    grid = (B, rows, GRID_W, N_HEADS, HEAD_DIM)
    qg = q[:, N_META:].reshape(grid)
    kg = k[:, N_META:].reshape(grid)
    vg = v[:, N_META:].reshape(grid)
    pm = jax.nn.softmax(jnp.einsum("bqhd,bkhd->bhqk", qm, km).astype(jnp.float32), axis=-1)
    out_m = jnp.einsum("bhqk,bkhd->bqhd", pm.astype(v.dtype), vm)
    cols = np.arange(GRID_W)
    col_start = np.clip(cols - WIN_COLS // 2, 0, GRID_W - WIN_COLS)
    col_mask = (cols[None, :] >= col_start[:, None]) & (cols[None, :] < col_start[:, None] + WIN_COLS)
    col_idx = np.clip(cols[None, :] - cols[:, None] + WIN_COLS - 1, 0, 2 * WIN_COLS - 2)

    def row_block(args):
        r, q_row = args
        rs = jnp.clip(r - kr // 2, 0, rows - kr)
        k_rows = lax.dynamic_slice_in_dim(kg, rs, kr, axis=1)
        v_rows = lax.dynamic_slice_in_dim(vg, rs, kr, axis=1)
        dr = rs + jnp.arange(kr) - r + (WIN_ROWS - 1)
        bias = jnp.transpose(rpb[:, dr][:, :, col_idx], (0, 2, 1, 3)).astype(jnp.float32)
        s_grid = jnp.einsum("bqhd,brkhd->bhqrk", q_row, k_rows).astype(jnp.float32) + bias[None]
        s_grid = jnp.where(col_mask[:, None, :], s_grid, -jnp.inf).reshape(B, N_HEADS, GRID_W, kr * GRID_W)
        s_meta = jnp.einsum("bqhd,bmhd->bhqm", q_row, km).astype(jnp.float32)
        p = jax.nn.softmax(jnp.concatenate([s_meta, s_grid], axis=-1), axis=-1).astype(v.dtype)
        p_meta = p[..., :N_META]
        p_grid = p[..., N_META:].reshape(B, N_HEADS, GRID_W, kr, GRID_W)
        return (jnp.einsum("bhqm,bmhd->bqhd", p_meta, vm)
                + jnp.einsum("bhqrk,brkhd->bqhd", p_grid, v_rows))

    out_g = lax.map(row_block, (jnp.arange(rows), jnp.moveaxis(qg, 1, 0)))
    out_g = jnp.moveaxis(out_g, 0, 1).reshape(B, n_tok, D_MODEL)
    out = jnp.concatenate([out_m.reshape(B, N_META, D_MODEL), out_g], axis=1)
    return out @ w_out


def conv_ffn(h, w_up, w_conv, w_down):
    u = dwconv_centred(h @ w_up, w_conv)
    g, val = jnp.split(u, 2, axis=-1)
    return (jax.nn.silu(g) * val) @ w_down


def encoder_trunk(x, meta_tokens, norm_mix, norm_ffn, norm_final,
                  hy_w_in, hy_w_conv, hy_f_w1, hy_f_w_inner, hy_f_b, hy_f_freq, hy_f_w_out,
                  hy_delta, hy_skip, hy_w_out, na_w_qkv, na_rpb, na_w_out,
                  ffn_w_up, ffn_w_conv, ffn_w_down):
    B = x.shape[0]
    meta = jnp.broadcast_to(meta_tokens[None].astype(x.dtype), (B, N_META, D_MODEL))
    h = jnp.concatenate([meta, x], axis=1)
    for i in range(DEPTH):
        j = i // N_MIXERS
        hn = rmsnorm(h, norm_mix[i])
        if i % N_MIXERS == 0:
            h = h + hyena_mixer(hn, hy_w_in[j], hy_w_conv[j], hy_f_w1[j], hy_f_w_inner[j], hy_f_b[j],
                                hy_f_freq[j], hy_f_w_out[j], hy_delta[j], hy_skip[j], hy_w_out[j])
        else:
            h = h + neighbourhood_attention(hn, na_w_qkv[j], na_rpb[j], na_w_out[j])
        h = h + conv_ffn(rmsnorm(h, norm_ffn[i]), ffn_w_up[i], ffn_w_conv[i], ffn_w_down[i])
    return rmsnorm(h, norm_final)[:, N_META:]


def setup_inputs(seed: int = 0) -> dict:
    key = jax.random.key(seed)
    ks = jax.random.split(key, 24)
    n_hy = (DEPTH + N_MIXERS - 1) // N_MIXERS
    n_na = DEPTH // N_MIXERS
    D, F, W = D_MODEL, D_FF, HY_FILTER_WIDTH

    def nrm(k, shape, s):
        return jax.random.normal(k, shape, jnp.float32) * s

    decay_lin = jnp.linspace(math.log(HY_DECAY_TARGET) / HY_SLOW_DECAY,
                             math.log(HY_DECAY_TARGET) / HY_FAST_DECAY, D, dtype=jnp.float32)
    return {
        "x_prompt": nrm(ks[0], (BATCH, SEQ, D), 1.0),
        "x_sample": nrm(ks[1], (DEC_BATCH, DEC_SEQ, D), 1.0),
        "meta_tokens": nrm(ks[2], (N_META, D), 1.0),
        "norm_mix": 1.0 + nrm(ks[3], (DEPTH, D), 0.02),
        "norm_ffn": 1.0 + nrm(ks[4], (DEPTH, D), 0.02),
        "norm_final": 1.0 + nrm(ks[5], (D,), 0.02),
        "hy_w_in": nrm(ks[6], (n_hy, D, 3 * D), D ** -0.5),
        "hy_w_conv": nrm(ks[7], (n_hy, CONV_W, 3 * D), CONV_W ** -0.5),
        "hy_f_w1": nrm(ks[8], (n_hy, HY_EMB, W), HY_EMB ** -0.5),
        "hy_f_w_inner": nrm(ks[9], (n_hy, 2, W, W), W ** -0.5),
        "hy_f_b": nrm(ks[10], (n_hy, 3, W), 0.1),
        "hy_f_freq": 1.0 + nrm(ks[11], (n_hy, 3, W), 0.1),
        "hy_f_w_out": nrm(ks[12], (n_hy, W, 2 * D), W ** -0.5),
        "hy_delta": decay_lin * (1.0 + nrm(ks[13], (n_hy, 2, D), 0.05)),
        "hy_skip": nrm(ks[14], (n_hy, D), 0.5),
        "hy_w_out": nrm(ks[15], (n_hy, D, D), D ** -0.5),
        "na_w_qkv": nrm(ks[16], (n_na, D, 3 * D), D ** -0.5),
        "na_rpb": nrm(ks[17], (n_na, N_HEADS, 2 * WIN_ROWS - 1, 2 * WIN_COLS - 1), 0.1),
        "na_w_out": nrm(ks[18], (n_na, D, D), D ** -0.5),
        "ffn_w_up": nrm(ks[19], (DEPTH, D, 2 * F), D ** -0.5),
        "ffn_w_conv": nrm(ks[20], (DEPTH, CONV_W, 2 * F), CONV_W ** -0.5),
        "ffn_w_down": nrm(ks[21], (DEPTH, F, D), F ** -0.5),
    }


def reference(x_prompt, x_sample, meta_tokens, norm_mix, norm_ffn, norm_final,
              hy_w_in, hy_w_conv, hy_f_w1, hy_f_w_inner, hy_f_b, hy_f_freq, hy_f_w_out,
              hy_delta, hy_skip, hy_w_out, na_w_qkv, na_rpb, na_w_out,
              ffn_w_up, ffn_w_conv, ffn_w_down):
    y_prompt = encoder_trunk(x_prompt, meta_tokens, norm_mix, norm_ffn, norm_final,
                             hy_w_in, hy_w_conv, hy_f_w1, hy_f_w_inner, hy_f_b, hy_f_freq, hy_f_w_out,
                             hy_delta, hy_skip, hy_w_out, na_w_qkv, na_rpb, na_w_out,
                             ffn_w_up, ffn_w_conv, ffn_w_down)
    y_sample = encoder_trunk(x_sample, meta_tokens, norm_mix, norm_ffn, norm_final,
                             hy_w_in, hy_w_conv, hy_f_w1, hy_f_w_inner, hy_f_b, hy_f_freq, hy_f_w_out,
                             hy_delta, hy_skip, hy_w_out, na_w_qkv, na_rpb, na_w_out,
                             ffn_w_up, ffn_w_conv, ffn_w_down)
    return (y_prompt, y_sample)
```

```python
import functools
import math
from typing import NamedTuple

import numpy as np
import jax
import jax.numpy as jnp
from jax import lax
from jax.experimental import pallas as pl
from jax.experimental.pallas import tpu as pltpu

N_META = 16
GRID_W = 64
HEAD_DIM = 128
WIN_ROWS = 8
WIN_COLS = 16
HY_EMB = 33
HY_BANDS = (HY_EMB - 1) // 2
RMS_EPS = 1e-6

SLOT_PAD = 16
HALO = 16
FFT_RADIX = 16
Q_ROWS = 4
Q_BLOCK = Q_ROWS * GRID_W
MASK_VALUE = -1e30
LANES = 128
V7X_VMEM_LIMIT = 56 * 1024 * 1024

_TM = 512
_F32 = jnp.float32
_BF16 = jnp.bfloat16


class _Trunk(NamedTuple):
    batch: int
    n_tok: int
    seq: int
    slot: int
    base: int


def _round_up(x, m):
    return (x + m - 1) // m * m


def _pick(n, target, mult):
    best = None
    for d in range(mult, min(n, target) + 1, mult):
        if n % d == 0:
            best = d
    return best if best is not None else n


def _plan(shapes, tm):
    trunks, base = [], 0
    for batch, n_tok in shapes:
        seq = N_META + n_tok
        slot = SLOT_PAD + seq
        trunks.append(_Trunk(batch, n_tok, seq, slot, base))
        base += batch * slot
    rows = _round_up(base + SLOT_PAD, tm)
    return trunks, rows


def _params(sem, vmem=V7X_VMEM_LIMIT):
    return pltpu.CompilerParams(dimension_semantics=sem, vmem_limit_bytes=vmem)


def _rmsnorm_kernel(h_ref, g_ref, valid_ref, o_ref):
    x = h_ref[...]
    y = x * lax.rsqrt(jnp.mean(x * x, axis=-1, keepdims=True) + RMS_EPS)
    y = y * g_ref[...]
    o_ref[...] = jnp.where(valid_ref[:, 0:1] > 0, y, 0.0).astype(o_ref.dtype)


def _rmsnorm(h, g, valid, out_dtype):
    rows, d = h.shape
    tr = _pick(rows, 256, 16)
    return pl.pallas_call(
        _rmsnorm_kernel,
        grid=(rows // tr,),
        in_specs=[pl.BlockSpec((tr, d), lambda i: (i, 0)),
                  pl.BlockSpec((1, d), lambda i: (0, 0)),
                  pl.BlockSpec((tr, LANES), lambda i: (i, 0))],
        out_specs=pl.BlockSpec((tr, d), lambda i: (i, 0)),
        out_shape=jax.ShapeDtypeStruct((rows, d), out_dtype),
        compiler_params=_params(("parallel",)),
        name="rmsnorm",
    )(h, g.reshape(1, d), valid)


def _halo_specs(tm, d, rows):
    per = tm // HALO
    last = rows // HALO - 1
    return [
        pl.BlockSpec((HALO, d), lambda i, j: (jnp.maximum(i * per - 1, 0), 0)),
        pl.BlockSpec((tm, d), lambda i, j: (i, 0)),
        pl.BlockSpec((HALO, d), lambda i, j: (jnp.minimum((i + 1) * per, last), 0)),
    ]


def _assemble_halo(xp_ref, x_ref, xn_ref, xe_ref, tm):
    @pl.when(pl.program_id(1) == 0)
    def _():
        xe_ref[0:HALO] = xp_ref[...]
        xe_ref[HALO:HALO + tm] = x_ref[...]
        xe_ref[HALO + tm:HALO + tm + HALO] = xn_ref[...]


def _conv3(p, c, tm):
    return (c[0:1] * p[HALO - 1:HALO - 1 + tm] + c[1:2] * p[HALO:HALO + tm]
            + c[2:3] * p[HALO + 1:HALO + 1 + tm])


def _ffn_up_kernel(xp_ref, x_ref, xn_ref, wg_ref, wv_ref, cg_ref, cv_ref, o_ref, xe_ref, *, tm):
    _assemble_halo(xp_ref, x_ref, xn_ref, xe_ref, tm)
    x = xe_ref[...]
    g = _conv3(jnp.dot(x, wg_ref[...], preferred_element_type=_F32), cg_ref[...], tm)
    v = _conv3(jnp.dot(x, wv_ref[...], preferred_element_type=_F32), cv_ref[...], tm)
    o_ref[...] = (g * (1.0 / (1.0 + jnp.exp(-g))) * v).astype(o_ref.dtype)


def _ffn_up(hn, w_up, w_conv, tm):
    rows, d = hn.shape
    f = w_up.shape[1] // 2
    tn = _pick(f, 512, LANES)
    nj = f // tn
    return pl.pallas_call(
        functools.partial(_ffn_up_kernel, tm=tm),
        grid=(rows // tm, nj),
        in_specs=_halo_specs(tm, d, rows) + [
            pl.BlockSpec((d, tn), lambda i, j: (0, j)),
            pl.BlockSpec((d, tn), lambda i, j: (0, j + nj)),
            pl.BlockSpec((3, tn), lambda i, j: (0, j)),
            pl.BlockSpec((3, tn), lambda i, j: (0, j + nj)),
        ],
        out_specs=pl.BlockSpec((tm, tn), lambda i, j: (i, j)),
        out_shape=jax.ShapeDtypeStruct((rows, f), _BF16),
        scratch_shapes=[pltpu.VMEM((tm + 2 * HALO, d), _BF16)],
        compiler_params=_params(("parallel", "arbitrary")),
        name="ffn_up",
    )(hn, hn, hn, w_up, w_up, w_conv, w_conv)


def _hyena_in_kernel(xp_ref, x_ref, xn_ref, w0_ref, w1_ref, w2_ref, c0_ref, c1_ref, c2_ref,
                     x0_ref, vx_ref, xe_ref, *, tm):
    _assemble_halo(xp_ref, x_ref, xn_ref, xe_ref, tm)
    x = xe_ref[...]
    x0 = _conv3(jnp.dot(x, w0_ref[...], preferred_element_type=_F32), c0_ref[...], tm)
    x1 = _conv3(jnp.dot(x, w1_ref[...], preferred_element_type=_F32), c1_ref[...], tm)
    v = _conv3(jnp.dot(x, w2_ref[...], preferred_element_type=_F32), c2_ref[...], tm)
    x0_ref[...] = x0
    vx_ref[...] = v * x1


def _hyena_in(hn, w_in, w_conv, tm):
    rows, d = hn.shape
    tn = _pick(d, 512, LANES)
    nj = d // tn
    wspec = lambda k: pl.BlockSpec((d, tn), lambda i, j: (0, j + k * nj))
    cspec = lambda k: pl.BlockSpec((3, tn), lambda i, j: (0, j + k * nj))
    out = pl.BlockSpec((tm, tn), lambda i, j: (i, j))
    return pl.pallas_call(
        functools.partial(_hyena_in_kernel, tm=tm),
        grid=(rows // tm, nj),
        in_specs=_halo_specs(tm, d, rows) + [wspec(0), wspec(1), wspec(2), cspec(0), cspec(1), cspec(2)],
        out_specs=[out, out],
        out_shape=[jax.ShapeDtypeStruct((rows, d), _F32)] * 2,
        scratch_shapes=[pltpu.VMEM((tm + 2 * HALO, d), _BF16)],
        compiler_params=_params(("parallel", "arbitrary")),
        name="hyena_in",
    )(hn, hn, hn, w_in, w_in, w_in, w_conv, w_conv, w_conv)


def _qkv_kernel(x_ref, w_ref, o_ref, *, n_q_tiles, scale):
    acc = jnp.dot(x_ref[...], w_ref[...], preferred_element_type=_F32)
    s = jnp.where(pl.program_id(1) < n_q_tiles, scale, 1.0).astype(_F32)
    o_ref[...] = (acc * s).astype(o_ref.dtype)


def _qkv(hn, w_qkv, tm):
    rows, d = hn.shape
    n = w_qkv.shape[1]
    tn = _pick(d, 1024, LANES)
    return pl.pallas_call(
        functools.partial(_qkv_kernel, n_q_tiles=d // tn, scale=HEAD_DIM ** -0.5),
        grid=(rows // tm, n // tn),
        in_specs=[pl.BlockSpec((tm, d), lambda i, j: (i, 0)),
                  pl.BlockSpec((d, tn), lambda i, j: (0, j))],
        out_specs=pl.BlockSpec((tm, tn), lambda i, j: (i, j)),
        out_shape=jax.ShapeDtypeStruct((rows, n), _BF16),
        compiler_params=_params(("parallel", "arbitrary")),
        name="qkv_proj",
    )(hn, w_qkv)


def _residual_kernel(x_ref, w_ref, h_ref, o_ref):
    o_ref[...] = h_ref[...] + jnp.dot(x_ref[...], w_ref[...], preferred_element_type=_F32)


def _residual_matmul(x, w, h, tm, name):
    rows, k = x.shape
    d = w.shape[1]
    tn = _pick(d, 512 if k <= 4096 else 256, LANES)
    return pl.pallas_call(
        _residual_kernel,
        grid=(rows // tm, d // tn),
        in_specs=[pl.BlockSpec((tm, k), lambda i, j: (i, 0)),
                  pl.BlockSpec((k, tn), lambda i, j: (0, j)),
                  pl.BlockSpec((tm, tn), lambda i, j: (i, j))],
        out_specs=pl.BlockSpec((tm, tn), lambda i, j: (i, j)),
        out_shape=jax.ShapeDtypeStruct((rows, d), _F32),
        input_output_aliases={2: 0},
        compiler_params=_params(("parallel", "arbitrary")),
        name=name,
    )(x, w, h)


def _fft_sizes(seq):
    n1 = _round_up(-(-(2 * seq - 1) // FFT_RADIX), 8)
    n1z = _round_up(-(-seq // FFT_RADIX), 16)
    return n1, n1z


def _dft_matrices(seq):
    n1, n1z = _fft_sizes(seq)
    n = FFT_RADIX * n1
    n2 = jnp.arange(FFT_RADIX, dtype=jnp.int32)[:, None, None]
    k1 = jnp.arange(n1, dtype=jnp.int32)[None, :, None]
    m1 = jnp.arange(n1z, dtype=jnp.int32)[None, None, :]
    phase = ((m1 * k1 * FFT_RADIX + n2 * k1) % n).astype(_F32) * (2.0 * math.pi / n)
    g = jnp.concatenate([jnp.cos(phase), -jnp.sin(phase)], axis=1)
    h = jnp.swapaxes(g, 1, 2) * (1.0 / n)
    return g.astype(_BF16), h.astype(_BF16)


def _dft_fwd_kernel(x_ref, g_ref, o_ref):
    n1 = o_ref.shape[1]
    a = jnp.dot(g_ref[...], x_ref[...].astype(_BF16), preferred_element_type=_F32)
    o_ref[0] = a[:n1]
    o_ref[1] = a[n1:]


def _dft_fwd(x, g):
    b, r, n1z, d = x.shape
    n1 = g.shape[1] // 2
    dc = _pick(d, 512, LANES)
    return pl.pallas_call(
        _dft_fwd_kernel,
        grid=(r, b, d // dc),
        in_specs=[pl.BlockSpec((None, None, n1z, dc), lambda p, i, c: (i, p, 0, c)),
                  pl.BlockSpec((None, 2 * n1, n1z), lambda p, i, c: (p, 0, 0))],
        out_specs=pl.BlockSpec((2, None, None, n1, dc), lambda p, i, c: (0, p, i, 0, c)),
        out_shape=jax.ShapeDtypeStruct((2, r, b, n1, d), _F32),
        compiler_params=_params(("arbitrary", "arbitrary", "arbitrary")),
        name="dft_fwd",
    )(x, g)


def _cmul_const(x, w):
    re, im = x
    wr, wi = float(w.real), float(w.imag)
    if abs(wi) < 1e-12:
        return (re, im) if wr > 0 else (-re, -im)
    if abs(wr) < 1e-12:
        return (-im, re) if wi > 0 else (im, -re)
    return (re * wr - im * wi, re * wi + im * wr)


def _fft_planes(xs, sign):
    n = len(xs)
    if n == 1:
        return xs
    ev = _fft_planes(xs[0::2], sign)
    od = _fft_planes(xs[1::2], sign)
    out = [None] * n
    for k in range(n // 2):
        w = complex(math.cos(2.0 * math.pi * k / n), sign * math.sin(2.0 * math.pi * k / n))
        tr, ti = _cmul_const(od[k], w)
        out[k] = (ev[k][0] + tr, ev[k][1] + ti)
        out[k + n // 2] = (ev[k][0] - tr, ev[k][1] - ti)
    return out


def _tile_loop(tk, dc, body):
    def step(r, carry):
        rows = pl.ds(pl.multiple_of(r * 8, 8), 8)
        for l in range(dc // LANES):
            body(rows, slice(l * LANES, (l + 1) * LANES))
        return carry
    lax.fori_loop(0, tk // 8, step, 0)


def _filter_spectrum_kernel(a_ref, s_ref, o_ref):
    r, tk, dc = o_ref.shape[1], o_ref.shape[2], o_ref.shape[3]

    def body(rows, lanes):
        inv = 1.0 / s_ref[:, lanes]
        xf = _fft_planes([(a_ref[0, n, 0, rows, lanes], a_ref[1, n, 0, rows, lanes]) for n in range(r)], -1)
        xb = _fft_planes([(a_ref[0, n, 1, rows, lanes], a_ref[1, n, 1, rows, lanes]) for n in range(r)], -1)
        for k in range(r):
            o_ref[0, k, rows, lanes] = (xf[k][0] + xb[k][0]) * inv
            o_ref[1, k, rows, lanes] = (xf[k][1] - xb[k][1]) * inv

    _tile_loop(tk, dc, body)


def _plane_tile(n1):
    return _pick(n1, 352, 8)


def _filter_spectrum(planes, norm):
    _, r, _, n1, d = planes.shape
    tk, dc = _plane_tile(n1), LANES
    return pl.pallas_call(
        _filter_spectrum_kernel,
        grid=(d // dc, n1 // tk),
        in_specs=[pl.BlockSpec((2, r, 2, tk, dc), lambda c, k: (0, 0, 0, k, c)),
                  pl.BlockSpec((1, dc), lambda c, k: (0, c))],
        out_specs=pl.BlockSpec((2, r, tk, dc), lambda c, k: (0, 0, k, c)),
        out_shape=jax.ShapeDtypeStruct((2, r, n1, d), _F32),
        compiler_params=_params(("parallel", "parallel")),
        name="filter_spectrum",
    )(planes, norm)


def _spectral_product_kernel(a_ref, k_ref, o_ref):
    r, tk, dc = o_ref.shape[1], o_ref.shape[2], o_ref.shape[3]

    def body(rows, lanes):
        x = _fft_planes([(a_ref[0, n, rows, lanes], a_ref[1, n, rows, lanes]) for n in range(r)], -1)
        y = []
        for k in range(r):
            kr, ki = k_ref[0, k, rows, lanes], k_ref[1, k, rows, lanes]
            y.append((x[k][0] * kr - x[k][1] * ki, x[k][0] * ki + x[k][1] * kr))
        z = _fft_planes(y, +1)
        for n in range(r):
            o_ref[0, n, rows, lanes] = z[n][0]
            o_ref[1, n, rows, lanes] = z[n][1]

    _tile_loop(tk, dc, body)


def _spectral_product(planes, kspec):
    _, r, b, n1, d = planes.shape
    tk, dc = _plane_tile(n1), LANES
    spec = pl.BlockSpec((2, r, None, tk, dc), lambda c, k, i: (0, 0, i, k, c))
    return pl.pallas_call(
        _spectral_product_kernel,
        grid=(d // dc, n1 // tk, b),
        in_specs=[spec, pl.BlockSpec((2, r, tk, dc), lambda c, k, i: (0, 0, k, c))],
        out_specs=spec,
        out_shape=jax.ShapeDtypeStruct(planes.shape, _F32),
        compiler_params=_params(("parallel", "parallel", "arbitrary")),
        name="spectral_product",
    )(planes, kspec)


def _dft_inv_kernel(b_ref, h_ref, vx_ref, x0_ref, skip_ref, o_ref):
    n1, dc = b_ref.shape[1], b_ref.shape[2]
    bb = b_ref[...].reshape(2 * n1, dc).astype(_BF16)
    y = jnp.dot(h_ref[...], bb, preferred_element_type=_F32)
    o_ref[...] = (y + vx_ref[...] * skip_ref[...]) * x0_ref[...]


def _dft_inv(planes, hmat, vx, x0, skip):
    _, r, b, n1, d = planes.shape
    n1z = vx.shape[2]
    dc = _pick(d, 512, LANES)
    sig = pl.BlockSpec((None, None, n1z, dc), lambda p, i, c: (i, p, 0, c))
    return pl.pallas_call(
        _dft_inv_kernel,
        grid=(r, b, d // dc),
        in_specs=[pl.BlockSpec((2, None, None, n1, dc), lambda p, i, c: (0, p, i, 0, c)),
                  pl.BlockSpec((None, n1z, 2 * n1), lambda p, i, c: (p, 0, 0)),
                  sig, sig,
                  pl.BlockSpec((1, dc), lambda p, i, c: (0, c))],
        out_specs=sig,
        out_shape=jax.ShapeDtypeStruct(vx.shape, _F32),
        compiler_params=_params(("arbitrary", "arbitrary", "arbitrary")),
        name="dft_inv",
    )(planes, hmat, vx, x0, skip)


def _filter_mlp_kernel(z_ref, w1_ref, wi_ref, b_ref, f_ref, o_ref):
    hi = lax.Precision.HIGHEST
    h = jnp.sin(f_ref[0:1] * (jnp.dot(z_ref[...], w1_ref[...], precision=hi, preferred_element_type=_F32) + b_ref[0:1]))
    h = jnp.sin(f_ref[1:2] * (jnp.dot(h, wi_ref[0], precision=hi, preferred_element_type=_F32) + b_ref[1:2]))
    h = jnp.sin(f_ref[2:3] * (jnp.dot(h, wi_ref[1], precision=hi, preferred_element_type=_F32) + b_ref[2:3]))
    o_ref[...] = h


def _filter_taps_kernel(hm_ref, z_ref, wf_ref, wb_ref, d_ref, kf_ref, kb_ref, s_ref):
    hi = lax.Precision.HIGHEST
    t = z_ref[:, 0:1]
    mf = z_ref[:, HY_EMB:HY_EMB + 1]
    mb = z_ref[:, HY_EMB + 1:HY_EMB + 2]
    hm = hm_ref[...]
    kf = jnp.dot(hm, wf_ref[...], precision=hi, preferred_element_type=_F32) * jnp.exp(-t * jnp.abs(d_ref[0:1])) * mf
    kb = jnp.dot(hm, wb_ref[...], precision=hi, preferred_element_type=_F32) * jnp.exp(-t * jnp.abs(d_ref[1:2])) * mb
    kf_ref[...] = kf
    kb_ref[...] = kb

    @pl.when(pl.program_id(1) == 0)
    def _():
        s_ref[...] = jnp.zeros_like(s_ref)
    s_ref[...] += jnp.sum(jnp.abs(kf) + jnp.abs(kb), axis=0, keepdims=True)


def _position_features(seq, rows):
    pos = jnp.arange(seq, dtype=_F32)[:, None]
    t = pos / max(seq - 1, 1)
    f = jnp.linspace(1e-4, HY_BANDS - 1, HY_BANDS, dtype=_F32)[None, :]
    ang = f * (2.0 * math.pi / seq) * pos
    fwd = jnp.ones((seq, 1), _F32)
    bwd = (pos >= 1).astype(_F32)
    z = jnp.concatenate([t, jnp.cos(ang), -jnp.sin(ang), fwd, bwd], axis=-1)
    return jnp.pad(z, ((0, rows - seq), (0, LANES - z.shape[1])))


def _hyena_filter(seq, w1, w_inner, b, freq, w_out, delta, g):
    n1, n1z = _fft_sizes(seq)
    lp = n1z * FFT_RADIX
    width = w1.shape[1]
    d = w_out.shape[1] // 2
    z = _position_features(seq, lp)
    w1p = jnp.pad(w1, ((0, LANES - w1.shape[0]), (0, 0)))
    tr = n1z
    hm = pl.pallas_call(
        _filter_mlp_kernel,
        grid=(lp // tr,),
        in_specs=[pl.BlockSpec((tr, LANES), lambda i: (i, 0)),
                  pl.BlockSpec((LANES, width), lambda i: (0, 0)),
                  pl.BlockSpec((2, width, width), lambda i: (0, 0, 0)),
                  pl.BlockSpec((3, width), lambda i: (0, 0)),
                  pl.BlockSpec((3, width), lambda i: (0, 0))],
        out_specs=pl.BlockSpec((tr, width), lambda i: (i, 0)),
        out_shape=jax.ShapeDtypeStruct((lp, width), _F32),
        compiler_params=_params(("parallel",)),
        name="filter_mlp",
    )(z, w1p, w_inner, b, freq)
    dc = _pick(d, 512, LANES)
    nc = d // dc
    taps = pl.BlockSpec((tr, dc), lambda c, i: (i, c))
    kf, kb, norm = pl.pallas_call(
        _filter_taps_kernel,
        grid=(nc, lp // tr),
        in_specs=[pl.BlockSpec((tr, width), lambda c, i: (i, 0)),
                  pl.BlockSpec((tr, LANES), lambda c, i: (i, 0)),
                  pl.BlockSpec((width, dc), lambda c, i: (0, c)),
                  pl.BlockSpec((width, dc), lambda c, i: (0, c + nc)),
                  pl.BlockSpec((2, dc), lambda c, i: (0, c))],
        out_specs=[taps, taps, pl.BlockSpec((1, dc), lambda c, i: (0, c))],
        out_shape=[jax.ShapeDtypeStruct((lp, d), _F32)] * 2 + [jax.ShapeDtypeStruct((1, d), _F32)],
        compiler_params=_params(("parallel", "arbitrary")),
        name="filter_taps",
    )(hm, z, w_out, w_out, delta)
    taps2 = jnp.swapaxes(jnp.stack([kf, kb]).reshape(2, n1z, FFT_RADIX, d), 1, 2)
    return _filter_spectrum(_dft_fwd(taps2, g), norm)


def _slots(flat, trunk):
    d = flat.shape[1]
    return flat[trunk.base:trunk.base + trunk.batch * trunk.slot].reshape(trunk.batch, trunk.slot, d)


def _signal_view(flat, trunk):
    _, n1z = _fft_sizes(trunk.seq)
    sig = _slots(flat, trunk)[:, SLOT_PAD:]
    sig = jnp.pad(sig, ((0, 0), (0, n1z * FFT_RADIX - trunk.seq), (0, 0)))
    return jnp.swapaxes(sig.reshape(trunk.batch, n1z, FFT_RADIX, flat.shape[1]), 1, 2)


def _to_flat(parts, trunks, rows, dtype):
    d = parts[0].shape[-1]
    out = []
    for part, t in zip(parts, trunks):
        out.append(jnp.pad(part.astype(dtype), ((0, 0), (SLOT_PAD, 0), (0, 0))).reshape(t.batch * t.slot, d))
    used = sum(t.batch * t.slot for t in trunks)
    out.append(jnp.zeros((rows - used, d), dtype))
    return jnp.concatenate(out, axis=0)


def _hyena_mixer(hn, trunks, rows, tm, w_in, w_conv, filt, delta, skip, w_out, h, dft):
    x0, vx = _hyena_in(hn, w_in, w_conv, tm)
    parts = []
    for t in trunks:
        g, hmat = dft[t.seq]
        kspec = _hyena_filter(t.seq, *filt, delta, g)
        vx_t, x0_t = _signal_view(vx, t), _signal_view(x0, t)
        planes = _spectral_product(_dft_fwd(vx_t, g), kspec)
        y = _dft_inv(planes, hmat, vx_t, x0_t, skip.reshape(1, -1))
        parts.append(jnp.swapaxes(y, 1, 2).reshape(t.batch, -1, y.shape[-1])[:, :t.seq])
    zx = _to_flat(parts, trunks, rows, _BF16)
    return _residual_matmul(zx, w_out, h, tm, "hyena_out")


def _attention_bias(rpb, rows_in_grid):
    n_blocks = rows_in_grid // Q_ROWS
    kr_win = min(WIN_ROWS, rows_in_grid)
    qr = np.arange(Q_ROWS)[:, None, None]
    kb = np.arange(3)[None, :, None]
    kr = np.arange(Q_ROWS)[None, None, :]
    dr_idx, row_ok = [], []
    for g in (0, 1, n_blocks - 1):
        r = Q_ROWS * g + qr
        key_row = Q_ROWS * (g + kb - 1) + kr
        rs = np.clip(r - kr_win // 2, 0, rows_in_grid - kr_win)
        ok = (key_row >= rs) & (key_row < rs + kr_win) & (g + kb - 1 >= 0) & (g + kb - 1 < n_blocks)
        dr_idx.append(np.clip(key_row - r + (WIN_ROWS - 1), 0, 2 * WIN_ROWS - 2))
        row_ok.append(ok)
    dr_idx = np.stack(dr_idx)
    row_ok = np.stack(row_ok)
    cols = np.arange(GRID_W)
    col_start = np.clip(cols - WIN_COLS // 2, 0, GRID_W - WIN_COLS)
    col_ok = (cols[None, :] >= col_start[:, None]) & (cols[None, :] < col_start[:, None] + WIN_COLS)
    col_idx = np.clip(cols[None, :] - cols[:, None] + WIN_COLS - 1, 0, 2 * WIN_COLS - 2)
    by_col = rpb[:, :, col_idx]
    by_row = by_col[:, dr_idx]
    ok = row_ok[:, :, :, :, None, None] & col_ok[None, None, None, None]
    bias = jnp.where(ok[None], by_row, MASK_VALUE)
    bias = jnp.transpose(bias, (1, 0, 2, 5, 3, 4, 6))
    return bias.reshape(3, rpb.shape[0], Q_BLOCK, 3 * Q_BLOCK).astype(_F32)


def _na_kernel(q_ref, k0_ref, k1_ref, k2_ref, v0_ref, v1_ref, v2_ref, km_ref, vm_ref, bias_ref, o_ref, *, heads):
    nt = (((1,), (1,)), ((), ()))
    for h in range(heads):
        hs = slice(h * HEAD_DIM, (h + 1) * HEAD_DIM)
        q = q_ref[:, hs]
        s = [lax.dot_general(q, k_ref[:, hs], nt, preferred_element_type=_F32)
             + bias_ref[h, :, i * Q_BLOCK:(i + 1) * Q_BLOCK]
             for i, k_ref in enumerate((k0_ref, k1_ref, k2_ref))]
        sm = lax.dot_general(q, km_ref[:, hs], nt, preferred_element_type=_F32)
        m = jnp.max(sm, axis=-1, keepdims=True)
        for si in s:
            m = jnp.maximum(m, jnp.max(si, axis=-1, keepdims=True))
        pm = jnp.exp(sm - m)
        den = jnp.sum(pm, axis=-1, keepdims=True)
        acc = jnp.dot(pm.astype(_BF16), vm_ref[:, hs], preferred_element_type=_F32)
        for si, v_ref in zip(s, (v0_ref, v1_ref, v2_ref)):
            p = jnp.exp(si - m)
            den = den + jnp.sum(p, axis=-1, keepdims=True)
            acc = acc + jnp.dot(p.astype(_BF16), v_ref[:, hs], preferred_element_type=_F32)
        o_ref[:, hs] = (acc / den).astype(o_ref.dtype)


def _na_meta_kernel(q_ref, k_ref, v_ref, o_ref, *, heads):
    nt = (((1,), (1,)), ((), ()))
    for h in range(heads):
        hs = slice(h * HEAD_DIM, (h + 1) * HEAD_DIM)
        s = lax.dot_general(q_ref[:, hs], k_ref[:, hs], nt, preferred_element_type=_F32)
        p = jnp.exp(s - jnp.max(s, axis=-1, keepdims=True))
        acc = jnp.dot(p.astype(_BF16), v_ref[:, hs], preferred_element_type=_F32)
        o_ref[:, hs] = (acc / jnp.sum(p, axis=-1, keepdims=True)).astype(o_ref.dtype)


def _neighbourhood_attention(qkv, trunk, bias):
    d = qkv.shape[1] // 3
    n_heads = d // HEAD_DIM
    hb = min(4, n_heads)
    wd = hb * HEAD_DIM
    nb = trunk.n_tok // Q_BLOCK
    b = trunk.batch
    tok0 = trunk.base + SLOT_PAD + N_META
    meta0 = (trunk.base + SLOT_PAD) // N_META
    slot_meta = trunk.slot // N_META

    def win(col0, shift):
        def index(h, i, g):
            gk = jnp.clip(g + shift, 0, nb - 1)
            return (pl.multiple_of(tok0 + i * trunk.slot + gk * Q_BLOCK, N_META),
                    pl.multiple_of(col0 + h * wd, LANES))
        return pl.BlockSpec((pl.Element(Q_BLOCK), pl.Element(wd)), index)

    def meta(col0):
        return pl.BlockSpec((pl.Element(N_META), pl.Element(wd)),
                            lambda h, i, g: (pl.multiple_of(trunk.base + SLOT_PAD + i * trunk.slot, N_META),
                                             pl.multiple_of(col0 + h * wd, LANES)))

    variant = lambda g: jnp.where(g == 0, 0, jnp.where(g == nb - 1, 2, 1))
    out_grid = pl.pallas_call(
        functools.partial(_na_kernel, heads=hb),
        grid=(n_heads // hb, b, nb),
        in_specs=[win(0, 0), win(d, -1), win(d, 0), win(d, 1), win(2 * d, -1), win(2 * d, 0), win(2 * d, 1),
                  meta(d), meta(2 * d),
                  pl.BlockSpec((None, hb, Q_BLOCK, 3 * Q_BLOCK), lambda h, i, g: (variant(g), h, 0, 0))],
        out_specs=pl.BlockSpec((Q_BLOCK, wd), lambda h, i, g: (i * nb + g, h)),
        out_shape=jax.ShapeDtypeStruct((b * trunk.n_tok, d), _BF16),
        compiler_params=_params(("parallel", "parallel", "arbitrary")),
        name="na_grid",
    )(qkv, qkv, qkv, qkv, qkv, qkv, qkv, qkv, qkv, bias)
    nd = d // LANES
    mspec = lambda k: pl.BlockSpec((N_META, d), lambda i: (meta0 + i * slot_meta, k))
    out_meta = pl.pallas_call(
        functools.partial(_na_meta_kernel, heads=n_heads),
        grid=(b,),
        in_specs=[mspec(0), mspec(1), mspec(2)],
        out_specs=pl.BlockSpec((N_META, d), lambda i: (i, 0)),
        out_shape=jax.ShapeDtypeStruct((b * N_META, d), _BF16),
        compiler_params=_params(("parallel",)),
        name="na_meta",
    )(qkv, qkv, qkv)
    del nd
    return jnp.concatenate([out_meta.reshape(b, N_META, d), out_grid.reshape(b, trunk.n_tok, d)], axis=1)


def _na_mixer(hn, trunks, rows, tm, w_qkv, rpb, w_out, h):
    qkv = _qkv(hn, w_qkv, tm)
    parts = [_neighbourhood_attention(qkv, t, _attention_bias(rpb, t.n_tok // GRID_W)) for t in trunks]
    return _residual_matmul(_to_flat(parts, trunks, rows, _BF16), w_out, h, tm, "na_out")


def _valid_rows(trunks, rows):
    r = jnp.arange(rows, dtype=jnp.int32)[:, None]
    v = jnp.zeros((rows, 1), jnp.bool_)
    for t in trunks:
        inside = (r >= t.base) & (r < t.base + t.batch * t.slot)
        v = v | (inside & ((r - t.base) % t.slot >= SLOT_PAD))
    return jnp.broadcast_to(v.astype(_F32), (rows, LANES))


def kernel(x_prompt, x_sample, meta_tokens, norm_mix, norm_ffn, norm_final, hy_w_in, hy_w_conv, hy_f_w1,
           hy_f_w_inner, hy_f_b, hy_f_freq, hy_f_w_out, hy_delta, hy_skip, hy_w_out, na_w_qkv, na_rpb,
           na_w_out, ffn_w_up, ffn_w_conv, ffn_w_down):
    d = x_prompt.shape[-1]
    depth = norm_mix.shape[0]
    xs = (x_prompt, x_sample)
    tm = _TM
    trunks, rows = _plan([(x.shape[0], x.shape[1]) for x in xs], tm)
    valid = _valid_rows(trunks, rows)
    dft = {t.seq: _dft_matrices(t.seq) for t in trunks}

    meta = meta_tokens.astype(_F32)
    seqs = [jnp.concatenate([jnp.broadcast_to(meta[None], (x.shape[0], N_META, d)), x], axis=1) for x in xs]
    h = _to_flat(seqs, trunks, rows, _F32)

    bf = lambda w: w.astype(_BF16)
    for i in range(depth):
        j = i // 2
        hn = _rmsnorm(h, norm_mix[i], valid, _BF16)
        if i % 2 == 0:
            filt = (hy_f_w1[j], hy_f_w_inner[j], hy_f_b[j], hy_f_freq[j], hy_f_w_out[j])
            h = _hyena_mixer(hn, trunks, rows, tm, bf(hy_w_in[j]), hy_w_conv[j], filt, hy_delta[j],
                             hy_skip[j], bf(hy_w_out[j]), h, dft)
        else:
            h = _na_mixer(hn, trunks, rows, tm, bf(na_w_qkv[j]), na_rpb[j], bf(na_w_out[j]), h)
        hn = _rmsnorm(h, norm_ffn[i], valid, _BF16)
        a = _ffn_up(hn, bf(ffn_w_up[i]), ffn_w_conv[i], tm)
        h = _residual_matmul(a, bf(ffn_w_down[i]), h, tm, "ffn_down")
    y = _rmsnorm(h, norm_final, valid, _F32)
    return tuple(_slots(y, t)[:, SLOT_PAD + N_META:] for t in trunks)
```

```python
import functools
import math
from typing import NamedTuple

import numpy as np
import jax
import jax.numpy as jnp
from jax import lax
from jax.experimental import pallas as pl
from jax.experimental.pallas import tpu as pltpu

N_META = 16
GRID_W = 64
HEAD_DIM = 128
WIN_ROWS = 8
WIN_COLS = 16
HY_EMB = 33
HY_BANDS = (HY_EMB - 1) // 2
RMS_EPS = 1e-6

SLOT_PAD = 16
HALO = 16
EPILOGUE_ROWS = 32
FFT_RADIX = 16
Q_ROWS = 4
Q_BLOCK = Q_ROWS * GRID_W
MASK_VALUE = -1e30
LANES = 128
V7X_VMEM_LIMIT = 56 * 1024 * 1024
DFT_RESIDENT_BYTES = 6 * 1024 * 1024

_TM = 512
_FFN_TN = 512
_F32 = jnp.float32
_BF16 = jnp.bfloat16


class _Trunk(NamedTuple):
    batch: int
    n_tok: int
    seq: int
    slot: int
    base: int


def _round_up(x, m):
    return (x + m - 1) // m * m


def _pick(n, target, mult):
    best = None
    for d in range(mult, min(n, target) + 1, mult):
        if n % d == 0:
            best = d
    return best if best is not None else n


def _fft_sizes(seq):
    n1 = _round_up(-(-(2 * seq - 1) // FFT_RADIX), 8)
    n1z = _round_up(-(-seq // FFT_RADIX), 8)
    return n1, n1z


def _plan(shapes, tm):
    trunks, base = [], 0
    for batch, n_tok in shapes:
        seq = N_META + n_tok
        slot = SLOT_PAD + seq
        trunks.append(_Trunk(batch, n_tok, seq, slot, base))
        base += batch * slot
    reach = max(t.base + (t.batch - 1) * t.slot + SLOT_PAD + FFT_RADIX * _fft_sizes(t.seq)[1] for t in trunks)
    rows = _round_up(max(base + SLOT_PAD, reach), tm)
    return trunks, rows


def _params(sem, vmem=V7X_VMEM_LIMIT):
    return pltpu.CompilerParams(dimension_semantics=sem, vmem_limit_bytes=vmem)


def _window(rows, cols, index):
    def aligned(*args):
        r, c = index(*args)
        hint = lambda x, m: x if isinstance(x, int) else pl.multiple_of(x, m)
        return hint(r, SLOT_PAD), hint(c, LANES)
    return pl.BlockSpec((pl.Element(rows), pl.Element(cols)), aligned)


def _norm_rows(h, g, valid):
    y = h * lax.rsqrt(jnp.mean(h * h, axis=-1, keepdims=True) + RMS_EPS)
    return jnp.where(valid[:, 0:1] > 0, y * g, 0.0)


def _final_norm_kernel(h_ref, g_ref, o_ref):
    x = h_ref[...]
    o_ref[...] = x * lax.rsqrt(jnp.mean(x * x, axis=-1, keepdims=True) + RMS_EPS) * g_ref[...]


def _final_norm(h, g, trunk):
    d = h.shape[1]
    tr = _pick(trunk.n_tok, 256, 16)
    nr = trunk.n_tok // tr
    tok0 = trunk.base + SLOT_PAD + N_META
    out = pl.pallas_call(
        _final_norm_kernel,
        grid=(trunk.batch, nr),
        in_specs=[_window(tr, d, lambda i, r: (tok0 + i * trunk.slot + r * tr, 0)),
                  pl.BlockSpec((1, d), lambda i, r: (0, 0))],
        out_specs=pl.BlockSpec((tr, d), lambda i, r: (i * nr + r, 0)),
        out_shape=jax.ShapeDtypeStruct((trunk.batch * trunk.n_tok, d), _F32),
        compiler_params=_params(("arbitrary", "arbitrary")),
        name="final_norm",
    )(h, g.reshape(1, d))
    return out.reshape(trunk.batch, trunk.n_tok, d)


def _conv3(p_ref, k, c, r0, nr):
    at = lambda shift: p_ref[k, HALO + shift + r0:HALO + shift + r0 + nr, :]
    return c[0:1] * at(-1) + c[1:2] * at(0) + c[2:3] * at(1)


def _ffn_epilogue(vals, outs, rows):
    g, v = vals
    outs[0][rows, :] = (g * (1.0 / (1.0 + jnp.exp(-g))) * v).astype(outs[0].dtype)


def _hyena_epilogue(vals, outs, rows):
    x0, x1, v = vals
    outs[0][rows, :] = x0
    outs[1][rows, :] = v * x1


def _conv_matmul_kernel(*refs, tm, nw, nj, epilogue):
    hp_ref, hc_ref, hn_ref, vp_ref, vc_ref, vn_ref, g_ref = refs[:7]
    w_refs = refs[7:7 + nw]
    c_refs = refs[7 + nw:7 + 2 * nw]
    out_refs = refs[7 + 2 * nw:-3]
    xe_ref, pa_ref, pb_ref = refs[-3:]
    s = pl.program_id(0)

    @pl.when(s == 0)
    def _():
        pb_ref[...] = jnp.zeros_like(pb_ref)

    @pl.when(s % nj == 0)
    def _():
        g = g_ref[...]
        xe_ref[0:HALO] = _norm_rows(hp_ref[...], g, vp_ref[...]).astype(_BF16)
        xe_ref[HALO:HALO + tm] = _norm_rows(hc_ref[...], g, vc_ref[...]).astype(_BF16)
        xe_ref[HALO + tm:HALO + tm + HALO] = _norm_rows(hn_ref[...], g, vn_ref[...]).astype(_BF16)

    def step(prev_ref, cur_ref):
        cs = [c_refs[k][...] for k in range(nw)]
        for r0 in range(0, tm, EPILOGUE_ROWS):
            epilogue([_conv3(prev_ref, k, cs[k], r0, EPILOGUE_ROWS) for k in range(nw)], out_refs,
                     slice(r0, r0 + EPILOGUE_ROWS))
        x = xe_ref[...]
        for k in range(nw):
            cur_ref[k] = jnp.dot(x, w_refs[k][...], preferred_element_type=_F32)

    @pl.when(s % 2 == 0)
    def _():
        step(pb_ref, pa_ref)

    @pl.when(s % 2 == 1)
    def _():
        step(pa_ref, pb_ref)


def _conv_matmul(h, valid, gamma, w, w_conv, nw, tn, tm, epilogue, out_dtypes, name):
    rows, d = h.shape
    n = w.shape[1] // nw
    nj, ni = n // tn, rows // tm
    per, last = tm // HALO, rows // HALO - 1
    steps = ni * nj
    cur = lambda s: jnp.minimum(s, steps - 1)
    lag = lambda s: jnp.maximum(s - 1, 0)

    def panel(width):
        return [pl.BlockSpec((HALO, width), lambda s: (jnp.maximum(cur(s) // nj * per - 1, 0), 0)),
                pl.BlockSpec((tm, width), lambda s: (cur(s) // nj, 0)),
                pl.BlockSpec((HALO, width), lambda s: (jnp.minimum((cur(s) // nj + 1) * per, last), 0))]

    wspec = lambda k: pl.BlockSpec((d, tn), lambda s: (0, cur(s) % nj + k * nj))
    cspec = lambda k: pl.BlockSpec((3, tn), lambda s: (0, lag(s) % nj + k * nj))
    out = pl.BlockSpec((tm, tn), lambda s: (lag(s) // nj, lag(s) % nj))
    return pl.pallas_call(
        functools.partial(_conv_matmul_kernel, tm=tm, nw=nw, nj=nj, epilogue=epilogue),
        grid=(steps + 1,),
        in_specs=panel(d) + panel(LANES) + [pl.BlockSpec((1, d), lambda s: (0, 0))]
        + [wspec(k) for k in range(nw)] + [cspec(k) for k in range(nw)],
        out_specs=[out] * len(out_dtypes),
        out_shape=[jax.ShapeDtypeStruct((rows, n), dt) for dt in out_dtypes],
        scratch_shapes=[pltpu.VMEM((tm + 2 * HALO, d), _BF16),
                        pltpu.VMEM((nw, tm + 2 * HALO, tn), _F32),
                        pltpu.VMEM((nw, tm + 2 * HALO, tn), _F32)],
        compiler_params=_params(("arbitrary",)),
        name=name,
    )(h, h, h, valid, valid, valid, gamma.reshape(1, d), *([w] * nw), *([w_conv] * nw))


def _qkv_kernel(h_ref, v_ref, g_ref, w_ref, o_ref, x_ref, *, n_q_tiles, scale):
    @pl.when(pl.program_id(1) == 0)
    def _():
        x_ref[...] = _norm_rows(h_ref[...], g_ref[...], v_ref[...]).astype(_BF16)
    acc = jnp.dot(x_ref[...], w_ref[...], preferred_element_type=_F32)
    s = jnp.where(pl.program_id(1) < n_q_tiles, scale, 1.0).astype(_F32)
    o_ref[...] = (acc * s).astype(o_ref.dtype)


def _qkv(h, valid, gamma, w_qkv, tm):
    rows, d = h.shape
    n = w_qkv.shape[1]
    tn = _pick(d, 1024, LANES)
    return pl.pallas_call(
        functools.partial(_qkv_kernel, n_q_tiles=d // tn, scale=HEAD_DIM ** -0.5),
        grid=(rows // tm, n // tn),
        in_specs=[pl.BlockSpec((tm, d), lambda i, j: (i, 0)),
                  pl.BlockSpec((tm, LANES), lambda i, j: (i, 0)),
                  pl.BlockSpec((1, d), lambda i, j: (0, 0)),
                  pl.BlockSpec((d, tn), lambda i, j: (0, j))],
        out_specs=pl.BlockSpec((tm, tn), lambda i, j: (i, j)),
        out_shape=jax.ShapeDtypeStruct((rows, n), _BF16),
        scratch_shapes=[pltpu.VMEM((tm, d), _BF16)],
        compiler_params=_params(("arbitrary", "arbitrary")),
        name="qkv_proj",
    )(h, valid, gamma.reshape(1, d), w_qkv)


def _residual_kernel(x_ref, w_ref, h_ref, o_ref):
    o_ref[...] = h_ref[...] + jnp.dot(x_ref[...], w_ref[...], preferred_element_type=_F32)


def _residual_matmul(x, w, h, tm, name):
    rows, k = x.shape
    d = w.shape[1]
    tn = _pick(d, 512 if k <= 4096 else 256, LANES)
    return pl.pallas_call(
        _residual_kernel,
        grid=(rows // tm, d // tn),
        in_specs=[pl.BlockSpec((tm, k), lambda i, j: (i, 0)),
                  pl.BlockSpec((k, tn), lambda i, j: (0, j)),
                  pl.BlockSpec((tm, tn), lambda i, j: (i, j))],
        out_specs=pl.BlockSpec((tm, tn), lambda i, j: (i, j)),
        out_shape=jax.ShapeDtypeStruct((rows, d), _F32),
        input_output_aliases={2: 0},
        compiler_params=_params(("arbitrary", "arbitrary")),
        name=name,
    )(x, w, h)


def _dft_matrices(seq):
    n1, n1z = _fft_sizes(seq)
    n = FFT_RADIX * n1
    n2 = jnp.arange(FFT_RADIX, dtype=jnp.int32)[:, None, None]
    k1 = jnp.arange(n1, dtype=jnp.int32)[None, :, None]
    m1 = jnp.arange(n1z, dtype=jnp.int32)[None, None, :]
    phase = ((m1 * k1 * FFT_RADIX + n2 * k1) % n).astype(_F32) * (2.0 * math.pi / n)
    g = jnp.concatenate([jnp.cos(phase), -jnp.sin(phase)], axis=1)
    g = jnp.where(m1 * FFT_RADIX + n2 < seq, g, 0.0)
    h = jnp.swapaxes(g, 1, 2) * (1.0 / n)
    return g.astype(_BF16), h.astype(_BF16)


def _plane_rows(refs, plane, n1z):
    rows = pl.ds(plane, n1z, stride=FFT_RADIX)
    return jnp.concatenate([ref[rows, :] for ref in refs], axis=1)


def _planes_per_step(g):
    r, m, k = g.shape
    return r if r * m * k * g.dtype.itemsize <= DFT_RESIDENT_BYTES else 1


def _lane_windows(n_rows, dc, row0, row_stride):
    sub = lambda l: _window(n_rows, LANES, lambda i, c, p: (row0 + i * row_stride, c * dc + l * LANES))
    return [sub(l) for l in range(dc // LANES)]


def _dft_fwd_kernel(*refs, pp):
    x_refs, g_ref, o_ref = refs[:-2], refs[-2], refs[-1]
    n1z, n1 = g_ref.shape[2], o_ref.shape[2]
    for q in range(pp):
        x = _plane_rows(x_refs, pl.program_id(2) * pp + q, n1z).astype(_BF16)
        a = jnp.dot(g_ref[q], x, preferred_element_type=_F32)
        o_ref[0, q] = a[:n1]
        o_ref[1, q] = a[n1:]


def _dft_fwd(x, g, n_sig, row0, row_stride):
    d = x.shape[1]
    r, n1z = g.shape[0], g.shape[2]
    n1 = g.shape[1] // 2
    dc = _pick(d, 256, LANES)
    pp = _planes_per_step(g)
    wins = _lane_windows(n1z * r, dc, row0, row_stride)
    return pl.pallas_call(
        functools.partial(_dft_fwd_kernel, pp=pp),
        grid=(n_sig, d // dc, r // pp),
        in_specs=wins + [pl.BlockSpec((pp, 2 * n1, n1z), lambda i, c, p: (p, 0, 0))],
        out_specs=pl.BlockSpec((2, pp, None, n1, dc), lambda i, c, p: (0, p, i, 0, c)),
        out_shape=jax.ShapeDtypeStruct((2, r, n_sig, n1, d), _F32),
        compiler_params=_params(("arbitrary", "arbitrary", "arbitrary")),
        name="dft_fwd",
    )(*([x] * len(wins)), g)


def _cmul_const(x, w):
    re, im = x
    wr, wi = float(w.real), float(w.imag)
    if abs(wi) < 1e-12:
        return (re, im) if wr > 0 else (-re, -im)
    if abs(wr) < 1e-12:
        return (-im, re) if wi > 0 else (im, -re)
    return (re * wr - im * wi, re * wi + im * wr)


def _fft_planes(xs, sign):
    n = len(xs)
    if n == 1:
        return xs
    ev = _fft_planes(xs[0::2], sign)
    od = _fft_planes(xs[1::2], sign)
    out = [None] * n
    for k in range(n // 2):
        w = complex(math.cos(2.0 * math.pi * k / n), sign * math.sin(2.0 * math.pi * k / n))
        tr, ti = _cmul_const(od[k], w)
        out[k] = (ev[k][0] + tr, ev[k][1] + ti)
        out[k + n // 2] = (ev[k][0] - tr, ev[k][1] - ti)
    return out


def _tile_loop(tk, dc, body):
    def step(r, carry):
        rows = pl.ds(pl.multiple_of(r * 8, 8), 8)
        for l in range(dc // LANES):
            body(rows, slice(l * LANES, (l + 1) * LANES))
        return carry
    lax.fori_loop(0, tk // 8, step, 0)


def _filter_spectrum_kernel(a_ref, s_ref, o_ref):
    r, tk, dc = o_ref.shape[1], o_ref.shape[2], o_ref.shape[3]

    def body(rows, lanes):
        inv = 1.0 / s_ref[:, lanes]
        xf = _fft_planes([(a_ref[0, n, 0, rows, lanes], a_ref[1, n, 0, rows, lanes]) for n in range(r)], -1)
        xb = _fft_planes([(a_ref[0, n, 1, rows, lanes], a_ref[1, n, 1, rows, lanes]) for n in range(r)], -1)
        for k in range(r):
            o_ref[0, k, rows, lanes] = (xf[k][0] + xb[k][0]) * inv
            o_ref[1, k, rows, lanes] = (xf[k][1] - xb[k][1]) * inv

    _tile_loop(tk, dc, body)


def _plane_tile(n1):
    return _pick(n1, 352, 8)


def _filter_spectrum(planes, norm):
    _, r, _, n1, d = planes.shape
    tk, dc = _plane_tile(n1), LANES
    return pl.pallas_call(
        _filter_spectrum_kernel,
        grid=(d // dc, n1 // tk),
        in_specs=[pl.BlockSpec((2, r, 2, tk, dc), lambda c, k: (0, 0, 0, k, c)),
                  pl.BlockSpec((1, dc), lambda c, k: (0, c))],
        out_specs=pl.BlockSpec((2, r, tk, dc), lambda c, k: (0, 0, k, c)),
        out_shape=jax.ShapeDtypeStruct((2, r, n1, d), _F32),
        compiler_params=_params(("arbitrary", "arbitrary")),
        name="filter_spectrum",
    )(planes, norm)


def _spectral_product_kernel(a_ref, k_ref, o_ref):
    r, tk, dc = o_ref.shape[1], o_ref.shape[2], o_ref.shape[3]

    def body(rows, lanes):
        x = _fft_planes([(a_ref[0, n, rows, lanes], a_ref[1, n, rows, lanes]) for n in range(r)], -1)
        y = []
        for k in range(r):
            kr, ki = k_ref[0, k, rows, lanes], k_ref[1, k, rows, lanes]
            y.append((x[k][0] * kr - x[k][1] * ki, x[k][0] * ki + x[k][1] * kr))
        z = _fft_planes(y, +1)
        for n in range(r):
            o_ref[0, n, rows, lanes] = z[n][0]
            o_ref[1, n, rows, lanes] = z[n][1]

    _tile_loop(tk, dc, body)


def _spectral_product(planes, kspec):
    _, r, b, n1, d = planes.shape
    tk, dc = _plane_tile(n1), LANES
    spec = pl.BlockSpec((2, r, None, tk, dc), lambda c, k, i: (0, 0, i, k, c))
    return pl.pallas_call(
        _spectral_product_kernel,
        grid=(d // dc, n1 // tk, b),
        in_specs=[spec, pl.BlockSpec((2, r, tk, dc), lambda c, k, i: (0, 0, k, c))],
        out_specs=spec,
        out_shape=jax.ShapeDtypeStruct(planes.shape, _F32),
        compiler_params=_params(("arbitrary", "arbitrary", "arbitrary")),
        name="spectral_product",
    )(planes, kspec)


def _dft_inv_kernel(*refs, pp, nsub):
    b_ref, h_ref, skip_ref, o_ref = refs[0], refs[1], refs[-2], refs[-1]
    vx_refs, x0_refs = refs[2:2 + nsub], refs[2 + nsub:2 + 2 * nsub]
    n1, dc = b_ref.shape[2], b_ref.shape[3]
    n1z = h_ref.shape[1]
    for q in range(pp):
        plane = pl.program_id(2) * pp + q
        bb = jnp.concatenate([b_ref[0, q], b_ref[1, q]], axis=0).astype(_BF16)
        y = jnp.dot(h_ref[q], bb, preferred_element_type=_F32)
        o_ref[q] = (y + _plane_rows(vx_refs, plane, n1z) * skip_ref[...]) * _plane_rows(x0_refs, plane, n1z)


def _dft_inv(planes, hmat, vx, x0, skip, row0, row_stride):
    _, r, b, n1, d = planes.shape
    n1z = hmat.shape[1]
    dc = _pick(d, 256, LANES)
    pp = _planes_per_step(hmat)
    wins = _lane_windows(n1z * r, dc, row0, row_stride)
    return pl.pallas_call(
        functools.partial(_dft_inv_kernel, pp=pp, nsub=len(wins)),
        grid=(b, d // dc, r // pp),
        in_specs=[pl.BlockSpec((2, pp, None, n1, dc), lambda i, c, p: (0, p, i, 0, c)),
                  pl.BlockSpec((pp, n1z, 2 * n1), lambda i, c, p: (p, 0, 0))]
        + wins + wins + [pl.BlockSpec((1, dc), lambda i, c, p: (0, c))],
        out_specs=pl.BlockSpec((None, pp, n1z, dc), lambda i, c, p: (i, p, 0, c)),
        out_shape=jax.ShapeDtypeStruct((b, r, n1z, d), _F32),
        compiler_params=_params(("arbitrary", "arbitrary", "arbitrary")),
        name="dft_inv",
    )(planes, hmat, *([vx] * len(wins)), *([x0] * len(wins)), skip)


def _filter_mlp_kernel(z_ref, w1_ref, wi_ref, b_ref, f_ref, o_ref):
    hi = lax.Precision.HIGHEST
    h = jnp.sin(f_ref[0:1] * (jnp.dot(z_ref[...], w1_ref[...], precision=hi, preferred_element_type=_F32) + b_ref[0:1]))
    h = jnp.sin(f_ref[1:2] * (jnp.dot(h, wi_ref[0], precision=hi, preferred_element_type=_F32) + b_ref[1:2]))
    h = jnp.sin(f_ref[2:3] * (jnp.dot(h, wi_ref[1], precision=hi, preferred_element_type=_F32) + b_ref[2:3]))
    o_ref[...] = h


def _filter_taps_kernel(hm_ref, z_ref, wf_ref, wb_ref, d_ref, k_ref, s_ref):
    hi = lax.Precision.HIGHEST
    t = z_ref[:, 0:1]
    mf = z_ref[:, HY_EMB:HY_EMB + 1]
    mb = z_ref[:, HY_EMB + 1:HY_EMB + 2]
    hm = hm_ref[...]
    kf = jnp.dot(hm, wf_ref[...], precision=hi, preferred_element_type=_F32) * jnp.exp(-t * jnp.abs(d_ref[0:1])) * mf
    kb = jnp.dot(hm, wb_ref[...], precision=hi, preferred_element_type=_F32) * jnp.exp(-t * jnp.abs(d_ref[1:2])) * mb
    k_ref[0] = kf
    k_ref[1] = kb

    @pl.when(pl.program_id(1) == 0)
    def _():
        s_ref[...] = jnp.zeros_like(s_ref)
    s_ref[...] += jnp.sum(jnp.abs(kf) + jnp.abs(kb), axis=0, keepdims=True)


def _position_features(seq, rows):
    pos = jnp.arange(seq, dtype=_F32)[:, None]
    t = pos / max(seq - 1, 1)
    f = jnp.linspace(1e-4, HY_BANDS - 1, HY_BANDS, dtype=_F32)[None, :]
    ang = f * (2.0 * math.pi / seq) * pos
    fwd = jnp.ones((seq, 1), _F32)
    bwd = (pos >= 1).astype(_F32)
    z = jnp.concatenate([t, jnp.cos(ang), -jnp.sin(ang), fwd, bwd], axis=-1)
    return jnp.pad(z, ((0, rows - seq), (0, LANES - z.shape[1])))


def _hyena_filter(seq, w1, w_inner, b, freq, w_out, delta, g):
    _, n1z = _fft_sizes(seq)
    lp = n1z * FFT_RADIX
    width = w1.shape[1]
    d = w_out.shape[1] // 2
    z = _position_features(seq, lp)
    w1p = jnp.pad(w1, ((0, LANES - w1.shape[0]), (0, 0)))
    tr = n1z
    hm = pl.pallas_call(
        _filter_mlp_kernel,
        grid=(lp // tr,),
        in_specs=[pl.BlockSpec((tr, LANES), lambda i: (i, 0)),
                  pl.BlockSpec((LANES, width), lambda i: (0, 0)),
                  pl.BlockSpec((2, width, width), lambda i: (0, 0, 0)),
                  pl.BlockSpec((3, width), lambda i: (0, 0)),
                  pl.BlockSpec((3, width), lambda i: (0, 0))],
        out_specs=pl.BlockSpec((tr, width), lambda i: (i, 0)),
        out_shape=jax.ShapeDtypeStruct((lp, width), _F32),
        compiler_params=_params(("arbitrary",)),
        name="filter_mlp",
    )(z, w1p, w_inner, b, freq)
    dc = _pick(d, 512, LANES)
    nc = d // dc
    taps, norm = pl.pallas_call(
        _filter_taps_kernel,
        grid=(nc, lp // tr),
        in_specs=[pl.BlockSpec((tr, width), lambda c, i: (i, 0)),
                  pl.BlockSpec((tr, LANES), lambda c, i: (i, 0)),
                  pl.BlockSpec((width, dc), lambda c, i: (0, c)),
                  pl.BlockSpec((width, dc), lambda c, i: (0, c + nc)),
                  pl.BlockSpec((2, dc), lambda c, i: (0, c))],
        out_specs=[pl.BlockSpec((2, tr, dc), lambda c, i: (0, i, c)), pl.BlockSpec((1, dc), lambda c, i: (0, c))],
        out_shape=[jax.ShapeDtypeStruct((2, lp, d), _F32), jax.ShapeDtypeStruct((1, d), _F32)],
        compiler_params=_params(("arbitrary", "arbitrary")),
        name="filter_taps",
    )(hm, z, w_out, w_out, delta)
    return _filter_spectrum(_dft_fwd(taps.reshape(2 * lp, d), g, 2, 0, lp), norm)


def _to_flat(parts, trunks, rows, dtype):
    d = parts[0].shape[-1]
    out = []
    for part, t in zip(parts, trunks):
        out.append(jnp.pad(part.astype(dtype), ((0, 0), (SLOT_PAD, 0), (0, 0))).reshape(t.batch * t.slot, d))
    used = sum(t.batch * t.slot for t in trunks)
    out.append(jnp.zeros((rows - used, d), dtype))
    return jnp.concatenate(out, axis=0)


def _hyena_mixer(h, valid, gamma, trunks, rows, tm, w_in, w_conv, filt, delta, skip, w_out, dft):
    d = h.shape[1]
    x0, vx = _conv_matmul(h, valid, gamma, w_in, w_conv, 3, _pick(d, 256, LANES), tm, _hyena_epilogue,
                          (_F32, _F32), "hyena_in")
    parts = []
    for t in trunks:
        g, hmat = dft[t.seq]
        kspec = _hyena_filter(t.seq, *filt, delta, g)
        row0 = t.base + SLOT_PAD
        planes = _spectral_product(_dft_fwd(vx, g, t.batch, row0, t.slot), kspec)
        y = _dft_inv(planes, hmat, vx, x0, skip.reshape(1, -1), row0, t.slot)
        parts.append(jnp.swapaxes(y, 1, 2).reshape(t.batch, -1, d)[:, :t.seq])
    return _residual_matmul(_to_flat(parts, trunks, rows, _BF16), w_out, h, tm, "hyena_out")


def _attention_bias(rpb, rows_in_grid):
    n_blocks = rows_in_grid // Q_ROWS
    kr_win = min(WIN_ROWS, rows_in_grid)
    qr = np.arange(Q_ROWS)[:, None, None]
    kb = np.arange(3)[None, :, None]
    kr = np.arange(Q_ROWS)[None, None, :]
    dr_idx, row_ok = [], []
    for g in (0, 1, n_blocks - 1):
        r = Q_ROWS * g + qr
        key_row = Q_ROWS * (g + kb - 1) + kr
        rs = np.clip(r - kr_win // 2, 0, rows_in_grid - kr_win)
        ok = (key_row >= rs) & (key_row < rs + kr_win) & (g + kb - 1 >= 0) & (g + kb - 1 < n_blocks)
        dr_idx.append(np.clip(key_row - r + (WIN_ROWS - 1), 0, 2 * WIN_ROWS - 2))
        row_ok.append(ok)
    dr_idx = np.stack(dr_idx)
    row_ok = np.stack(row_ok)
    cols = np.arange(GRID_W)
    col_start = np.clip(cols - WIN_COLS // 2, 0, GRID_W - WIN_COLS)
    col_ok = (cols[None, :] >= col_start[:, None]) & (cols[None, :] < col_start[:, None] + WIN_COLS)
    col_idx = np.clip(cols[None, :] - cols[:, None] + WIN_COLS - 1, 0, 2 * WIN_COLS - 2)
    by_col = rpb[:, :, col_idx]
    by_row = by_col[:, dr_idx]
    ok = row_ok[:, :, :, :, None, None] & col_ok[None, None, None, None]
    bias = jnp.where(ok[None], by_row, MASK_VALUE)
    bias = jnp.transpose(bias, (1, 0, 2, 5, 3, 4, 6))
    return bias.reshape(3, rpb.shape[0], Q_BLOCK, 3 * Q_BLOCK).astype(_F32)


def _na_kernel(q_ref, k0_ref, k1_ref, k2_ref, v0_ref, v1_ref, v2_ref, km_ref, vm_ref, bias_ref, o_ref, *, heads):
    nt = (((1,), (1,)), ((), ()))
    for h in range(heads):
        hs = slice(h * HEAD_DIM, (h + 1) * HEAD_DIM)
        q = q_ref[:, hs]
        s = [lax.dot_general(q, k_ref[:, hs], nt, preferred_element_type=_F32)
             + bias_ref[h, :, i * Q_BLOCK:(i + 1) * Q_BLOCK]
             for i, k_ref in enumerate((k0_ref, k1_ref, k2_ref))]
        sm = lax.dot_general(q, km_ref[:, hs], nt, preferred_element_type=_F32)
        m = jnp.max(sm, axis=-1, keepdims=True)
        for si in s:
            m = jnp.maximum(m, jnp.max(si, axis=-1, keepdims=True))
        pm = jnp.exp(sm - m)
        den = jnp.sum(pm, axis=-1, keepdims=True)
        acc = jnp.dot(pm.astype(_BF16), vm_ref[:, hs], preferred_element_type=_F32)
        for si, v_ref in zip(s, (v0_ref, v1_ref, v2_ref)):
            p = jnp.exp(si - m)
            den = den + jnp.sum(p, axis=-1, keepdims=True)
            acc = acc + jnp.dot(p.astype(_BF16), v_ref[:, hs], preferred_element_type=_F32)
        o_ref[:, hs] = (acc / den).astype(o_ref.dtype)


def _na_meta_kernel(q_ref, k_ref, v_ref, o_ref, *, heads):
    nt = (((1,), (1,)), ((), ()))
    for h in range(heads):
        hs = slice(h * HEAD_DIM, (h + 1) * HEAD_DIM)
        s = lax.dot_general(q_ref[:, hs], k_ref[:, hs], nt, preferred_element_type=_F32)
        p = jnp.exp(s - jnp.max(s, axis=-1, keepdims=True))
        acc = jnp.dot(p.astype(_BF16), v_ref[:, hs], preferred_element_type=_F32)
        o_ref[:, hs] = (acc / jnp.sum(p, axis=-1, keepdims=True)).astype(o_ref.dtype)


def _neighbourhood_attention(qkv, trunk, bias):
    d = qkv.shape[1] // 3
    n_heads = d // HEAD_DIM
    hb = min(4, n_heads)
    wd = hb * HEAD_DIM
    nb = trunk.n_tok // Q_BLOCK
    b = trunk.batch
    tok0 = trunk.base + SLOT_PAD + N_META
    meta0 = (trunk.base + SLOT_PAD) // N_META
    slot_meta = trunk.slot // N_META

    def win(col0, shift):
        return _window(Q_BLOCK, wd, lambda h, i, g: (
            tok0 + i * trunk.slot + jnp.clip(g + shift, 0, nb - 1) * Q_BLOCK, col0 + h * wd))

    def meta(col0):
        return _window(N_META, wd, lambda h, i, g: (trunk.base + SLOT_PAD + i * trunk.slot, col0 + h * wd))

    variant = lambda g: jnp.where(g == 0, 0, jnp.where(g == nb - 1, 2, 1))
    out_grid = pl.pallas_call(
        functools.partial(_na_kernel, heads=hb),
        grid=(n_heads // hb, b, nb),
        in_specs=[win(0, 0), win(d, -1), win(d, 0), win(d, 1), win(2 * d, -1), win(2 * d, 0), win(2 * d, 1),
                  meta(d), meta(2 * d),
                  pl.BlockSpec((None, hb, Q_BLOCK, 3 * Q_BLOCK), lambda h, i, g: (variant(g), h, 0, 0))],
        out_specs=pl.BlockSpec((Q_BLOCK, wd), lambda h, i, g: (i * nb + g, h)),
        out_shape=jax.ShapeDtypeStruct((b * trunk.n_tok, d), _BF16),
        compiler_params=_params(("arbitrary", "arbitrary", "arbitrary")),
        name="na_grid",
    )(qkv, qkv, qkv, qkv, qkv, qkv, qkv, qkv, qkv, bias)
    mspec = lambda k: pl.BlockSpec((N_META, d), lambda i: (meta0 + i * slot_meta, k))
    out_meta = pl.pallas_call(
        functools.partial(_na_meta_kernel, heads=n_heads),
        grid=(b,),
        in_specs=[mspec(0), mspec(1), mspec(2)],
        out_specs=pl.BlockSpec((N_META, d), lambda i: (i, 0)),
        out_shape=jax.ShapeDtypeStruct((b * N_META, d), _BF16),
        compiler_params=_params(("arbitrary",)),
        name="na_meta",
    )(qkv, qkv, qkv)
    return jnp.concatenate([out_meta.reshape(b, N_META, d), out_grid.reshape(b, trunk.n_tok, d)], axis=1)


def _na_mixer(h, valid, gamma, trunks, rows, tm, w_qkv, rpb, w_out):
    qkv = _qkv(h, valid, gamma, w_qkv, tm)
    biases = {}
    parts = []
    for t in trunks:
        grid_rows = t.n_tok // GRID_W
        key = grid_rows if grid_rows < 3 * Q_ROWS else -1
        if key not in biases:
            biases[key] = _attention_bias(rpb, grid_rows)
        parts.append(_neighbourhood_attention(qkv, t, biases[key]))
    return _residual_matmul(_to_flat(parts, trunks, rows, _BF16), w_out, h, tm, "na_out")


def _valid_rows(trunks, rows):
    r = jnp.arange(rows, dtype=jnp.int32)[:, None]
    v = jnp.zeros((rows, 1), jnp.bool_)
    for t in trunks:
        inside = (r >= t.base) & (r < t.base + t.batch * t.slot)
        v = v | (inside & ((r - t.base) % t.slot >= SLOT_PAD))
    return jnp.broadcast_to(v.astype(_F32), (rows, LANES))


def _padded_ffn_weights(w_up, w_conv, w_down, tn):
    f = w_down.shape[0]
    pad = _round_up(f, tn) - f
    halves = lambda a: jnp.concatenate([jnp.pad(a[:, :f], ((0, 0), (0, pad))), jnp.pad(a[:, f:], ((0, 0), (0, pad)))], axis=1)
    return halves(w_up).astype(_BF16), halves(w_conv), jnp.pad(w_down, ((0, pad), (0, 0))).astype(_BF16)


def kernel(x_prompt, x_sample, meta_tokens, norm_mix, norm_ffn, norm_final, hy_w_in, hy_w_conv, hy_f_w1,
           hy_f_w_inner, hy_f_b, hy_f_freq, hy_f_w_out, hy_delta, hy_skip, hy_w_out, na_w_qkv, na_rpb,
           na_w_out, ffn_w_up, ffn_w_conv, ffn_w_down):
    d = x_prompt.shape[-1]
    depth = norm_mix.shape[0]
    xs = (x_prompt, x_sample)
    tm = _TM
    trunks, rows = _plan([(x.shape[0], x.shape[1]) for x in xs], tm)
    valid = _valid_rows(trunks, rows)
    dft = {t.seq: _dft_matrices(t.seq) for t in trunks}
    ffn_tn = _pick(_round_up(ffn_w_down.shape[1], _FFN_TN), _FFN_TN, LANES)

    meta = meta_tokens.astype(_F32)
    seqs = [jnp.concatenate([jnp.broadcast_to(meta[None], (x.shape[0], N_META, d)), x], axis=1) for x in xs]
    h = _to_flat(seqs, trunks, rows, _F32)

    bf = lambda w: w.astype(_BF16)
    for i in range(depth):
        j = i // 2
        if i % 2 == 0:
            filt = (hy_f_w1[j], hy_f_w_inner[j], hy_f_b[j], hy_f_freq[j], hy_f_w_out[j])
            h = _hyena_mixer(h, valid, norm_mix[i], trunks, rows, tm, bf(hy_w_in[j]), hy_w_conv[j], filt,
                             hy_delta[j], hy_skip[j], bf(hy_w_out[j]), dft)
        else:
            h = _na_mixer(h, valid, norm_mix[i], trunks, rows, tm, bf(na_w_qkv[j]), na_rpb[j], bf(na_w_out[j]))
        w_up, w_conv, w_down = _padded_ffn_weights(ffn_w_up[i], ffn_w_conv[i], ffn_w_down[i], ffn_tn)
        a = _conv_matmul(h, valid, norm_ffn[i], w_up, w_conv, 2, ffn_tn, tm, _ffn_epilogue, (_BF16,), "ffn_up")[0]
        h = _residual_matmul(a, w_down, h, tm, "ffn_down")
    return tuple(_final_norm(h, norm_final, t) for t in trunks)
```

```python
import functools
import math
from typing import NamedTuple

import numpy as np
import jax
import jax.numpy as jnp
from jax import lax
from jax.experimental import pallas as pl
from jax.experimental.pallas import tpu as pltpu

N_META = 16
GRID_W = 64
HEAD_DIM = 128
WIN_ROWS = 8
WIN_COLS = 16
HY_EMB = 33
HY_BANDS = (HY_EMB - 1) // 2
RMS_EPS = 1e-6

SLOT_PAD = 16
HALO = 16
EPILOGUE_ROWS = 32
MXU_COLS = 256
RESIDENT_WEIGHT_K = 4096
FFT_RADIX = 16
Q_ROWS = 4
Q_BLOCK = Q_ROWS * GRID_W
MASK_VALUE = -1e30
LANES = 128
V7X_VMEM_LIMIT = 56 * 1024 * 1024
DFT_RESIDENT_BYTES = 6 * 1024 * 1024

_TM = 512
_FFN_TN = 512
_F32 = jnp.float32
_BF16 = jnp.bfloat16


class _Trunk(NamedTuple):
    batch: int
    n_tok: int
    seq: int
    slot: int
    base: int


def _round_up(x, m):
    return (x + m - 1) // m * m


def _pick(n, target, mult):
    best = None
    for d in range(mult, min(n, target) + 1, mult):
        if n % d == 0:
            best = d
    return best if best is not None else n


def _fft_sizes(seq):
    n1 = _round_up(-(-(2 * seq - 1) // FFT_RADIX), 8)
    n1z = _round_up(-(-seq // FFT_RADIX), 8)
    return n1, n1z


def _plan(shapes, tm):
    trunks, base = [], 0
    for batch, n_tok in shapes:
        seq = N_META + n_tok
        slot = SLOT_PAD + seq
        trunks.append(_Trunk(batch, n_tok, seq, slot, base))
        base += batch * slot
    reach = max(t.base + (t.batch - 1) * t.slot + SLOT_PAD + FFT_RADIX * _fft_sizes(t.seq)[1] for t in trunks)
    rows = _round_up(max(base + SLOT_PAD, reach), tm)
    return trunks, rows


def _params(sem, vmem=V7X_VMEM_LIMIT):
    return pltpu.CompilerParams(dimension_semantics=sem, vmem_limit_bytes=vmem)


def _window(rows, cols, index):
    def aligned(*args):
        r, c = index(*args)
        hint = lambda x, m: x if isinstance(x, int) else pl.multiple_of(x, m)
        return hint(r, SLOT_PAD), hint(c, LANES)
    return pl.BlockSpec((pl.Element(rows), pl.Element(cols)), aligned)


def _norm_rows(h, g, valid):
    y = h * lax.rsqrt(jnp.mean(h * h, axis=-1, keepdims=True) + RMS_EPS)
    return jnp.where(valid[:, 0:1] > 0, y * g, 0.0)


def _final_norm_kernel(h_ref, g_ref, o_ref):
    x = h_ref[...]
    o_ref[...] = x * lax.rsqrt(jnp.mean(x * x, axis=-1, keepdims=True) + RMS_EPS) * g_ref[...]


def _final_norm(h, g, trunk):
    d = h.shape[1]
    tr = _pick(trunk.n_tok, 256, 16)
    nr = trunk.n_tok // tr
    tok0 = trunk.base + SLOT_PAD + N_META
    out = pl.pallas_call(
        _final_norm_kernel,
        grid=(trunk.batch, nr),
        in_specs=[_window(tr, d, lambda i, r: (tok0 + i * trunk.slot + r * tr, 0)),
                  pl.BlockSpec((1, d), lambda i, r: (0, 0))],
        out_specs=pl.BlockSpec((tr, d), lambda i, r: (i * nr + r, 0)),
        out_shape=jax.ShapeDtypeStruct((trunk.batch * trunk.n_tok, d), _F32),
        compiler_params=_params(("arbitrary", "arbitrary")),
        name="final_norm",
    )(h, g.reshape(1, d))
    return out.reshape(trunk.batch, trunk.n_tok, d)


def _conv3(p_ref, k, c, r0, nr):
    at = lambda shift: p_ref[k, HALO + shift + r0:HALO + shift + r0 + nr, :]
    return c[0:1] * at(-1) + c[1:2] * at(0) + c[2:3] * at(1)


def _ffn_epilogue(vals, outs, rows):
    g, v = vals
    outs[0][rows, :] = (g * (1.0 / (1.0 + jnp.exp(-g))) * v).astype(outs[0].dtype)


def _hyena_epilogue(vals, outs, rows):
    x0, x1, v = vals
    outs[0][rows, :] = x0
    outs[1][rows, :] = v * x1


def _halo_norm_kernel(hp_ref, hc_ref, hn_ref, vp_ref, vc_ref, vn_ref, g_ref, o_ref):
    tm = hc_ref.shape[0]
    g = g_ref[...]
    o_ref[0:HALO] = _norm_rows(hp_ref[...], g, vp_ref[...]).astype(o_ref.dtype)
    o_ref[HALO:HALO + tm] = _norm_rows(hc_ref[...], g, vc_ref[...]).astype(o_ref.dtype)
    o_ref[HALO + tm:HALO + tm + HALO] = _norm_rows(hn_ref[...], g, vn_ref[...]).astype(o_ref.dtype)


def _halo_norm(h, valid, gamma, tm):
    rows, d = h.shape
    per, last = tm // HALO, rows // HALO - 1

    def panel(width):
        return [pl.BlockSpec((HALO, width), lambda i: (jnp.maximum(i * per - 1, 0), 0)),
                pl.BlockSpec((tm, width), lambda i: (i, 0)),
                pl.BlockSpec((HALO, width), lambda i: (jnp.minimum((i + 1) * per, last), 0))]

    return pl.pallas_call(
        _halo_norm_kernel,
        grid=(rows // tm,),
        in_specs=panel(d) + panel(LANES) + [pl.BlockSpec((1, d), lambda i: (0, 0))],
        out_specs=pl.BlockSpec((None, tm + 2 * HALO, d), lambda i: (i, 0, 0)),
        out_shape=jax.ShapeDtypeStruct((rows // tm, tm + 2 * HALO, d), _BF16),
        compiler_params=_params(("arbitrary",)),
        name="halo_norm",
    )(h, h, h, valid, valid, valid, gamma.reshape(1, d))


def _plain_norm_kernel(h_ref, v_ref, g_ref, o_ref):
    o_ref[...] = _norm_rows(h_ref[...], g_ref[...], v_ref[...]).astype(o_ref.dtype)


def _plain_norm(h, valid, gamma):
    rows, d = h.shape
    tr = _pick(rows, 256, 16)
    return pl.pallas_call(
        _plain_norm_kernel,
        grid=(rows // tr,),
        in_specs=[pl.BlockSpec((tr, d), lambda i: (i, 0)),
                  pl.BlockSpec((tr, LANES), lambda i: (i, 0)),
                  pl.BlockSpec((1, d), lambda i: (0, 0))],
        out_specs=pl.BlockSpec((tr, d), lambda i: (i, 0)),
        out_shape=jax.ShapeDtypeStruct((rows, d), _BF16),
        compiler_params=_params(("arbitrary",)),
        name="rmsnorm",
    )(h, valid, gamma.reshape(1, d))


def _conv_matmul_kernel(*refs, tm, nw, epilogue):
    x_ref = refs[0]
    w_refs = refs[1:1 + nw]
    c_refs = refs[1 + nw:1 + 2 * nw]
    out_refs = refs[1 + 2 * nw:-2]
    pa_ref, pb_ref = refs[-2:]
    s = pl.program_id(0)

    @pl.when(s == 0)
    def _():
        pb_ref[...] = jnp.zeros_like(pb_ref)

    def step(prev_ref, cur_ref):
        tn = cur_ref.shape[2]
        pieces = [(k, c0) for k in range(nw) for c0 in range(0, tn, MXU_COLS)]
        chunks = list(range(0, tm, EPILOGUE_ROWS))
        per_piece = -(-len(chunks) // len(pieces))
        cs = [c_refs[k][...] for k in range(nw)]
        x = x_ref[...]
        for n, (k, c0) in enumerate(pieces):
            cols = slice(c0, min(c0 + MXU_COLS, tn))
            cur_ref[k, :, cols] = jnp.dot(x, w_refs[k][:, cols], preferred_element_type=_F32)
            for r0 in chunks[n * per_piece:(n + 1) * per_piece]:
                epilogue([_conv3(prev_ref, j, cs[j], r0, EPILOGUE_ROWS) for j in range(nw)], out_refs,
                         slice(r0, r0 + EPILOGUE_ROWS))

    @pl.when(s % 2 == 0)
    def _():
        step(pb_ref, pa_ref)

    @pl.when(s % 2 == 1)
    def _():
        step(pa_ref, pb_ref)


def _conv_matmul(xp, w, w_conv, nw, tn, epilogue, out_dtypes, name):
    ni, tmh, d = xp.shape
    tm = tmh - 2 * HALO
    n = w.shape[1] // nw
    nj = n // tn
    steps = ni * nj
    cur = lambda s: jnp.minimum(s, steps - 1)
    lag = lambda s: jnp.maximum(s - 1, 0)
    wspec = lambda k: pl.BlockSpec((d, tn), lambda s: (0, cur(s) // ni + k * nj))
    cspec = lambda k: pl.BlockSpec((3, tn), lambda s: (0, lag(s) // ni + k * nj))
    out = pl.BlockSpec((tm, tn), lambda s: (lag(s) % ni, lag(s) // ni))
    return pl.pallas_call(
        functools.partial(_conv_matmul_kernel, tm=tm, nw=nw, epilogue=epilogue),
        grid=(steps + 1,),
        in_specs=[pl.BlockSpec((None, tmh, d), lambda s: (cur(s) % ni, 0, 0))]
        + [wspec(k) for k in range(nw)] + [cspec(k) for k in range(nw)],
        out_specs=[out] * len(out_dtypes),
        out_shape=[jax.ShapeDtypeStruct((ni * tm, n), dt) for dt in out_dtypes],
        scratch_shapes=[pltpu.VMEM((nw, tmh, tn), _F32), pltpu.VMEM((nw, tmh, tn), _F32)],
        compiler_params=_params(("arbitrary",)),
        name=name,
    )(xp, *([w] * nw), *([w_conv] * nw))


def _qkv_kernel(x_ref, w_ref, o_ref, *, n_q_tiles, scale):
    acc = jnp.dot(x_ref[...], w_ref[...], preferred_element_type=_F32)
    s = jnp.where(pl.program_id(0) < n_q_tiles, scale, 1.0).astype(_F32)
    o_ref[...] = (acc * s).astype(o_ref.dtype)


def _qkv(hn, w_qkv, tm):
    rows, d = hn.shape
    n = w_qkv.shape[1]
    tn = _pick(d, 1024, LANES)
    return pl.pallas_call(
        functools.partial(_qkv_kernel, n_q_tiles=d // tn, scale=HEAD_DIM ** -0.5),
        grid=(n // tn, rows // tm),
        in_specs=[pl.BlockSpec((tm, d), lambda j, i: (i, 0)),
                  pl.BlockSpec((d, tn), lambda j, i: (0, j))],
        out_specs=pl.BlockSpec((tm, tn), lambda j, i: (i, j)),
        out_shape=jax.ShapeDtypeStruct((rows, n), _BF16),
        compiler_params=_params(("arbitrary", "arbitrary")),
        name="qkv_proj",
    )(hn, w_qkv)


def _residual_kernel(x_ref, w_ref, h_ref, o_ref):
    o_ref[...] = h_ref[...] + jnp.dot(x_ref[...], w_ref[...], preferred_element_type=_F32)


def _residual_matmul(x, w, h, tm, name):
    rows, k = x.shape
    d = w.shape[1]
    weights_resident = k <= RESIDENT_WEIGHT_K
    tn = _pick(d, 1024 if weights_resident else 256, LANES)
    grid = (d // tn, rows // tm) if weights_resident else (rows // tm, d // tn)
    ij = (lambda a, b: (b, a)) if weights_resident else (lambda a, b: (a, b))
    return pl.pallas_call(
        _residual_kernel,
        grid=grid,
        in_specs=[pl.BlockSpec((tm, k), lambda a, b: (ij(a, b)[0], 0)),
                  pl.BlockSpec((k, tn), lambda a, b: (0, ij(a, b)[1])),
                  pl.BlockSpec((tm, tn), lambda a, b: ij(a, b))],
        out_specs=pl.BlockSpec((tm, tn), lambda a, b: ij(a, b)),
        out_shape=jax.ShapeDtypeStruct((rows, d), _F32),
        input_output_aliases={2: 0},
        compiler_params=_params(("arbitrary", "arbitrary")),
        name=name,
    )(x, w, h)


def _dft_matrices(seq):
    n1, n1z = _fft_sizes(seq)
    n = FFT_RADIX * n1
    n2 = jnp.arange(FFT_RADIX, dtype=jnp.int32)[:, None, None]
    k1 = jnp.arange(n1, dtype=jnp.int32)[None, :, None]
    m1 = jnp.arange(n1z, dtype=jnp.int32)[None, None, :]
    phase = ((m1 * k1 * FFT_RADIX + n2 * k1) % n).astype(_F32) * (2.0 * math.pi / n)
    g = jnp.concatenate([jnp.cos(phase), -jnp.sin(phase)], axis=1)
    g = jnp.where(m1 * FFT_RADIX + n2 < seq, g, 0.0)
    h = jnp.swapaxes(g, 1, 2) * (1.0 / n)
    return g.astype(_BF16), h.astype(_BF16)


def _plane_rows(refs, plane, n1z):
    rows = pl.ds(plane, n1z, stride=FFT_RADIX)
    return jnp.concatenate([ref[rows, :] for ref in refs], axis=1)


def _planes_per_step(g):
    r, m, k = g.shape
    return r if r * m * k * g.dtype.itemsize <= DFT_RESIDENT_BYTES else 1


def _lane_windows(n_rows, dc, row0, row_stride):
    sub = lambda l: _window(n_rows, LANES, lambda i, c, p: (row0 + i * row_stride, c * dc + l * LANES))
    return [sub(l) for l in range(dc // LANES)]


def _dft_fwd_kernel(*refs, pp):
    x_refs, g_ref, o_ref = refs[:-2], refs[-2], refs[-1]
    n1z, n1 = g_ref.shape[2], o_ref.shape[2]
    for q in range(pp):
        x = _plane_rows(x_refs, pl.program_id(2) * pp + q, n1z).astype(_BF16)
        a = jnp.dot(g_ref[q], x, preferred_element_type=_F32)
        o_ref[0, q] = a[:n1]
        o_ref[1, q] = a[n1:]


def _dft_fwd(x, g, n_sig, row0, row_stride):
    d = x.shape[1]
    r, n1z = g.shape[0], g.shape[2]
    n1 = g.shape[1] // 2
    dc = _pick(d, 256, LANES)
    pp = _planes_per_step(g)
    wins = _lane_windows(n1z * r, dc, row0, row_stride)
    return pl.pallas_call(
        functools.partial(_dft_fwd_kernel, pp=pp),
        grid=(n_sig, d // dc, r // pp),
        in_specs=wins + [pl.BlockSpec((pp, 2 * n1, n1z), lambda i, c, p: (p, 0, 0))],
        out_specs=pl.BlockSpec((2, pp, None, n1, dc), lambda i, c, p: (0, p, i, 0, c)),
        out_shape=jax.ShapeDtypeStruct((2, r, n_sig, n1, d), _F32),
        compiler_params=_params(("arbitrary", "arbitrary", "arbitrary")),
        name="dft_fwd",
    )(*([x] * len(wins)), g)


def _cmul_const(x, w):
    re, im = x
    wr, wi = float(w.real), float(w.imag)
    if abs(wi) < 1e-12:
        return (re, im) if wr > 0 else (-re, -im)
    if abs(wr) < 1e-12:
        return (-im, re) if wi > 0 else (im, -re)
    return (re * wr - im * wi, re * wi + im * wr)


def _fft_planes(xs, sign):
    n = len(xs)
    if n == 1:
        return xs
    ev = _fft_planes(xs[0::2], sign)
    od = _fft_planes(xs[1::2], sign)
    out = [None] * n
    for k in range(n // 2):
        w = complex(math.cos(2.0 * math.pi * k / n), sign * math.sin(2.0 * math.pi * k / n))
        tr, ti = _cmul_const(od[k], w)
        out[k] = (ev[k][0] + tr, ev[k][1] + ti)
        out[k + n // 2] = (ev[k][0] - tr, ev[k][1] - ti)
    return out


def _tile_loop(tk, dc, body):
    def step(r, carry):
        rows = pl.ds(pl.multiple_of(r * 8, 8), 8)
        for l in range(dc // LANES):
            body(rows, slice(l * LANES, (l + 1) * LANES))
        return carry
    lax.fori_loop(0, tk // 8, step, 0)


def _filter_spectrum_kernel(a_ref, s_ref, o_ref):
    r, tk, dc = o_ref.shape[1], o_ref.shape[2], o_ref.shape[3]

    def body(rows, lanes):
        inv = 1.0 / s_ref[:, lanes]
        xf = _fft_planes([(a_ref[0, n, 0, rows, lanes], a_ref[1, n, 0, rows, lanes]) for n in range(r)], -1)
        xb = _fft_planes([(a_ref[0, n, 1, rows, lanes], a_ref[1, n, 1, rows, lanes]) for n in range(r)], -1)
        for k in range(r):
            o_ref[0, k, rows, lanes] = (xf[k][0] + xb[k][0]) * inv
            o_ref[1, k, rows, lanes] = (xf[k][1] - xb[k][1]) * inv

    _tile_loop(tk, dc, body)


def _plane_tile(n1):
    return _pick(n1, 352, 8)


def _filter_spectrum(planes, norm):
    _, r, _, n1, d = planes.shape
    tk, dc = _plane_tile(n1), LANES
    return pl.pallas_call(
        _filter_spectrum_kernel,
        grid=(d // dc, n1 // tk),
        in_specs=[pl.BlockSpec((2, r, 2, tk, dc), lambda c, k: (0, 0, 0, k, c)),
                  pl.BlockSpec((1, dc), lambda c, k: (0, c))],
        out_specs=pl.BlockSpec((2, r, tk, dc), lambda c, k: (0, 0, k, c)),
        out_shape=jax.ShapeDtypeStruct((2, r, n1, d), _F32),
        compiler_params=_params(("arbitrary", "arbitrary")),
        name="filter_spectrum",
    )(planes, norm)


def _spectral_product_kernel(a_ref, k_ref, o_ref):
    r, tk, dc = o_ref.shape[1], o_ref.shape[2], o_ref.shape[3]

    def body(rows, lanes):
        x = _fft_planes([(a_ref[0, n, rows, lanes], a_ref[1, n, rows, lanes]) for n in range(r)], -1)
        y = []
        for k in range(r):
            kr, ki = k_ref[0, k, rows, lanes], k_ref[1, k, rows, lanes]
            y.append((x[k][0] * kr - x[k][1] * ki, x[k][0] * ki + x[k][1] * kr))
        z = _fft_planes(y, +1)
        for n in range(r):
            o_ref[0, n, rows, lanes] = z[n][0]
            o_ref[1, n, rows, lanes] = z[n][1]

    _tile_loop(tk, dc, body)


def _spectral_product(planes, kspec):
    _, r, b, n1, d = planes.shape
    tk, dc = _plane_tile(n1), LANES
    spec = pl.BlockSpec((2, r, None, tk, dc), lambda c, k, i: (0, 0, i, k, c))
    return pl.pallas_call(
        _spectral_product_kernel,
        grid=(d // dc, n1 // tk, b),
        in_specs=[spec, pl.BlockSpec((2, r, tk, dc), lambda c, k, i: (0, 0, k, c))],
        out_specs=spec,
        out_shape=jax.ShapeDtypeStruct(planes.shape, _F32),
        compiler_params=_params(("arbitrary", "arbitrary", "arbitrary")),
        name="spectral_product",
    )(planes, kspec)


def _dft_inv_kernel(*refs, pp, nsub):
    b_ref, h_ref, skip_ref, o_ref = refs[0], refs[1], refs[-2], refs[-1]
    vx_refs, x0_refs = refs[2:2 + nsub], refs[2 + nsub:2 + 2 * nsub]
    n1, dc = b_ref.shape[2], b_ref.shape[3]
    n1z = h_ref.shape[1]
    for q in range(pp):
        plane = pl.program_id(2) * pp + q
        bb = jnp.concatenate([b_ref[0, q], b_ref[1, q]], axis=0).astype(_BF16)
        y = jnp.dot(h_ref[q], bb, preferred_element_type=_F32)
        o_ref[q] = (y + _plane_rows(vx_refs, plane, n1z) * skip_ref[...]) * _plane_rows(x0_refs, plane, n1z)


def _dft_inv(planes, hmat, vx, x0, skip, row0, row_stride):
    _, r, b, n1, d = planes.shape
    n1z = hmat.shape[1]
    dc = _pick(d, 256, LANES)
    pp = _planes_per_step(hmat)
    wins = _lane_windows(n1z * r, dc, row0, row_stride)
    return pl.pallas_call(
        functools.partial(_dft_inv_kernel, pp=pp, nsub=len(wins)),
        grid=(b, d // dc, r // pp),
        in_specs=[pl.BlockSpec((2, pp, None, n1, dc), lambda i, c, p: (0, p, i, 0, c)),
                  pl.BlockSpec((pp, n1z, 2 * n1), lambda i, c, p: (p, 0, 0))]
        + wins + wins + [pl.BlockSpec((1, dc), lambda i, c, p: (0, c))],
        out_specs=pl.BlockSpec((None, pp, n1z, dc), lambda i, c, p: (i, p, 0, c)),
        out_shape=jax.ShapeDtypeStruct((b, r, n1z, d), _F32),
        compiler_params=_params(("arbitrary", "arbitrary", "arbitrary")),
        name="dft_inv",
    )(planes, hmat, *([vx] * len(wins)), *([x0] * len(wins)), skip)


def _filter_mlp_kernel(z_ref, w1_ref, wi_ref, b_ref, f_ref, o_ref):
    hi = lax.Precision.HIGHEST
    h = jnp.sin(f_ref[0:1] * (jnp.dot(z_ref[...], w1_ref[...], precision=hi, preferred_element_type=_F32) + b_ref[0:1]))
    h = jnp.sin(f_ref[1:2] * (jnp.dot(h, wi_ref[0], precision=hi, preferred_element_type=_F32) + b_ref[1:2]))
    h = jnp.sin(f_ref[2:3] * (jnp.dot(h, wi_ref[1], precision=hi, preferred_element_type=_F32) + b_ref[2:3]))
    o_ref[...] = h


def _filter_taps_kernel(hm_ref, z_ref, wf_ref, wb_ref, d_ref, k_ref, s_ref):
    hi = lax.Precision.HIGHEST
    t = z_ref[:, 0:1]
    mf = z_ref[:, HY_EMB:HY_EMB + 1]
    mb = z_ref[:, HY_EMB + 1:HY_EMB + 2]
    hm = hm_ref[...]
    kf = jnp.dot(hm, wf_ref[...], precision=hi, preferred_element_type=_F32) * jnp.exp(-t * jnp.abs(d_ref[0:1])) * mf
    kb = jnp.dot(hm, wb_ref[...], precision=hi, preferred_element_type=_F32) * jnp.exp(-t * jnp.abs(d_ref[1:2])) * mb
    k_ref[0] = kf
    k_ref[1] = kb

    @pl.when(pl.program_id(1) == 0)
    def _():
        s_ref[...] = jnp.zeros_like(s_ref)
    s_ref[...] += jnp.sum(jnp.abs(kf) + jnp.abs(kb), axis=0, keepdims=True)


def _position_features(seq, rows):
    pos = jnp.arange(seq, dtype=_F32)[:, None]
    t = pos / max(seq - 1, 1)
    f = jnp.linspace(1e-4, HY_BANDS - 1, HY_BANDS, dtype=_F32)[None, :]
    ang = f * (2.0 * math.pi / seq) * pos
    fwd = jnp.ones((seq, 1), _F32)
    bwd = (pos >= 1).astype(_F32)
    z = jnp.concatenate([t, jnp.cos(ang), -jnp.sin(ang), fwd, bwd], axis=-1)
    return jnp.pad(z, ((0, rows - seq), (0, LANES - z.shape[1])))


def _hyena_filter(seq, w1, w_inner, b, freq, w_out, delta, g):
    _, n1z = _fft_sizes(seq)
    lp = n1z * FFT_RADIX
    width = w1.shape[1]
    d = w_out.shape[1] // 2
    z = _position_features(seq, lp)
    w1p = jnp.pad(w1, ((0, LANES - w1.shape[0]), (0, 0)))
    tr = n1z
    hm = pl.pallas_call(
        _filter_mlp_kernel,
        grid=(lp // tr,),
        in_specs=[pl.BlockSpec((tr, LANES), lambda i: (i, 0)),
                  pl.BlockSpec((LANES, width), lambda i: (0, 0)),
                  pl.BlockSpec((2, width, width), lambda i: (0, 0, 0)),
                  pl.BlockSpec((3, width), lambda i: (0, 0)),
                  pl.BlockSpec((3, width), lambda i: (0, 0))],
        out_specs=pl.BlockSpec((tr, width), lambda i: (i, 0)),
        out_shape=jax.ShapeDtypeStruct((lp, width), _F32),
        compiler_params=_params(("arbitrary",)),
        name="filter_mlp",
    )(z, w1p, w_inner, b, freq)
    dc = _pick(d, 512, LANES)
    nc = d // dc
    taps, norm = pl.pallas_call(
        _filter_taps_kernel,
        grid=(nc, lp // tr),
        in_specs=[pl.BlockSpec((tr, width), lambda c, i: (i, 0)),
                  pl.BlockSpec((tr, LANES), lambda c, i: (i, 0)),
                  pl.BlockSpec((width, dc), lambda c, i: (0, c)),
                  pl.BlockSpec((width, dc), lambda c, i: (0, c + nc)),
                  pl.BlockSpec((2, dc), lambda c, i: (0, c))],
        out_specs=[pl.BlockSpec((2, tr, dc), lambda c, i: (0, i, c)), pl.BlockSpec((1, dc), lambda c, i: (0, c))],
        out_shape=[jax.ShapeDtypeStruct((2, lp, d), _F32), jax.ShapeDtypeStruct((1, d), _F32)],
        compiler_params=_params(("arbitrary", "arbitrary")),
        name="filter_taps",
    )(hm, z, w_out, w_out, delta)
    return _filter_spectrum(_dft_fwd(taps.reshape(2 * lp, d), g, 2, 0, lp), norm)


def _to_flat(parts, trunks, rows, dtype):
    d = parts[0].shape[-1]
    out = []
    for part, t in zip(parts, trunks):
        out.append(jnp.pad(part.astype(dtype), ((0, 0), (SLOT_PAD, 0), (0, 0))).reshape(t.batch * t.slot, d))
    used = sum(t.batch * t.slot for t in trunks)
    out.append(jnp.zeros((rows - used, d), dtype))
    return jnp.concatenate(out, axis=0)


def _hyena_mixer(h, valid, gamma, trunks, rows, tm, w_in, w_conv, filt, delta, skip, w_out, dft):
    d = h.shape[1]
    x0, vx = _conv_matmul(_halo_norm(h, valid, gamma, tm), w_in, w_conv, 3, _pick(d, 512, LANES),
                          _hyena_epilogue, (_F32, _F32), "hyena_in")
    parts = []
    for t in trunks:
        g, hmat = dft[t.seq]
        kspec = _hyena_filter(t.seq, *filt, delta, g)
        row0 = t.base + SLOT_PAD
        planes = _spectral_product(_dft_fwd(vx, g, t.batch, row0, t.slot), kspec)
        y = _dft_inv(planes, hmat, vx, x0, skip.reshape(1, -1), row0, t.slot)
        parts.append(jnp.swapaxes(y, 1, 2).reshape(t.batch, -1, d)[:, :t.seq])
    return _residual_matmul(_to_flat(parts, trunks, rows, _BF16), w_out, h, tm, "hyena_out")


def _attention_bias(rpb, rows_in_grid):
    n_blocks = rows_in_grid // Q_ROWS
    kr_win = min(WIN_ROWS, rows_in_grid)
    qr = np.arange(Q_ROWS)[:, None, None]
    kb = np.arange(3)[None, :, None]
    kr = np.arange(Q_ROWS)[None, None, :]
    dr_idx, row_ok = [], []
    for g in (0, 1, n_blocks - 1):
        r = Q_ROWS * g + qr
        key_row = Q_ROWS * (g + kb - 1) + kr
        rs = np.clip(r - kr_win // 2, 0, rows_in_grid - kr_win)
        ok = (key_row >= rs) & (key_row < rs + kr_win) & (g + kb - 1 >= 0) & (g + kb - 1 < n_blocks)
        dr_idx.append(np.clip(key_row - r + (WIN_ROWS - 1), 0, 2 * WIN_ROWS - 2))
        row_ok.append(ok)
    dr_idx = np.stack(dr_idx)
    row_ok = np.stack(row_ok)
    cols = np.arange(GRID_W)
    col_start = np.clip(cols - WIN_COLS // 2, 0, GRID_W - WIN_COLS)
    col_ok = (cols[None, :] >= col_start[:, None]) & (cols[None, :] < col_start[:, None] + WIN_COLS)
    col_idx = np.clip(cols[None, :] - cols[:, None] + WIN_COLS - 1, 0, 2 * WIN_COLS - 2)
    by_col = rpb[:, :, col_idx]
    by_row = by_col[:, dr_idx]
    ok = row_ok[:, :, :, :, None, None] & col_ok[None, None, None, None]
    bias = jnp.where(ok[None], by_row, MASK_VALUE)
    bias = jnp.transpose(bias, (1, 0, 2, 5, 3, 4, 6))
    return bias.reshape(3, rpb.shape[0], Q_BLOCK, 3 * Q_BLOCK).astype(_F32)


def _na_kernel(q_ref, k0_ref, k1_ref, k2_ref, v0_ref, v1_ref, v2_ref, km_ref, vm_ref, bias_ref, o_ref, *, heads):
    nt = (((1,), (1,)), ((), ()))
    for h in range(heads):
        hs = slice(h * HEAD_DIM, (h + 1) * HEAD_DIM)
        q = q_ref[:, hs]
        s = [lax.dot_general(q, k_ref[:, hs], nt, preferred_element_type=_F32)
             + bias_ref[h, :, i * Q_BLOCK:(i + 1) * Q_BLOCK]
             for i, k_ref in enumerate((k0_ref, k1_ref, k2_ref))]
        sm = lax.dot_general(q, km_ref[:, hs], nt, preferred_element_type=_F32)
        m = jnp.max(sm, axis=-1, keepdims=True)
        for si in s:
            m = jnp.maximum(m, jnp.max(si, axis=-1, keepdims=True))
        pm = jnp.exp(sm - m)
        den = jnp.sum(pm, axis=-1, keepdims=True)
        acc = jnp.dot(pm.astype(_BF16), vm_ref[:, hs], preferred_element_type=_F32)
        for si, v_ref in zip(s, (v0_ref, v1_ref, v2_ref)):
            p = jnp.exp(si - m)
            den = den + jnp.sum(p, axis=-1, keepdims=True)
            acc = acc + jnp.dot(p.astype(_BF16), v_ref[:, hs], preferred_element_type=_F32)
        o_ref[:, hs] = (acc / den).astype(o_ref.dtype)


def _na_meta_kernel(q_ref, k_ref, v_ref, o_ref, *, heads):
    nt = (((1,), (1,)), ((), ()))
    for h in range(heads):
        hs = slice(h * HEAD_DIM, (h + 1) * HEAD_DIM)
        s = lax.dot_general(q_ref[:, hs], k_ref[:, hs], nt, preferred_element_type=_F32)
        p = jnp.exp(s - jnp.max(s, axis=-1, keepdims=True))
        acc = jnp.dot(p.astype(_BF16), v_ref[:, hs], preferred_element_type=_F32)
        o_ref[:, hs] = (acc / jnp.sum(p, axis=-1, keepdims=True)).astype(o_ref.dtype)


def _neighbourhood_attention(qkv, trunk, bias):
    d = qkv.shape[1] // 3
    n_heads = d // HEAD_DIM
    hb = min(4, n_heads)
    wd = hb * HEAD_DIM
    nb = trunk.n_tok // Q_BLOCK
    b = trunk.batch
    tok0 = trunk.base + SLOT_PAD + N_META
    meta0 = (trunk.base + SLOT_PAD) // N_META
    slot_meta = trunk.slot // N_META

    def win(col0, shift):
        return _window(Q_BLOCK, wd, lambda h, i, g: (
            tok0 + i * trunk.slot + jnp.clip(g + shift, 0, nb - 1) * Q_BLOCK, col0 + h * wd))

    def meta(col0):
        return _window(N_META, wd, lambda h, i, g: (trunk.base + SLOT_PAD + i * trunk.slot, col0 + h * wd))

    variant = lambda g: jnp.where(g == 0, 0, jnp.where(g == nb - 1, 2, 1))
    out_grid = pl.pallas_call(
        functools.partial(_na_kernel, heads=hb),
        grid=(n_heads // hb, b, nb),
        in_specs=[win(0, 0), win(d, -1), win(d, 0), win(d, 1), win(2 * d, -1), win(2 * d, 0), win(2 * d, 1),
                  meta(d), meta(2 * d),
                  pl.BlockSpec((None, hb, Q_BLOCK, 3 * Q_BLOCK), lambda h, i, g: (variant(g), h, 0, 0))],
        out_specs=pl.BlockSpec((Q_BLOCK, wd), lambda h, i, g: (i * nb + g, h)),
        out_shape=jax.ShapeDtypeStruct((b * trunk.n_tok, d), _BF16),
        compiler_params=_params(("arbitrary", "arbitrary", "arbitrary")),
        name="na_grid",
    )(qkv, qkv, qkv, qkv, qkv, qkv, qkv, qkv, qkv, bias)
    mspec = lambda k: pl.BlockSpec((N_META, d), lambda i: (meta0 + i * slot_meta, k))
    out_meta = pl.pallas_call(
        functools.partial(_na_meta_kernel, heads=n_heads),
        grid=(b,),
        in_specs=[mspec(0), mspec(1), mspec(2)],
        out_specs=pl.BlockSpec((N_META, d), lambda i: (i, 0)),
        out_shape=jax.ShapeDtypeStruct((b * N_META, d), _BF16),
        compiler_params=_params(("arbitrary",)),
        name="na_meta",
    )(qkv, qkv, qkv)
    return jnp.concatenate([out_meta.reshape(b, N_META, d), out_grid.reshape(b, trunk.n_tok, d)], axis=1)


def _na_mixer(h, valid, gamma, trunks, rows, tm, w_qkv, rpb, w_out):
    qkv = _qkv(_plain_norm(h, valid, gamma), w_qkv, tm)
    biases = {}
    parts = []
    for t in trunks:
        grid_rows = t.n_tok // GRID_W
        key = grid_rows if grid_rows < 3 * Q_ROWS else -1
        if key not in biases:
            biases[key] = _attention_bias(rpb, grid_rows)
        parts.append(_neighbourhood_attention(qkv, t, biases[key]))
    return _residual_matmul(_to_flat(parts, trunks, rows, _BF16), w_out, h, tm, "na_out")


def _valid_rows(trunks, rows):
    r = jnp.arange(rows, dtype=jnp.int32)[:, None]
    v = jnp.zeros((rows, 1), jnp.bool_)
    for t in trunks:
        inside = (r >= t.base) & (r < t.base + t.batch * t.slot)
        v = v | (inside & ((r - t.base) % t.slot >= SLOT_PAD))
    return jnp.broadcast_to(v.astype(_F32), (rows, LANES))


def _padded_ffn_weights(w_up, w_conv, w_down, tn):
    f = w_down.shape[0]
    pad = _round_up(f, tn) - f
    halves = lambda a: jnp.concatenate([jnp.pad(a[:, :f], ((0, 0), (0, pad))), jnp.pad(a[:, f:], ((0, 0), (0, pad)))], axis=1)
    return halves(w_up).astype(_BF16), halves(w_conv), jnp.pad(w_down, ((0, pad), (0, 0))).astype(_BF16)


def kernel(x_prompt, x_sample, meta_tokens, norm_mix, norm_ffn, norm_final, hy_w_in, hy_w_conv, hy_f_w1,
           hy_f_w_inner, hy_f_b, hy_f_freq, hy_f_w_out, hy_delta, hy_skip, hy_w_out, na_w_qkv, na_rpb,
           na_w_out, ffn_w_up, ffn_w_conv, ffn_w_down):
    d = x_prompt.shape[-1]
    depth = norm_mix.shape[0]
    xs = (x_prompt, x_sample)
    tm = _TM
    trunks, rows = _plan([(x.shape[0], x.shape[1]) for x in xs], tm)
    valid = _valid_rows(trunks, rows)
    dft = {t.seq: _dft_matrices(t.seq) for t in trunks}
    ffn_tn = _pick(_round_up(ffn_w_down.shape[1], _FFN_TN), _FFN_TN, LANES)

    meta = meta_tokens.astype(_F32)
    seqs = [jnp.concatenate([jnp.broadcast_to(meta[None], (x.shape[0], N_META, d)), x], axis=1) for x in xs]
    h = _to_flat(seqs, trunks, rows, _F32)

    bf = lambda w: w.astype(_BF16)
    for i in range(depth):
        j = i // 2
        if i % 2 == 0:
            filt = (hy_f_w1[j], hy_f_w_inner[j], hy_f_b[j], hy_f_freq[j], hy_f_w_out[j])
            h = _hyena_mixer(h, valid, norm_mix[i], trunks, rows, tm, bf(hy_w_in[j]), hy_w_conv[j], filt,
                             hy_delta[j], hy_skip[j], bf(hy_w_out[j]), dft)
        else:
            h = _na_mixer(h, valid, norm_mix[i], trunks, rows, tm, bf(na_w_qkv[j]), na_rpb[j], bf(na_w_out[j]))
        w_up, w_conv, w_down = _padded_ffn_weights(ffn_w_up[i], ffn_w_conv[i], ffn_w_down[i], ffn_tn)
        a = _conv_matmul(_halo_norm(h, valid, norm_ffn[i], tm), w_up, w_conv, 2, ffn_tn, _ffn_epilogue,
                         (_BF16,), "ffn_up")[0]
        h = _residual_matmul(a, w_down, h, tm, "ffn_down")
    return tuple(_final_norm(h, norm_final, t) for t in trunks)
```

```python
import functools
import math
from typing import NamedTuple

import numpy as np
import jax
import jax.numpy as jnp
from jax import lax
from jax.experimental import pallas as pl
from jax.experimental.pallas import tpu as pltpu

N_META = 16
GRID_W = 64
HEAD_DIM = 128
WIN_ROWS = 8
WIN_COLS = 16
HY_EMB = 33
HY_BANDS = (HY_EMB - 1) // 2
RMS_EPS = 1e-6

SLOT_PAD = 16
HALO = 16
EPILOGUE_ROWS = 32
MXU_COLS = 256
RESIDENT_WEIGHT_K = 4096
FFT_RADIX = 16
Q_ROWS = 4
Q_BLOCK = Q_ROWS * GRID_W
MASK_VALUE = -1e30
LANES = 128
V7X_VMEM_LIMIT = 56 * 1024 * 1024
DFT_BLOCK_BYTES = 10 * 1024 * 1024
TWIDDLE_LANES = 8

_TM = 640
_FFN_TN = 512
_F32 = jnp.float32
_BF16 = jnp.bfloat16


class _Trunk(NamedTuple):
    batch: int
    n_tok: int
    seq: int
    slot: int
    base: int


def _round_up(x, m):
    return (x + m - 1) // m * m


def _pick(n, target, mult):
    best = None
    for d in range(mult, min(n, target) + 1, mult):
        if n % d == 0:
            best = d
    return best if best is not None else n


def _fft_sizes(seq):
    n1 = _round_up(-(-(2 * seq - 1) // FFT_RADIX), 8)
    n1z = _round_up(-(-seq // FFT_RADIX), 8)
    return n1, n1z


def _plan(shapes, tm):
    trunks, base = [], 0
    for batch, n_tok in shapes:
        seq = N_META + n_tok
        slot = SLOT_PAD + seq
        trunks.append(_Trunk(batch, n_tok, seq, slot, base))
        base += batch * slot
    reach = max(t.base + (t.batch - 1) * t.slot + SLOT_PAD + FFT_RADIX * _fft_sizes(t.seq)[1] for t in trunks)
    rows = _round_up(max(base + SLOT_PAD, reach), tm)
    return trunks, rows


def _params(sem, vmem=V7X_VMEM_LIMIT):
    return pltpu.CompilerParams(dimension_semantics=sem, vmem_limit_bytes=vmem)


def _window(rows, cols, index):
    def aligned(*args):
        r, c = index(*args)
        hint = lambda x, m: x if isinstance(x, int) else pl.multiple_of(x, m)
        return hint(r, SLOT_PAD), hint(c, LANES)
    return pl.BlockSpec((pl.Element(rows), pl.Element(cols)), aligned)


def _norm_rows(h, g, valid):
    y = h * lax.rsqrt(jnp.mean(h * h, axis=-1, keepdims=True) + RMS_EPS)
    return jnp.where(valid[:, 0:1] > 0, y * g, 0.0)


def _final_norm_kernel(h_ref, g_ref, o_ref):
    x = h_ref[...]
    o_ref[...] = x * lax.rsqrt(jnp.mean(x * x, axis=-1, keepdims=True) + RMS_EPS) * g_ref[...]


def _final_norm(h, g, trunk):
    d = h.shape[1]
    tr = _pick(trunk.n_tok, 256, 16)
    nr = trunk.n_tok // tr
    tok0 = trunk.base + SLOT_PAD + N_META
    out = pl.pallas_call(
        _final_norm_kernel,
        grid=(trunk.batch, nr),
        in_specs=[_window(tr, d, lambda i, r: (tok0 + i * trunk.slot + r * tr, 0)),
                  pl.BlockSpec((1, d), lambda i, r: (0, 0))],
        out_specs=pl.BlockSpec((tr, d), lambda i, r: (i * nr + r, 0)),
        out_shape=jax.ShapeDtypeStruct((trunk.batch * trunk.n_tok, d), _F32),
        compiler_params=_params(("arbitrary", "arbitrary")),
        name="final_norm",
    )(h, g.reshape(1, d))
    return out.reshape(trunk.batch, trunk.n_tok, d)


def _conv3(p_ref, k, c, r0, nr):
    at = lambda shift: p_ref[k, HALO + shift + r0:HALO + shift + r0 + nr, :]
    return c[0:1] * at(-1) + c[1:2] * at(0) + c[2:3] * at(1)


def _ffn_epilogue(vals, outs, rows):
    g, v = vals
    outs[0][rows, :] = (g * (1.0 / (1.0 + jnp.exp(-g))) * v).astype(outs[0].dtype)


def _hyena_epilogue(vals, outs, rows):
    x0, x1, v = vals
    outs[0][rows, :] = x0
    outs[1][rows, :] = v * x1


def _halo_norm_kernel(hp_ref, hc_ref, hn_ref, vp_ref, vc_ref, vn_ref, g_ref, o_ref):
    g = g_ref[...]
    packed = lambda h_ref, v_ref: pltpu.bitcast(_norm_rows(h_ref[...], g, v_ref[...]).astype(_BF16), jnp.uint32)
    tm2 = hc_ref.shape[0] // 2
    o_ref[0:HALO // 2] = packed(hp_ref, vp_ref)
    o_ref[HALO // 2:HALO // 2 + tm2] = packed(hc_ref, vc_ref)
    o_ref[HALO // 2 + tm2:HALO + tm2] = packed(hn_ref, vn_ref)


def _halo_norm(h, valid, gamma, tm):
    rows, d = h.shape
    per, last = tm // HALO, rows // HALO - 1

    def panel(width):
        return [pl.BlockSpec((HALO, width), lambda i: (jnp.maximum(i * per - 1, 0), 0)),
                pl.BlockSpec((tm, width), lambda i: (i, 0)),
                pl.BlockSpec((HALO, width), lambda i: (jnp.minimum((i + 1) * per, last), 0))]

    return pl.pallas_call(
        _halo_norm_kernel,
        grid=(rows // tm,),
        in_specs=panel(d) + panel(LANES) + [pl.BlockSpec((1, d), lambda i: (0, 0))],
        out_specs=pl.BlockSpec((None, tm // 2 + HALO, d), lambda i: (i, 0, 0)),
        out_shape=jax.ShapeDtypeStruct((rows // tm, tm // 2 + HALO, d), jnp.uint32),
        compiler_params=_params(("arbitrary",)),
        name="halo_norm",
    )(h, h, h, valid, valid, valid, gamma.reshape(1, d))


def _plain_norm_kernel(h_ref, v_ref, g_ref, o_ref):
    o_ref[...] = _norm_rows(h_ref[...], g_ref[...], v_ref[...]).astype(o_ref.dtype)


def _plain_norm(h, valid, gamma):
    rows, d = h.shape
    tr = _pick(rows, 256, 16)
    return pl.pallas_call(
        _plain_norm_kernel,
        grid=(rows // tr,),
        in_specs=[pl.BlockSpec((tr, d), lambda i: (i, 0)),
                  pl.BlockSpec((tr, LANES), lambda i: (i, 0)),
                  pl.BlockSpec((1, d), lambda i: (0, 0))],
        out_specs=pl.BlockSpec((tr, d), lambda i: (i, 0)),
        out_shape=jax.ShapeDtypeStruct((rows, d), _BF16),
        compiler_params=_params(("arbitrary",)),
        name="rmsnorm",
    )(h, valid, gamma.reshape(1, d))


def _conv_matmul_kernel(*refs, tm, nw, epilogue):
    x_ref = refs[0]
    w_refs = refs[1:1 + nw]
    c_refs = refs[1 + nw:1 + 2 * nw]
    out_refs = refs[1 + 2 * nw:-2]
    pa_ref, pb_ref = refs[-2:]
    s = pl.program_id(0)

    @pl.when(s == 0)
    def _():
        pb_ref[...] = jnp.zeros_like(pb_ref)

    def step(prev_ref, cur_ref):
        tn = cur_ref.shape[2]
        pieces = [(k, c0) for k in range(nw) for c0 in range(0, tn, MXU_COLS)]
        chunks = list(range(0, tm, EPILOGUE_ROWS))
        per_piece = -(-len(chunks) // len(pieces))
        cs = [c_refs[k][...] for k in range(nw)]
        x = pltpu.bitcast(x_ref[...], _BF16)
        for n, (k, c0) in enumerate(pieces):
            cols = slice(c0, min(c0 + MXU_COLS, tn))
            cur_ref[k, :, cols] = jnp.dot(x, w_refs[k][:, cols], preferred_element_type=_F32)
            for r0 in chunks[n * per_piece:(n + 1) * per_piece]:
                epilogue([_conv3(prev_ref, j, cs[j], r0, EPILOGUE_ROWS) for j in range(nw)], out_refs,
                         slice(r0, r0 + EPILOGUE_ROWS))

    @pl.when(s % 2 == 0)
    def _():
        step(pb_ref, pa_ref)

    @pl.when(s % 2 == 1)
    def _():
        step(pa_ref, pb_ref)


def _conv_matmul(xp, w, w_conv, nw, tn, epilogue, out_dtypes, name):
    ni, packed_rows, d = xp.shape
    tmh = 2 * packed_rows
    tm = tmh - 2 * HALO
    n = w.shape[1] // nw
    nj = n // tn
    steps = ni * nj
    cur = lambda s: jnp.minimum(s, steps - 1)
    lag = lambda s: jnp.maximum(s - 1, 0)
    wspec = lambda k: pl.BlockSpec((d, tn), lambda s: (0, cur(s) // ni + k * nj))
    cspec = lambda k: pl.BlockSpec((3, tn), lambda s: (0, lag(s) // ni + k * nj))
    out = pl.BlockSpec((tm, tn), lambda s: (lag(s) % ni, lag(s) // ni))
    return pl.pallas_call(
        functools.partial(_conv_matmul_kernel, tm=tm, nw=nw, epilogue=epilogue),
        grid=(steps + 1,),
        in_specs=[pl.BlockSpec((None, packed_rows, d), lambda s: (cur(s) % ni, 0, 0))]
        + [wspec(k) for k in range(nw)] + [cspec(k) for k in range(nw)],
        out_specs=[out] * len(out_dtypes),
        out_shape=[jax.ShapeDtypeStruct((ni * tm, n), dt) for dt in out_dtypes],
        scratch_shapes=[pltpu.VMEM((nw, tmh, tn), _F32), pltpu.VMEM((nw, tmh, tn), _F32)],
        compiler_params=_params(("arbitrary",)),
        name=name,
    )(xp, *([w] * nw), *([w_conv] * nw))


def _qkv_kernel(x_ref, w_ref, o_ref, *, n_q_tiles, scale):
    acc = jnp.dot(x_ref[...], w_ref[...], preferred_element_type=_F32)
    s = jnp.where(pl.program_id(0) < n_q_tiles, scale, 1.0).astype(_F32)
    o_ref[...] = (acc * s).astype(o_ref.dtype)


def _qkv(hn, w_qkv, tm):
    rows, d = hn.shape
    n = w_qkv.shape[1]
    tn = _pick(d, 1024, LANES)
    return pl.pallas_call(
        functools.partial(_qkv_kernel, n_q_tiles=d // tn, scale=HEAD_DIM ** -0.5),
        grid=(n // tn, rows // tm),
        in_specs=[pl.BlockSpec((tm, d), lambda j, i: (i, 0)),
                  pl.BlockSpec((d, tn), lambda j, i: (0, j))],
        out_specs=pl.BlockSpec((tm, tn), lambda j, i: (i, j)),
        out_shape=jax.ShapeDtypeStruct((rows, n), _BF16),
        compiler_params=_params(("arbitrary", "arbitrary")),
        name="qkv_proj",
    )(hn, w_qkv)


def _residual_kernel(x_ref, w_ref, h_ref, o_ref):
    o_ref[...] = h_ref[...] + jnp.dot(x_ref[...], w_ref[...], preferred_element_type=_F32)


def _residual_matmul(x, w, h, tm, name):
    rows, k = x.shape
    d = w.shape[1]
    weights_resident = k <= RESIDENT_WEIGHT_K
    tn = _pick(d, 1024 if weights_resident else 256, LANES)
    grid = (d // tn, rows // tm) if weights_resident else (rows // tm, d // tn)
    ij = (lambda a, b: (b, a)) if weights_resident else (lambda a, b: (a, b))
    return pl.pallas_call(
        _residual_kernel,
        grid=grid,
        in_specs=[pl.BlockSpec((tm, k), lambda a, b: (ij(a, b)[0], 0)),
                  pl.BlockSpec((k, tn), lambda a, b: (0, ij(a, b)[1])),
                  pl.BlockSpec((tm, tn), lambda a, b: ij(a, b))],
        out_specs=pl.BlockSpec((tm, tn), lambda a, b: ij(a, b)),
        out_shape=jax.ShapeDtypeStruct((rows, d), _F32),
        input_output_aliases={2: 0},
        compiler_params=_params(("arbitrary", "arbitrary")),
        name=name,
    )(x, w, h)


class _Dft(NamedTuple):
    fwd: jax.Array
    inv: jax.Array
    twiddle: jax.Array


def _dft_matrices(seq):
    assert seq % FFT_RADIX == 0
    n1, n1z = _fft_sizes(seq)
    n = FFT_RADIX * n1
    k1 = jnp.arange(n1, dtype=jnp.int32)[:, None]
    m1 = jnp.arange(n1z, dtype=jnp.int32)[None, :]
    phase = ((m1 * k1) % n1).astype(_F32) * (2.0 * math.pi / n1)
    f = jnp.concatenate([jnp.cos(phase), -jnp.sin(phase)], axis=0)
    f = jnp.where(m1 * FFT_RADIX < seq, f, 0.0)
    p = jnp.arange(FFT_RADIX, dtype=jnp.int32)[:, None]
    tphase = (p * k1[:, 0][None, :]).astype(_F32) * (2.0 * math.pi / n)
    tw = jnp.stack([jnp.cos(tphase), -jnp.sin(tphase)], axis=1)
    tw = jnp.broadcast_to(tw[..., None], tw.shape + (TWIDDLE_LANES,))
    return _Dft(f.astype(_BF16), (f.T * (1.0 / n)).astype(_BF16), tw)


def _plane_rows(refs, plane, n1z):
    rows = pl.ds(plane, n1z, stride=FFT_RADIX)
    return jnp.concatenate([ref[rows, :] for ref in refs], axis=1)


def _planes_per_step(n1, dc):
    plane_block = 2 * FFT_RADIX * n1 * dc * 4
    return FFT_RADIX if plane_block <= DFT_BLOCK_BYTES else 1


def _lane_windows(n_rows, dc, row0, row_stride):
    sub = lambda l: _window(n_rows, LANES, lambda i, c, p: (row0 + i * row_stride, c * dc + l * LANES))
    return [sub(l) for l in range(dc // LANES)]


def _dft_fwd_kernel(*refs, pp):
    x_refs, f_ref, t_ref, o_ref = refs[:-3], refs[-3], refs[-2], refs[-1]
    n1z, n1 = f_ref.shape[1], o_ref.shape[2]
    for q in range(pp):
        x = _plane_rows(x_refs, pl.program_id(2) * pp + q, n1z).astype(_BF16)
        a = jnp.dot(f_ref[...], x, preferred_element_type=_F32)
        ar, ai = a[:n1], a[n1:]
        tr, ti = t_ref[q, 0, :, 0:1], t_ref[q, 1, :, 0:1]
        o_ref[0, q] = ar * tr - ai * ti
        o_ref[1, q] = ar * ti + ai * tr


def _dft_fwd(x, dft, n_sig, row0, row_stride):
    d = x.shape[1]
    r = FFT_RADIX
    n1, n1z = dft.fwd.shape[0] // 2, dft.fwd.shape[1]
    dc = _pick(d, 256, LANES)
    pp = _planes_per_step(n1, dc)
    wins = _lane_windows(n1z * r, dc, row0, row_stride)
    return pl.pallas_call(
        functools.partial(_dft_fwd_kernel, pp=pp),
        grid=(n_sig, d // dc, r // pp),
        in_specs=wins + [pl.BlockSpec((2 * n1, n1z), lambda i, c, p: (0, 0)),
                         pl.BlockSpec((pp, 2, n1, TWIDDLE_LANES), lambda i, c, p: (p, 0, 0, 0))],
        out_specs=pl.BlockSpec((2, pp, None, n1, dc), lambda i, c, p: (0, p, i, 0, c)),
        out_shape=jax.ShapeDtypeStruct((2, r, n_sig, n1, d), _F32),
        compiler_params=_params(("arbitrary", "arbitrary", "arbitrary")),
        name="dft_fwd",
    )(*([x] * len(wins)), dft.fwd, dft.twiddle)


def _cmul_const(x, w):
    re, im = x
    wr, wi = float(w.real), float(w.imag)
    if abs(wi) < 1e-12:
        return (re, im) if wr > 0 else (-re, -im)
    if abs(wr) < 1e-12:
        return (-im, re) if wi > 0 else (im, -re)
    return (re * wr - im * wi, re * wi + im * wr)


def _fft_planes(xs, sign):
    n = len(xs)
    if n == 1:
        return xs
    ev = _fft_planes(xs[0::2], sign)
    od = _fft_planes(xs[1::2], sign)
    out = [None] * n
    for k in range(n // 2):
        w = complex(math.cos(2.0 * math.pi * k / n), sign * math.sin(2.0 * math.pi * k / n))
        tr, ti = _cmul_const(od[k], w)
        out[k] = (ev[k][0] + tr, ev[k][1] + ti)
        out[k + n // 2] = (ev[k][0] - tr, ev[k][1] - ti)
    return out


def _tile_loop(tk, dc, body):
    def step(r, carry):
        rows = pl.ds(pl.multiple_of(r * 8, 8), 8)
        for l in range(dc // LANES):
            body(rows, slice(l * LANES, (l + 1) * LANES))
        return carry
    lax.fori_loop(0, tk // 8, step, 0)


def _filter_spectrum_kernel(a_ref, s_ref, o_ref):
    r, tk, dc = o_ref.shape[1], o_ref.shape[2], o_ref.shape[3]

    def body(rows, lanes):
        inv = 1.0 / s_ref[:, lanes]
        xf = _fft_planes([(a_ref[0, n, 0, rows, lanes], a_ref[1, n, 0, rows, lanes]) for n in range(r)], -1)
        xb = _fft_planes([(a_ref[0, n, 1, rows, lanes], a_ref[1, n, 1, rows, lanes]) for n in range(r)], -1)
        for k in range(r):
            o_ref[0, k, rows, lanes] = (xf[k][0] + xb[k][0]) * inv
            o_ref[1, k, rows, lanes] = (xf[k][1] - xb[k][1]) * inv

    _tile_loop(tk, dc, body)


def _plane_tile(n1):
    return _pick(n1, 352, 8)


def _filter_spectrum(planes, norm):
    _, r, _, n1, d = planes.shape
    tk, dc = _plane_tile(n1), LANES
    return pl.pallas_call(
        _filter_spectrum_kernel,
        grid=(d // dc, n1 // tk),
        in_specs=[pl.BlockSpec((2, r, 2, tk, dc), lambda c, k: (0, 0, 0, k, c)),
                  pl.BlockSpec((1, dc), lambda c, k: (0, c))],
        out_specs=pl.BlockSpec((2, r, tk, dc), lambda c, k: (0, 0, k, c)),
        out_shape=jax.ShapeDtypeStruct((2, r, n1, d), _F32),
        compiler_params=_params(("arbitrary", "arbitrary")),
        name="filter_spectrum",
    )(planes, norm)


def _spectral_product_kernel(a_ref, k_ref, o_ref):
    r, tk, dc = o_ref.shape[1], o_ref.shape[2], o_ref.shape[3]

    def body(rows, lanes):
        x = _fft_planes([(a_ref[0, n, rows, lanes], a_ref[1, n, rows, lanes]) for n in range(r)], -1)
        y = []
        for k in range(r):
            kr, ki = k_ref[0, k, rows, lanes], k_ref[1, k, rows, lanes]
            y.append((x[k][0] * kr - x[k][1] * ki, x[k][0] * ki + x[k][1] * kr))
        z = _fft_planes(y, +1)
        for n in range(r):
            o_ref[0, n, rows, lanes] = z[n][0]
            o_ref[1, n, rows, lanes] = z[n][1]

    _tile_loop(tk, dc, body)


def _spectral_product(planes, kspec):
    _, r, b, n1, d = planes.shape
    tk, dc = _plane_tile(n1), LANES
    spec = pl.BlockSpec((2, r, None, tk, dc), lambda c, k, i: (0, 0, i, k, c))
    return pl.pallas_call(
        _spectral_product_kernel,
        grid=(d // dc, n1 // tk, b),
        in_specs=[spec, pl.BlockSpec((2, r, tk, dc), lambda c, k, i: (0, 0, k, c))],
        out_specs=spec,
        out_shape=jax.ShapeDtypeStruct(planes.shape, _F32),
        compiler_params=_params(("arbitrary", "arbitrary", "arbitrary")),
        name="spectral_product",
    )(planes, kspec)


def _dft_inv_kernel(*refs, pp, nsub):
    b_ref, h_ref, t_ref, skip_ref, o_ref = refs[0], refs[1], refs[2], refs[-2], refs[-1]
    vx_refs, x0_refs = refs[3:3 + nsub], refs[3 + nsub:3 + 2 * nsub]
    n1z = h_ref.shape[0]
    for q in range(pp):
        plane = pl.program_id(2) * pp + q
        br, bi = b_ref[0, q], b_ref[1, q]
        tr, ti = t_ref[q, 0, :, 0:1], t_ref[q, 1, :, 0:1]
        bb = jnp.concatenate([br * tr + bi * ti, bi * tr - br * ti], axis=0).astype(_BF16)
        y = jnp.dot(h_ref[...], bb, preferred_element_type=_F32)
        o_ref[q] = (y + _plane_rows(vx_refs, plane, n1z) * skip_ref[...]) * _plane_rows(x0_refs, plane, n1z)


def _dft_inv(planes, dft, vx, x0, skip, row0, row_stride):
    _, r, b, n1, d = planes.shape
    n1z = dft.inv.shape[0]
    dc = _pick(d, 256, LANES)
    pp = _planes_per_step(n1, dc)
    wins = _lane_windows(n1z * r, dc, row0, row_stride)
    return pl.pallas_call(
        functools.partial(_dft_inv_kernel, pp=pp, nsub=len(wins)),
        grid=(b, d // dc, r // pp),
        in_specs=[pl.BlockSpec((2, pp, None, n1, dc), lambda i, c, p: (0, p, i, 0, c)),
                  pl.BlockSpec((n1z, 2 * n1), lambda i, c, p: (0, 0)),
                  pl.BlockSpec((pp, 2, n1, TWIDDLE_LANES), lambda i, c, p: (p, 0, 0, 0))]
        + wins + wins + [pl.BlockSpec((1, dc), lambda i, c, p: (0, c))],
        out_specs=pl.BlockSpec((None, pp, n1z, dc), lambda i, c, p: (i, p, 0, c)),
        out_shape=jax.ShapeDtypeStruct((b, r, n1z, d), _F32),
        compiler_params=_params(("arbitrary", "arbitrary", "arbitrary")),
        name="dft_inv",
    )(planes, dft.inv, dft.twiddle, *([vx] * len(wins)), *([x0] * len(wins)), skip)


def _filter_mlp_kernel(z_ref, w1_ref, wi_ref, b_ref, f_ref, o_ref):
    hi = lax.Precision.HIGHEST
    h = jnp.sin(f_ref[0:1] * (jnp.dot(z_ref[...], w1_ref[...], precision=hi, preferred_element_type=_F32) + b_ref[0:1]))
    h = jnp.sin(f_ref[1:2] * (jnp.dot(h, wi_ref[0], precision=hi, preferred_element_type=_F32) + b_ref[1:2]))
    h = jnp.sin(f_ref[2:3] * (jnp.dot(h, wi_ref[1], precision=hi, preferred_element_type=_F32) + b_ref[2:3]))
    o_ref[...] = h


def _filter_taps_kernel(hm_ref, z_ref, wf_ref, wb_ref, d_ref, k_ref, s_ref):
    hi = lax.Precision.HIGHEST
    t = z_ref[:, 0:1]
    mf = z_ref[:, HY_EMB:HY_EMB + 1]
    mb = z_ref[:, HY_EMB + 1:HY_EMB + 2]
    hm = hm_ref[...]
    kf = jnp.dot(hm, wf_ref[...], precision=hi, preferred_element_type=_F32) * jnp.exp(-t * jnp.abs(d_ref[0:1])) * mf
    kb = jnp.dot(hm, wb_ref[...], precision=hi, preferred_element_type=_F32) * jnp.exp(-t * jnp.abs(d_ref[1:2])) * mb
    k_ref[0] = kf
    k_ref[1] = kb

    @pl.when(pl.program_id(1) == 0)
    def _():
        s_ref[...] = jnp.zeros_like(s_ref)
    s_ref[...] += jnp.sum(jnp.abs(kf) + jnp.abs(kb), axis=0, keepdims=True)


def _position_features(seq, rows):
    pos = jnp.arange(seq, dtype=_F32)[:, None]
    t = pos / max(seq - 1, 1)
    f = jnp.linspace(1e-4, HY_BANDS - 1, HY_BANDS, dtype=_F32)[None, :]
    ang = f * (2.0 * math.pi / seq) * pos
    fwd = jnp.ones((seq, 1), _F32)
    bwd = (pos >= 1).astype(_F32)
    z = jnp.concatenate([t, jnp.cos(ang), -jnp.sin(ang), fwd, bwd], axis=-1)
    return jnp.pad(z, ((0, rows - seq), (0, LANES - z.shape[1])))


def _hyena_filter(seq, w1, w_inner, b, freq, w_out, delta, dft):
    _, n1z = _fft_sizes(seq)
    lp = n1z * FFT_RADIX
    width = w1.shape[1]
    d = w_out.shape[1] // 2
    z = _position_features(seq, lp)
    w1p = jnp.pad(w1, ((0, LANES - w1.shape[0]), (0, 0)))
    tr = n1z
    hm = pl.pallas_call(
        _filter_mlp_kernel,
        grid=(lp // tr,),
        in_specs=[pl.BlockSpec((tr, LANES), lambda i: (i, 0)),
                  pl.BlockSpec((LANES, width), lambda i: (0, 0)),
                  pl.BlockSpec((2, width, width), lambda i: (0, 0, 0)),
                  pl.BlockSpec((3, width), lambda i: (0, 0)),
                  pl.BlockSpec((3, width), lambda i: (0, 0))],
        out_specs=pl.BlockSpec((tr, width), lambda i: (i, 0)),
        out_shape=jax.ShapeDtypeStruct((lp, width), _F32),
        compiler_params=_params(("arbitrary",)),
        name="filter_mlp",
    )(z, w1p, w_inner, b, freq)
    dc = _pick(d, 512, LANES)
    nc = d // dc
    taps, norm = pl.pallas_call(
        _filter_taps_kernel,
        grid=(nc, lp // tr),
        in_specs=[pl.BlockSpec((tr, width), lambda c, i: (i, 0)),
                  pl.BlockSpec((tr, LANES), lambda c, i: (i, 0)),
                  pl.BlockSpec((width, dc), lambda c, i: (0, c)),
                  pl.BlockSpec((width, dc), lambda c, i: (0, c + nc)),
                  pl.BlockSpec((2, dc), lambda c, i: (0, c))],
        out_specs=[pl.BlockSpec((2, tr, dc), lambda c, i: (0, i, c)), pl.BlockSpec((1, dc), lambda c, i: (0, c))],
        out_shape=[jax.ShapeDtypeStruct((2, lp, d), _F32), jax.ShapeDtypeStruct((1, d), _F32)],
        compiler_params=_params(("arbitrary", "arbitrary")),
        name="filter_taps",
    )(hm, z, w_out, w_out, delta)
    return _filter_spectrum(_dft_fwd(taps.reshape(2 * lp, d), dft, 2, 0, lp), norm)


def _to_flat(parts, trunks, rows, dtype):
    d = parts[0].shape[-1]
    out = []
    for part, t in zip(parts, trunks):
        out.append(jnp.pad(part.astype(dtype), ((0, 0), (SLOT_PAD, 0), (0, 0))).reshape(t.batch * t.slot, d))
    used = sum(t.batch * t.slot for t in trunks)
    out.append(jnp.zeros((rows - used, d), dtype))
    return jnp.concatenate(out, axis=0)


def _hyena_mixer(h, valid, gamma, trunks, rows, tm, w_in, w_conv, filt, delta, skip, w_out, dft):
    d = h.shape[1]
    x0, vx = _conv_matmul(_halo_norm(h, valid, gamma, tm), w_in, w_conv, 3, _pick(d, 512, LANES),
                          _hyena_epilogue, (_F32, _F32), "hyena_in")
    parts = []
    for t in trunks:
        kspec = _hyena_filter(t.seq, *filt, delta, dft[t.seq])
        row0 = t.base + SLOT_PAD
        planes = _spectral_product(_dft_fwd(vx, dft[t.seq], t.batch, row0, t.slot), kspec)
        y = _dft_inv(planes, dft[t.seq], vx, x0, skip.reshape(1, -1), row0, t.slot)
        parts.append(jnp.swapaxes(y, 1, 2).reshape(t.batch, -1, d)[:, :t.seq])
    return _residual_matmul(_to_flat(parts, trunks, rows, _BF16), w_out, h, tm, "hyena_out")


def _attention_bias(rpb, rows_in_grid):
    n_blocks = rows_in_grid // Q_ROWS
    kr_win = min(WIN_ROWS, rows_in_grid)
    qr = np.arange(Q_ROWS)[:, None, None]
    kb = np.arange(3)[None, :, None]
    kr = np.arange(Q_ROWS)[None, None, :]
    dr_idx, row_ok = [], []
    for g in (0, 1, n_blocks - 1):
        r = Q_ROWS * g + qr
        key_row = Q_ROWS * (g + kb - 1) + kr
        rs = np.clip(r - kr_win // 2, 0, rows_in_grid - kr_win)
        ok = (key_row >= rs) & (key_row < rs + kr_win) & (g + kb - 1 >= 0) & (g + kb - 1 < n_blocks)
        dr_idx.append(np.clip(key_row - r + (WIN_ROWS - 1), 0, 2 * WIN_ROWS - 2))
        row_ok.append(ok)
    dr_idx = np.stack(dr_idx)
    row_ok = np.stack(row_ok)
    cols = np.arange(GRID_W)
    col_start = np.clip(cols - WIN_COLS // 2, 0, GRID_W - WIN_COLS)
    col_ok = (cols[None, :] >= col_start[:, None]) & (cols[None, :] < col_start[:, None] + WIN_COLS)
    col_idx = np.clip(cols[None, :] - cols[:, None] + WIN_COLS - 1, 0, 2 * WIN_COLS - 2)
    by_col = rpb[:, :, col_idx]
    by_row = by_col[:, dr_idx]
    ok = row_ok[:, :, :, :, None, None] & col_ok[None, None, None, None]
    bias = jnp.where(ok[None], by_row, MASK_VALUE)
    bias = jnp.transpose(bias, (1, 0, 2, 5, 3, 4, 6))
    return bias.reshape(3, rpb.shape[0], Q_BLOCK, 3 * Q_BLOCK).astype(_F32)


def _na_kernel(q_ref, k0_ref, k1_ref, k2_ref, v0_ref, v1_ref, v2_ref, km_ref, vm_ref, bias_ref, o_ref, *, heads):
    nt = (((1,), (1,)), ((), ()))
    for h in range(heads):
        hs = slice(h * HEAD_DIM, (h + 1) * HEAD_DIM)
        q = q_ref[:, hs]
        s = [lax.dot_general(q, k_ref[:, hs], nt, preferred_element_type=_F32)
             + bias_ref[h, :, i * Q_BLOCK:(i + 1) * Q_BLOCK]
             for i, k_ref in enumerate((k0_ref, k1_ref, k2_ref))]
        sm = lax.dot_general(q, km_ref[:, hs], nt, preferred_element_type=_F32)
        m = jnp.max(sm, axis=-1, keepdims=True)
        for si in s:
            m = jnp.maximum(m, jnp.max(si, axis=-1, keepdims=True))
        pm = jnp.exp(sm - m)
        den = jnp.sum(pm, axis=-1, keepdims=True)
        acc = jnp.dot(pm.astype(_BF16), vm_ref[:, hs], preferred_element_type=_F32)
        for si, v_ref in zip(s, (v0_ref, v1_ref, v2_ref)):
            p = jnp.exp(si - m)
            den = den + jnp.sum(p, axis=-1, keepdims=True)
            acc = acc + jnp.dot(p.astype(_BF16), v_ref[:, hs], preferred_element_type=_F32)
        o_ref[:, hs] = (acc / den).astype(o_ref.dtype)


def _na_meta_kernel(q_ref, k_ref, v_ref, o_ref, *, heads):
    nt = (((1,), (1,)), ((), ()))
    for h in range(heads):
        hs = slice(h * HEAD_DIM, (h + 1) * HEAD_DIM)
        s = lax.dot_general(q_ref[:, hs], k_ref[:, hs], nt, preferred_element_type=_F32)
        p = jnp.exp(s - jnp.max(s, axis=-1, keepdims=True))
        acc = jnp.dot(p.astype(_BF16), v_ref[:, hs], preferred_element_type=_F32)
        o_ref[:, hs] = (acc / jnp.sum(p, axis=-1, keepdims=True)).astype(o_ref.dtype)


def _neighbourhood_attention(qkv, trunk, bias):
    d = qkv.shape[1] // 3
    n_heads = d // HEAD_DIM
    hb = min(4, n_heads)
    wd = hb * HEAD_DIM
    nb = trunk.n_tok // Q_BLOCK
    b = trunk.batch
    tok0 = trunk.base + SLOT_PAD + N_META
    meta0 = (trunk.base + SLOT_PAD) // N_META
    slot_meta = trunk.slot // N_META

    def win(col0, shift):
        return _window(Q_BLOCK, wd, lambda h, i, g: (
            tok0 + i * trunk.slot + jnp.clip(g + shift, 0, nb - 1) * Q_BLOCK, col0 + h * wd))

    def meta(col0):
        return _window(N_META, wd, lambda h, i, g: (trunk.base + SLOT_PAD + i * trunk.slot, col0 + h * wd))

    variant = lambda g: jnp.where(g == 0, 0, jnp.where(g == nb - 1, 2, 1))
    out_grid = pl.pallas_call(
        functools.partial(_na_kernel, heads=hb),
        grid=(n_heads // hb, b, nb),
        in_specs=[win(0, 0), win(d, -1), win(d, 0), win(d, 1), win(2 * d, -1), win(2 * d, 0), win(2 * d, 1),
                  meta(d), meta(2 * d),
                  pl.BlockSpec((None, hb, Q_BLOCK, 3 * Q_BLOCK), lambda h, i, g: (variant(g), h, 0, 0))],
        out_specs=pl.BlockSpec((Q_BLOCK, wd), lambda h, i, g: (i * nb + g, h)),
        out_shape=jax.ShapeDtypeStruct((b * trunk.n_tok, d), _BF16),
        compiler_params=_params(("arbitrary", "arbitrary", "arbitrary")),
        name="na_grid",
    )(qkv, qkv, qkv, qkv, qkv, qkv, qkv, qkv, qkv, bias)
    mspec = lambda k: pl.BlockSpec((N_META, d), lambda i: (meta0 + i * slot_meta, k))
    out_meta = pl.pallas_call(
        functools.partial(_na_meta_kernel, heads=n_heads),
        grid=(b,),
        in_specs=[mspec(0), mspec(1), mspec(2)],
        out_specs=pl.BlockSpec((N_META, d), lambda i: (i, 0)),
        out_shape=jax.ShapeDtypeStruct((b * N_META, d), _BF16),
        compiler_params=_params(("arbitrary",)),
        name="na_meta",
    )(qkv, qkv, qkv)
    return jnp.concatenate([out_meta.reshape(b, N_META, d), out_grid.reshape(b, trunk.n_tok, d)], axis=1)


def _na_mixer(h, valid, gamma, trunks, rows, tm, w_qkv, rpb, w_out):
    qkv = _qkv(_plain_norm(h, valid, gamma), w_qkv, tm)
    biases = {}
    parts = []
    for t in trunks:
        grid_rows = t.n_tok // GRID_W
        key = grid_rows if grid_rows < 3 * Q_ROWS else -1
        if key not in biases:
            biases[key] = _attention_bias(rpb, grid_rows)
        parts.append(_neighbourhood_attention(qkv, t, biases[key]))
    return _residual_matmul(_to_flat(parts, trunks, rows, _BF16), w_out, h, tm, "na_out")


def _valid_rows(trunks, rows):
    r = jnp.arange(rows, dtype=jnp.int32)[:, None]
    v = jnp.zeros((rows, 1), jnp.bool_)
    for t in trunks:
        inside = (r >= t.base) & (r < t.base + t.batch * t.slot)
        v = v | (inside & ((r - t.base) % t.slot >= SLOT_PAD))
    return jnp.broadcast_to(v.astype(_F32), (rows, LANES))


def _padded_ffn_weights(w_up, w_conv, w_down, tn):
    f = w_down.shape[0]
    pad = _round_up(f, tn) - f
    halves = lambda a: jnp.concatenate([jnp.pad(a[:, :f], ((0, 0), (0, pad))), jnp.pad(a[:, f:], ((0, 0), (0, pad)))], axis=1)
    return halves(w_up).astype(_BF16), halves(w_conv), jnp.pad(w_down, ((0, pad), (0, 0))).astype(_BF16)


def kernel(x_prompt, x_sample, meta_tokens, norm_mix, norm_ffn, norm_final, hy_w_in, hy_w_conv, hy_f_w1,
           hy_f_w_inner, hy_f_b, hy_f_freq, hy_f_w_out, hy_delta, hy_skip, hy_w_out, na_w_qkv, na_rpb,
           na_w_out, ffn_w_up, ffn_w_conv, ffn_w_down):
    d = x_prompt.shape[-1]
    depth = norm_mix.shape[0]
    xs = (x_prompt, x_sample)
    tm = _TM
    trunks, rows = _plan([(x.shape[0], x.shape[1]) for x in xs], tm)
    valid = _valid_rows(trunks, rows)
    dft = {t.seq: _dft_matrices(t.seq) for t in trunks}
    ffn_tn = _pick(_round_up(ffn_w_down.shape[1], _FFN_TN), _FFN_TN, LANES)

    meta = meta_tokens.astype(_F32)
    seqs = [jnp.concatenate([jnp.broadcast_to(meta[None], (x.shape[0], N_META, d)), x], axis=1) for x in xs]
    h = _to_flat(seqs, trunks, rows, _F32)

    bf = lambda w: w.astype(_BF16)
    for i in range(depth):
        j = i // 2
        if i % 2 == 0:
            filt = (hy_f_w1[j], hy_f_w_inner[j], hy_f_b[j], hy_f_freq[j], hy_f_w_out[j])
            h = _hyena_mixer(h, valid, norm_mix[i], trunks, rows, tm, bf(hy_w_in[j]), hy_w_conv[j], filt,
                             hy_delta[j], hy_skip[j], bf(hy_w_out[j]), dft)
        else:
            h = _na_mixer(h, valid, norm_mix[i], trunks, rows, tm, bf(na_w_qkv[j]), na_rpb[j], bf(na_w_out[j]))
        w_up, w_conv, w_down = _padded_ffn_weights(ffn_w_up[i], ffn_w_conv[i], ffn_w_down[i], ffn_tn)
        a = _conv_matmul(_halo_norm(h, valid, norm_ffn[i], tm), w_up, w_conv, 2, ffn_tn, _ffn_epilogue,
                         (_BF16,), "ffn_up")[0]
        h = _residual_matmul(a, w_down, h, tm, "ffn_down")
    return tuple(_final_norm(h, norm_final, t) for t in trunks)
```

```python
import functools
import math
from typing import NamedTuple

import numpy as np
import jax
import jax.numpy as jnp
from jax import lax
from jax.experimental import pallas as pl
from jax.experimental.pallas import tpu as pltpu

N_META = 16
GRID_W = 64
HEAD_DIM = 128
WIN_ROWS = 8
WIN_COLS = 16
HY_EMB = 33
HY_BANDS = (HY_EMB - 1) // 2
RMS_EPS = 1e-6

SLOT_PAD = 16
HALO = 16
EPILOGUE_ROWS = 32
MXU_COLS = 256
RESIDENT_WEIGHT_K = 4096
FFT_RADIX = 16
Q_ROWS = 4
Q_BLOCK = Q_ROWS * GRID_W
MASK_VALUE = -1e30
LANES = 128
V7X_VMEM_LIMIT = 56 * 1024 * 1024
DFT_BLOCK_BYTES = 10 * 1024 * 1024
TWIDDLE_LANES = 8

_TM = 640
_FFN_TN = 512
_F32 = jnp.float32
_BF16 = jnp.bfloat16


class _Trunk(NamedTuple):
    batch: int
    n_tok: int
    seq: int
    slot: int
    base: int


def _round_up(x, m):
    return (x + m - 1) // m * m


def _pick(n, target, mult):
    best = None
    for d in range(mult, min(n, target) + 1, mult):
        if n % d == 0:
            best = d
    return best if best is not None else n


def _fft_sizes(seq):
    n1 = _round_up(-(-(2 * seq - 1) // FFT_RADIX), 8)
    n1z = _round_up(-(-seq // FFT_RADIX), 8)
    return n1, n1z


def _plan(shapes, tm):
    trunks, base = [], 0
    for batch, n_tok in shapes:
        seq = N_META + n_tok
        slot = SLOT_PAD + seq
        trunks.append(_Trunk(batch, n_tok, seq, slot, base))
        base += batch * slot
    reach = max(t.base + (t.batch - 1) * t.slot + SLOT_PAD + FFT_RADIX * _fft_sizes(t.seq)[1] for t in trunks)
    rows = _round_up(max(base + SLOT_PAD, reach), tm)
    return trunks, rows


def _params(sem, vmem=V7X_VMEM_LIMIT):
    return pltpu.CompilerParams(dimension_semantics=sem, vmem_limit_bytes=vmem)


def _window(rows, cols, index):
    def aligned(*args):
        r, c = index(*args)
        hint = lambda x, m: x if isinstance(x, int) else pl.multiple_of(x, m)
        return hint(r, SLOT_PAD), hint(c, LANES)
    return pl.BlockSpec((pl.Element(rows), pl.Element(cols)), aligned)


def _norm_rows(h, g, valid):
    y = h * lax.rsqrt(jnp.mean(h * h, axis=-1, keepdims=True) + RMS_EPS)
    return jnp.where(valid[:, 0:1] > 0, y * g, 0.0)


def _final_norm_kernel(h_ref, g_ref, o_ref):
    x = h_ref[...]
    o_ref[...] = x * lax.rsqrt(jnp.mean(x * x, axis=-1, keepdims=True) + RMS_EPS) * g_ref[...]


def _final_norm(h, g, trunk):
    d = h.shape[1]
    tr = _pick(trunk.n_tok, 256, 16)
    nr = trunk.n_tok // tr
    tok0 = trunk.base + SLOT_PAD + N_META
    out = pl.pallas_call(
        _final_norm_kernel,
        grid=(trunk.batch, nr),
        in_specs=[_window(tr, d, lambda i, r: (tok0 + i * trunk.slot + r * tr, 0)),
                  pl.BlockSpec((1, d), lambda i, r: (0, 0))],
        out_specs=pl.BlockSpec((tr, d), lambda i, r: (i * nr + r, 0)),
        out_shape=jax.ShapeDtypeStruct((trunk.batch * trunk.n_tok, d), _F32),
        compiler_params=_params(("arbitrary", "arbitrary")),
        name="final_norm",
    )(h, g.reshape(1, d))
    return out.reshape(trunk.batch, trunk.n_tok, d)


def _conv3(p_ref, k, c, r0, nr):
    at = lambda shift: p_ref[k, HALO + shift + r0:HALO + shift + r0 + nr, :]
    return c[0:1] * at(-1) + c[1:2] * at(0) + c[2:3] * at(1)


def _ffn_epilogue(vals, outs, rows):
    g, v = vals
    outs[0][rows, :] = (g * (1.0 / (1.0 + jnp.exp(-g))) * v).astype(outs[0].dtype)


def _hyena_epilogue(vals, outs, rows):
    x0, x1, v = vals
    outs[0][rows, :] = x0
    outs[1][rows, :] = v * x1


def _halo_norm_kernel(hp_ref, hc_ref, hn_ref, vp_ref, vc_ref, vn_ref, g_ref, o_ref):
    g = g_ref[...]
    packed = lambda h_ref, v_ref: pltpu.bitcast(_norm_rows(h_ref[...], g, v_ref[...]).astype(_BF16), jnp.uint32)
    tm2 = hc_ref.shape[0] // 2
    o_ref[0:HALO // 2] = packed(hp_ref, vp_ref)
    o_ref[HALO // 2:HALO // 2 + tm2] = packed(hc_ref, vc_ref)
    o_ref[HALO // 2 + tm2:HALO + tm2] = packed(hn_ref, vn_ref)


def _halo_norm(h, valid, gamma, tm):
    rows, d = h.shape
    per, last = tm // HALO, rows // HALO - 1

    def panel(width):
        return [pl.BlockSpec((HALO, width), lambda i: (jnp.maximum(i * per - 1, 0), 0)),
                pl.BlockSpec((tm, width), lambda i: (i, 0)),
                pl.BlockSpec((HALO, width), lambda i: (jnp.minimum((i + 1) * per, last), 0))]

    return pl.pallas_call(
        _halo_norm_kernel,
        grid=(rows // tm,),
        in_specs=panel(d) + panel(LANES) + [pl.BlockSpec((1, d), lambda i: (0, 0))],
        out_specs=pl.BlockSpec((None, tm // 2 + HALO, d), lambda i: (i, 0, 0)),
        out_shape=jax.ShapeDtypeStruct((rows // tm, tm // 2 + HALO, d), jnp.uint32),
        compiler_params=_params(("arbitrary",)),
        name="halo_norm",
    )(h, h, h, valid, valid, valid, gamma.reshape(1, d))


def _plain_norm_kernel(h_ref, v_ref, g_ref, o_ref):
    o_ref[...] = _norm_rows(h_ref[...], g_ref[...], v_ref[...]).astype(o_ref.dtype)


def _plain_norm(h, valid, gamma):
    rows, d = h.shape
    tr = _pick(rows, 256, 16)
    return pl.pallas_call(
        _plain_norm_kernel,
        grid=(rows // tr,),
        in_specs=[pl.BlockSpec((tr, d), lambda i: (i, 0)),
                  pl.BlockSpec((tr, LANES), lambda i: (i, 0)),
                  pl.BlockSpec((1, d), lambda i: (0, 0))],
        out_specs=pl.BlockSpec((tr, d), lambda i: (i, 0)),
        out_shape=jax.ShapeDtypeStruct((rows, d), _BF16),
        compiler_params=_params(("arbitrary",)),
        name="rmsnorm",
    )(h, valid, gamma.reshape(1, d))


def _conv_matmul_kernel(*refs, tm, nw, n_onto, epilogue):
    x_ref = refs[0]
    w_refs = refs[1:1 + nw]
    c_refs = refs[1 + nw:1 + 2 * nw]
    out_refs = refs[1 + 2 * nw + n_onto:-2]
    pa_ref, pb_ref = refs[-2:]
    s = pl.program_id(0)

    @pl.when(s == 0)
    def _():
        pb_ref[...] = jnp.zeros_like(pb_ref)

    def step(prev_ref, cur_ref):
        tn = cur_ref.shape[2]
        pieces = [(k, c0) for k in range(nw) for c0 in range(0, tn, MXU_COLS)]
        chunks = list(range(0, tm, EPILOGUE_ROWS))
        per_piece = -(-len(chunks) // len(pieces))
        cs = [c_refs[k][...] for k in range(nw)]
        x = pltpu.bitcast(x_ref[...], _BF16)
        for n, (k, c0) in enumerate(pieces):
            cols = slice(c0, min(c0 + MXU_COLS, tn))
            cur_ref[k, :, cols] = jnp.dot(x, w_refs[k][:, cols], preferred_element_type=_F32)
            for r0 in chunks[n * per_piece:(n + 1) * per_piece]:
                epilogue([_conv3(prev_ref, j, cs[j], r0, EPILOGUE_ROWS) for j in range(nw)], out_refs,
                         slice(r0, r0 + EPILOGUE_ROWS))

    @pl.when(s % 2 == 0)
    def _():
        step(pb_ref, pa_ref)

    @pl.when(s % 2 == 1)
    def _():
        step(pa_ref, pb_ref)


def _conv_matmul(xp, ws, cs, group_tiles, tn, nj, tile0, out_cols, epilogue, out_dtypes, name, onto=()):
    ni, packed_rows, d = xp.shape
    tmh = 2 * packed_rows
    tm = tmh - 2 * HALO
    nw = len(ws)
    steps = ni * nj
    cur = lambda s: jnp.minimum(s, steps - 1)
    lag = lambda s: jnp.maximum(s - 1, 0)
    wspec = lambda k: pl.BlockSpec((d, tn), lambda s: (0, cur(s) // ni + tile0 + group_tiles[k]))
    cspec = lambda k: pl.BlockSpec((3, tn), lambda s: (0, lag(s) // ni + tile0 + group_tiles[k]))
    out = pl.BlockSpec((tm, tn), lambda s: (lag(s) % ni, lag(s) // ni + tile0))
    return pl.pallas_call(
        functools.partial(_conv_matmul_kernel, tm=tm, nw=nw, n_onto=len(onto), epilogue=epilogue),
        grid=(steps + 1,),
        in_specs=[pl.BlockSpec((None, packed_rows, d), lambda s: (cur(s) % ni, 0, 0))]
        + [wspec(k) for k in range(nw)] + [cspec(k) for k in range(nw)]
        + [pl.BlockSpec(memory_space=pl.ANY)] * len(onto),
        out_specs=[out] * len(out_dtypes),
        out_shape=[jax.ShapeDtypeStruct((ni * tm, out_cols), dt) for dt in out_dtypes],
        input_output_aliases={1 + 2 * nw + k: k for k in range(len(onto))},
        scratch_shapes=[pltpu.VMEM((nw, tmh, tn), _F32), pltpu.VMEM((nw, tmh, tn), _F32)],
        compiler_params=_params(("arbitrary",)),
        name=name,
    )(xp, *ws, *cs, *onto)


def _qkv_kernel(x_ref, w_ref, o_ref, *, n_q_tiles, scale):
    acc = jnp.dot(x_ref[...], w_ref[...], preferred_element_type=_F32)
    s = jnp.where(pl.program_id(0) < n_q_tiles, scale, 1.0).astype(_F32)
    o_ref[...] = (acc * s).astype(o_ref.dtype)


def _qkv(hn, w_qkv, tm):
    rows, d = hn.shape
    n = w_qkv.shape[1]
    tn = _pick(d, 1024, LANES)
    return pl.pallas_call(
        functools.partial(_qkv_kernel, n_q_tiles=d // tn, scale=HEAD_DIM ** -0.5),
        grid=(n // tn, rows // tm),
        in_specs=[pl.BlockSpec((tm, d), lambda j, i: (i, 0)),
                  pl.BlockSpec((d, tn), lambda j, i: (0, j))],
        out_specs=pl.BlockSpec((tm, tn), lambda j, i: (i, j)),
        out_shape=jax.ShapeDtypeStruct((rows, n), _BF16),
        compiler_params=_params(("arbitrary", "arbitrary")),
        name="qkv_proj",
    )(hn, w_qkv)


def _residual_kernel(x_ref, w_ref, h_ref, o_ref):
    o_ref[...] = h_ref[...] + jnp.dot(x_ref[...], w_ref[...], preferred_element_type=_F32)


def _residual_matmul(x, w, h, tm, name):
    rows, k = x.shape
    d = w.shape[1]
    weights_resident = k <= RESIDENT_WEIGHT_K
    tn = _pick(d, 1024 if weights_resident else 256, LANES)
    grid = (d // tn, rows // tm) if weights_resident else (rows // tm, d // tn)
    ij = (lambda a, b: (b, a)) if weights_resident else (lambda a, b: (a, b))
    return pl.pallas_call(
        _residual_kernel,
        grid=grid,
        in_specs=[pl.BlockSpec((tm, k), lambda a, b: (ij(a, b)[0], 0)),
                  pl.BlockSpec((k, tn), lambda a, b: (0, ij(a, b)[1])),
                  pl.BlockSpec((tm, tn), lambda a, b: ij(a, b))],
        out_specs=pl.BlockSpec((tm, tn), lambda a, b: ij(a, b)),
        out_shape=jax.ShapeDtypeStruct((rows, d), _F32),
        input_output_aliases={2: 0},
        compiler_params=_params(("arbitrary", "arbitrary")),
        name=name,
    )(x, w, h)


class _Dft(NamedTuple):
    fwd: jax.Array
    inv: jax.Array
    twiddle: jax.Array


def _dft_matrices(seq):
    assert seq % FFT_RADIX == 0
    n1, n1z = _fft_sizes(seq)
    n = FFT_RADIX * n1
    k1 = jnp.arange(n1, dtype=jnp.int32)[:, None]
    m1 = jnp.arange(n1z, dtype=jnp.int32)[None, :]
    phase = ((m1 * k1) % n1).astype(_F32) * (2.0 * math.pi / n1)
    f = jnp.concatenate([jnp.cos(phase), -jnp.sin(phase)], axis=0)
    f = jnp.where(m1 * FFT_RADIX < seq, f, 0.0)
    p = jnp.arange(FFT_RADIX, dtype=jnp.int32)[:, None]
    tphase = (p * k1[:, 0][None, :]).astype(_F32) * (2.0 * math.pi / n)
    tw = jnp.stack([jnp.cos(tphase), -jnp.sin(tphase)], axis=1)
    tw = jnp.broadcast_to(tw[..., None], tw.shape + (TWIDDLE_LANES,))
    return _Dft(f.astype(_BF16), (f.T * (1.0 / n)).astype(_BF16), tw)


def _plane_rows(refs, plane, n1z):
    rows = pl.ds(plane, n1z, stride=FFT_RADIX)
    return jnp.concatenate([ref[rows, :] for ref in refs], axis=1)


def _planes_per_step(n1, dc):
    plane_block = 2 * FFT_RADIX * n1 * dc * 4
    return FFT_RADIX if plane_block <= DFT_BLOCK_BYTES else 1


def _lane_windows(n_rows, dc, row0, row_stride):
    sub = lambda l: _window(n_rows, LANES, lambda i, c, p: (row0 + i * row_stride, c * dc + l * LANES))
    return [sub(l) for l in range(dc // LANES)]


def _dft_fwd_kernel(*refs, pp):
    x_refs, f_ref, t_ref, o_ref = refs[:-3], refs[-3], refs[-2], refs[-1]
    n1z, n1 = f_ref.shape[1], o_ref.shape[2]
    for q in range(pp):
        x = _plane_rows(x_refs, pl.program_id(2) * pp + q, n1z).astype(_BF16)
        a = jnp.dot(f_ref[...], x, preferred_element_type=_F32)
        ar, ai = a[:n1], a[n1:]
        tr, ti = t_ref[q, 0, :, 0:1], t_ref[q, 1, :, 0:1]
        o_ref[0, q] = ar * tr - ai * ti
        o_ref[1, q] = ar * ti + ai * tr


def _dft_fwd(x, dft, n_sig, row0, row_stride):
    d = x.shape[1]
    r = FFT_RADIX
    n1, n1z = dft.fwd.shape[0] // 2, dft.fwd.shape[1]
    dc = _pick(d, 256, LANES)
    pp = _planes_per_step(n1, dc)
    wins = _lane_windows(n1z * r, dc, row0, row_stride)
    return pl.pallas_call(
        functools.partial(_dft_fwd_kernel, pp=pp),
        grid=(n_sig, d // dc, r // pp),
        in_specs=wins + [pl.BlockSpec((2 * n1, n1z), lambda i, c, p: (0, 0)),
                         pl.BlockSpec((pp, 2, n1, TWIDDLE_LANES), lambda i, c, p: (p, 0, 0, 0))],
        out_specs=pl.BlockSpec((2, pp, None, n1, dc), lambda i, c, p: (0, p, i, 0, c)),
        out_shape=jax.ShapeDtypeStruct((2, r, n_sig, n1, d), _F32),
        compiler_params=_params(("arbitrary", "arbitrary", "arbitrary")),
        name="dft_fwd",
    )(*([x] * len(wins)), dft.fwd, dft.twiddle)


def _cmul_const(x, w):
    re, im = x
    wr, wi = float(w.real), float(w.imag)
    if abs(wi) < 1e-12:
        return (re, im) if wr > 0 else (-re, -im)
    if abs(wr) < 1e-12:
        return (-im, re) if wi > 0 else (im, -re)
    return (re * wr - im * wi, re * wi + im * wr)


def _fft_planes(xs, sign):
    n = len(xs)
    if n == 1:
        return xs
    ev = _fft_planes(xs[0::2], sign)
    od = _fft_planes(xs[1::2], sign)
    out = [None] * n
    for k in range(n // 2):
        w = complex(math.cos(2.0 * math.pi * k / n), sign * math.sin(2.0 * math.pi * k / n))
        tr, ti = _cmul_const(od[k], w)
        out[k] = (ev[k][0] + tr, ev[k][1] + ti)
        out[k + n // 2] = (ev[k][0] - tr, ev[k][1] - ti)
    return out


def _tile_loop(tk, dc, body):
    def step(r, carry):
        rows = pl.ds(pl.multiple_of(r * 8, 8), 8)
        for l in range(dc // LANES):
            body(rows, slice(l * LANES, (l + 1) * LANES))
        return carry
    lax.fori_loop(0, tk // 8, step, 0)


def _filter_spectrum_kernel(a_ref, s_ref, o_ref):
    r, tk, dc = o_ref.shape[1], o_ref.shape[2], o_ref.shape[3]

    def body(rows, lanes):
        inv = 1.0 / s_ref[:, lanes]
        xf = _fft_planes([(a_ref[0, n, 0, rows, lanes], a_ref[1, n, 0, rows, lanes]) for n in range(r)], -1)
        xb = _fft_planes([(a_ref[0, n, 1, rows, lanes], a_ref[1, n, 1, rows, lanes]) for n in range(r)], -1)
        for k in range(r):
            o_ref[0, k, rows, lanes] = (xf[k][0] + xb[k][0]) * inv
            o_ref[1, k, rows, lanes] = (xf[k][1] - xb[k][1]) * inv

    _tile_loop(tk, dc, body)


def _plane_tile(n1):
    return _pick(n1, 352, 8)


def _filter_spectrum(planes, norm):
    _, r, _, n1, d = planes.shape
    tk, dc = _plane_tile(n1), LANES
    return pl.pallas_call(
        _filter_spectrum_kernel,
        grid=(d // dc, n1 // tk),
        in_specs=[pl.BlockSpec((2, r, 2, tk, dc), lambda c, k: (0, 0, 0, k, c)),
                  pl.BlockSpec((1, dc), lambda c, k: (0, c))],
        out_specs=pl.BlockSpec((2, r, tk, dc), lambda c, k: (0, 0, k, c)),
        out_shape=jax.ShapeDtypeStruct((2, r, n1, d), _F32),
        compiler_params=_params(("arbitrary", "arbitrary")),
        name="filter_spectrum",
    )(planes, norm)


def _spectral_product_kernel(a_ref, k_ref, o_ref):
    r, tk, dc = o_ref.shape[1], o_ref.shape[2], o_ref.shape[3]

    def body(rows, lanes):
        x = _fft_planes([(a_ref[0, n, rows, lanes], a_ref[1, n, rows, lanes]) for n in range(r)], -1)
        y = []
        for k in range(r):
            kr, ki = k_ref[0, k, rows, lanes], k_ref[1, k, rows, lanes]
            y.append((x[k][0] * kr - x[k][1] * ki, x[k][0] * ki + x[k][1] * kr))
        z = _fft_planes(y, +1)
        for n in range(r):
            o_ref[0, n, rows, lanes] = z[n][0]
            o_ref[1, n, rows, lanes] = z[n][1]

    _tile_loop(tk, dc, body)


def _spectral_product(planes, kspec):
    _, r, b, n1, d = planes.shape
    tk, dc = _plane_tile(n1), LANES
    spec = pl.BlockSpec((2, r, None, tk, dc), lambda c, k, i: (0, 0, i, k, c))
    return pl.pallas_call(
        _spectral_product_kernel,
        grid=(d // dc, n1 // tk, b),
        in_specs=[spec, pl.BlockSpec((2, r, tk, dc), lambda c, k, i: (0, 0, k, c))],
        out_specs=spec,
        out_shape=jax.ShapeDtypeStruct(planes.shape, _F32),
        compiler_params=_params(("arbitrary", "arbitrary", "arbitrary")),
        name="spectral_product",
    )(planes, kspec)


def _dft_inv_kernel(*refs, pp, nsub):
    b_ref, h_ref, t_ref, skip_ref, o_ref = refs[0], refs[1], refs[2], refs[-2], refs[-1]
    vx_refs, x0_refs = refs[3:3 + nsub], refs[3 + nsub:3 + 2 * nsub]
    n1z = h_ref.shape[0]
    for q in range(pp):
        plane = pl.program_id(2) * pp + q
        br, bi = b_ref[0, q], b_ref[1, q]
        tr, ti = t_ref[q, 0, :, 0:1], t_ref[q, 1, :, 0:1]
        bb = jnp.concatenate([br * tr + bi * ti, bi * tr - br * ti], axis=0).astype(_BF16)
        y = jnp.dot(h_ref[...], bb, preferred_element_type=_F32)
        o_ref[q] = (y + _plane_rows(vx_refs, plane, n1z) * skip_ref[...]) * _plane_rows(x0_refs, plane, n1z)


def _dft_inv(planes, dft, vx, x0, skip, row0, row_stride):
    _, r, b, n1, d = planes.shape
    n1z = dft.inv.shape[0]
    dc = _pick(d, 256, LANES)
    pp = _planes_per_step(n1, dc)
    wins = _lane_windows(n1z * r, dc, row0, row_stride)
    return pl.pallas_call(
        functools.partial(_dft_inv_kernel, pp=pp, nsub=len(wins)),
        grid=(b, d // dc, r // pp),
        in_specs=[pl.BlockSpec((2, pp, None, n1, dc), lambda i, c, p: (0, p, i, 0, c)),
                  pl.BlockSpec((n1z, 2 * n1), lambda i, c, p: (0, 0)),
                  pl.BlockSpec((pp, 2, n1, TWIDDLE_LANES), lambda i, c, p: (p, 0, 0, 0))]
        + wins + wins + [pl.BlockSpec((1, dc), lambda i, c, p: (0, c))],
        out_specs=pl.BlockSpec((None, pp, n1z, dc), lambda i, c, p: (i, p, 0, c)),
        out_shape=jax.ShapeDtypeStruct((b, r, n1z, d), _F32),
        compiler_params=_params(("arbitrary", "arbitrary", "arbitrary")),
        name="dft_inv",
    )(planes, dft.inv, dft.twiddle, *([vx] * len(wins)), *([x0] * len(wins)), skip)


def _fused_conv_kernel(*refs, nsub):
    vx_refs, x0_refs = refs[:nsub], refs[nsub:2 * nsub]
    f_ref, h_ref, t_ref, k_ref, skip_ref, o_ref, a_ref = refs[2 * nsub:]
    r, n1, dc = a_ref.shape[1], a_ref.shape[2], a_ref.shape[3]
    n1z = f_ref.shape[1]
    for p in range(r):
        x = _plane_rows(vx_refs, p, n1z).astype(_BF16)
        a = jnp.dot(f_ref[...], x, preferred_element_type=_F32)
        ar, ai = a[:n1], a[n1:]
        tr, ti = t_ref[p, 0, :, 0:1], t_ref[p, 1, :, 0:1]
        a_ref[0, p] = ar * tr - ai * ti
        a_ref[1, p] = ar * ti + ai * tr

    def body(rows, lanes):
        x = _fft_planes([(a_ref[0, n, rows, lanes], a_ref[1, n, rows, lanes]) for n in range(r)], -1)
        y = []
        for k in range(r):
            kr, ki = k_ref[0, k, rows, lanes], k_ref[1, k, rows, lanes]
            y.append((x[k][0] * kr - x[k][1] * ki, x[k][0] * ki + x[k][1] * kr))
        z = _fft_planes(y, +1)
        for n in range(r):
            a_ref[0, n, rows, lanes] = z[n][0]
            a_ref[1, n, rows, lanes] = z[n][1]

    _tile_loop(n1, dc, body)
    for p in range(r):
        br, bi = a_ref[0, p], a_ref[1, p]
        tr, ti = t_ref[p, 0, :, 0:1], t_ref[p, 1, :, 0:1]
        bb = jnp.concatenate([br * tr + bi * ti, bi * tr - br * ti], axis=0).astype(_BF16)
        y = jnp.dot(h_ref[...], bb, preferred_element_type=_F32)
        o_ref[p] = (y + _plane_rows(vx_refs, p, n1z) * skip_ref[...]) * _plane_rows(x0_refs, p, n1z)


def _fused_conv(vx, x0, kspec, dft, skip, batch, row0, row_stride):
    d = vx.shape[1]
    r = FFT_RADIX
    n1, n1z = dft.fwd.shape[0] // 2, dft.fwd.shape[1]
    dc = _pick(d, 256, LANES)
    wins = _lane_windows(n1z * r, dc, row0, row_stride)
    sig = lambda spec: pl.BlockSpec(spec.block_shape, lambda c, i: spec.index_map(i, c, 0))
    wins = [sig(w) for w in wins]
    return pl.pallas_call(
        functools.partial(_fused_conv_kernel, nsub=len(wins)),
        grid=(d // dc, batch),
        in_specs=wins + wins + [
            pl.BlockSpec((2 * n1, n1z), lambda c, i: (0, 0)),
            pl.BlockSpec((n1z, 2 * n1), lambda c, i: (0, 0)),
            pl.BlockSpec((r, 2, n1, TWIDDLE_LANES), lambda c, i: (0, 0, 0, 0)),
            pl.BlockSpec((2, r, n1, dc), lambda c, i: (0, 0, 0, c)),
            pl.BlockSpec((1, dc), lambda c, i: (0, c))],
        out_specs=pl.BlockSpec((None, r, n1z, dc), lambda c, i: (i, 0, 0, c)),
        out_shape=jax.ShapeDtypeStruct((batch, r, n1z, d), _F32),
        scratch_shapes=[pltpu.VMEM((2, r, n1, dc), _F32)],
        compiler_params=_params(("arbitrary", "arbitrary")),
        name="hyena_conv",
    )(*([vx] * len(wins)), *([x0] * len(wins)), dft.fwd, dft.inv, dft.twiddle, kspec, skip)


def _filter_mlp_kernel(z_ref, w1_ref, wi_ref, b_ref, f_ref, o_ref):
    hi = lax.Precision.HIGHEST
    h = jnp.sin(f_ref[0:1] * (jnp.dot(z_ref[...], w1_ref[...], precision=hi, preferred_element_type=_F32) + b_ref[0:1]))
    h = jnp.sin(f_ref[1:2] * (jnp.dot(h, wi_ref[0], precision=hi, preferred_element_type=_F32) + b_ref[1:2]))
    h = jnp.sin(f_ref[2:3] * (jnp.dot(h, wi_ref[1], precision=hi, preferred_element_type=_F32) + b_ref[2:3]))
    o_ref[...] = h


def _filter_taps_kernel(hm_ref, z_ref, wf_ref, wb_ref, d_ref, k_ref, s_ref):
    hi = lax.Precision.HIGHEST
    t = z_ref[:, 0:1]
    mf = z_ref[:, HY_EMB:HY_EMB + 1]
    mb = z_ref[:, HY_EMB + 1:HY_EMB + 2]
    hm = hm_ref[...]
    kf = jnp.dot(hm, wf_ref[...], precision=hi, preferred_element_type=_F32) * jnp.exp(-t * jnp.abs(d_ref[0:1])) * mf
    kb = jnp.dot(hm, wb_ref[...], precision=hi, preferred_element_type=_F32) * jnp.exp(-t * jnp.abs(d_ref[1:2])) * mb
    k_ref[0] = kf
    k_ref[1] = kb

    @pl.when(pl.program_id(1) == 0)
    def _():
        s_ref[...] = jnp.zeros_like(s_ref)
    s_ref[...] += jnp.sum(jnp.abs(kf) + jnp.abs(kb), axis=0, keepdims=True)


def _position_features(seq, rows):
    pos = jnp.arange(seq, dtype=_F32)[:, None]
    t = pos / max(seq - 1, 1)
    f = jnp.linspace(1e-4, HY_BANDS - 1, HY_BANDS, dtype=_F32)[None, :]
    ang = f * (2.0 * math.pi / seq) * pos
    fwd = jnp.ones((seq, 1), _F32)
    bwd = (pos >= 1).astype(_F32)
    z = jnp.concatenate([t, jnp.cos(ang), -jnp.sin(ang), fwd, bwd], axis=-1)
    return jnp.pad(z, ((0, rows - seq), (0, LANES - z.shape[1])))


def _hyena_filter(seq, w1, w_inner, b, freq, w_out, delta, dft):
    _, n1z = _fft_sizes(seq)
    lp = n1z * FFT_RADIX
    width = w1.shape[1]
    d = w_out.shape[1] // 2
    z = _position_features(seq, lp)
    w1p = jnp.pad(w1, ((0, LANES - w1.shape[0]), (0, 0)))
    tr = n1z
    hm = pl.pallas_call(
        _filter_mlp_kernel,
        grid=(lp // tr,),
        in_specs=[pl.BlockSpec((tr, LANES), lambda i: (i, 0)),
                  pl.BlockSpec((LANES, width), lambda i: (0, 0)),
                  pl.BlockSpec((2, width, width), lambda i: (0, 0, 0)),
                  pl.BlockSpec((3, width), lambda i: (0, 0)),
                  pl.BlockSpec((3, width), lambda i: (0, 0))],
        out_specs=pl.BlockSpec((tr, width), lambda i: (i, 0)),
        out_shape=jax.ShapeDtypeStruct((lp, width), _F32),
        compiler_params=_params(("arbitrary",)),
        name="filter_mlp",
    )(z, w1p, w_inner, b, freq)
    dc = _pick(d, 512, LANES)
    nc = d // dc
    taps, norm = pl.pallas_call(
        _filter_taps_kernel,
        grid=(nc, lp // tr),
        in_specs=[pl.BlockSpec((tr, width), lambda c, i: (i, 0)),
                  pl.BlockSpec((tr, LANES), lambda c, i: (i, 0)),
                  pl.BlockSpec((width, dc), lambda c, i: (0, c)),
                  pl.BlockSpec((width, dc), lambda c, i: (0, c + nc)),
                  pl.BlockSpec((2, dc), lambda c, i: (0, c))],
        out_specs=[pl.BlockSpec((2, tr, dc), lambda c, i: (0, i, c)), pl.BlockSpec((1, dc), lambda c, i: (0, c))],
        out_shape=[jax.ShapeDtypeStruct((2, lp, d), _F32), jax.ShapeDtypeStruct((1, d), _F32)],
        compiler_params=_params(("arbitrary", "arbitrary")),
        name="filter_taps",
    )(hm, z, w_out, w_out, delta)
    return _filter_spectrum(_dft_fwd(taps.reshape(2 * lp, d), dft, 2, 0, lp), norm)


def _to_flat(parts, trunks, rows, dtype):
    d = parts[0].shape[-1]
    out = []
    for part, t in zip(parts, trunks):
        out.append(jnp.pad(part.astype(dtype), ((0, 0), (SLOT_PAD, 0), (0, 0))).reshape(t.batch * t.slot, d))
    used = sum(t.batch * t.slot for t in trunks)
    out.append(jnp.zeros((rows - used, d), dtype))
    return jnp.concatenate(out, axis=0)


def _hyena_mixer(h, valid, gamma, trunks, rows, tm, w_in, w_conv, filt, delta, skip, w_out, dft):
    d = h.shape[1]
    tn = _pick(d, 512, LANES)
    nj = d // tn
    x0, vx = _conv_matmul(_halo_norm(h, valid, gamma, tm), [w_in] * 3, [w_conv] * 3, [0, nj, 2 * nj], tn, nj, 0, d,
                          _hyena_epilogue, (_F32, _F32), "hyena_in")
    parts = []
    for t in trunks:
        kspec = _hyena_filter(t.seq, *filt, delta, dft[t.seq])
        row0 = t.base + SLOT_PAD
        n1 = dft[t.seq].fwd.shape[0] // 2
        if _planes_per_step(n1, _pick(d, 256, LANES)) == FFT_RADIX:
            y = _fused_conv(vx, x0, kspec, dft[t.seq], skip.reshape(1, -1), t.batch, row0, t.slot)
        else:
            planes = _spectral_product(_dft_fwd(vx, dft[t.seq], t.batch, row0, t.slot), kspec)
            y = _dft_inv(planes, dft[t.seq], vx, x0, skip.reshape(1, -1), row0, t.slot)
        parts.append(jnp.swapaxes(y, 1, 2).reshape(t.batch, -1, d)[:, :t.seq])
    return _residual_matmul(_to_flat(parts, trunks, rows, _BF16), w_out, h, tm, "hyena_out")


def _attention_bias(rpb, rows_in_grid):
    n_blocks = rows_in_grid // Q_ROWS
    kr_win = min(WIN_ROWS, rows_in_grid)
    qr = np.arange(Q_ROWS)[:, None, None]
    kb = np.arange(3)[None, :, None]
    kr = np.arange(Q_ROWS)[None, None, :]
    dr_idx, row_ok = [], []
    for g in (0, 1, n_blocks - 1):
        r = Q_ROWS * g + qr
        key_row = Q_ROWS * (g + kb - 1) + kr
        rs = np.clip(r - kr_win // 2, 0, rows_in_grid - kr_win)
        ok = (key_row >= rs) & (key_row < rs + kr_win) & (g + kb - 1 >= 0) & (g + kb - 1 < n_blocks)
        dr_idx.append(np.clip(key_row - r + (WIN_ROWS - 1), 0, 2 * WIN_ROWS - 2))
        row_ok.append(ok)
    dr_idx = np.stack(dr_idx)
    row_ok = np.stack(row_ok)
    cols = np.arange(GRID_W)
    col_start = np.clip(cols - WIN_COLS // 2, 0, GRID_W - WIN_COLS)
    col_ok = (cols[None, :] >= col_start[:, None]) & (cols[None, :] < col_start[:, None] + WIN_COLS)
    col_idx = np.clip(cols[None, :] - cols[:, None] + WIN_COLS - 1, 0, 2 * WIN_COLS - 2)
    by_col = rpb[:, :, col_idx]
    by_row = by_col[:, dr_idx]
    ok = row_ok[:, :, :, :, None, None] & col_ok[None, None, None, None]
    bias = jnp.where(ok[None], by_row, MASK_VALUE)
    bias = jnp.transpose(bias, (1, 0, 2, 5, 3, 4, 6))
    return bias.reshape(3, rpb.shape[0], Q_BLOCK, 3 * Q_BLOCK).astype(_F32)


def _na_kernel(q_ref, k0_ref, k1_ref, k2_ref, v0_ref, v1_ref, v2_ref, km_ref, vm_ref, bias_ref, o_ref, *, heads):
    nt = (((1,), (1,)), ((), ()))
    for h in range(heads):
        hs = slice(h * HEAD_DIM, (h + 1) * HEAD_DIM)
        q = q_ref[:, hs]
        s = [lax.dot_general(q, k_ref[:, hs], nt, preferred_element_type=_F32)
             + bias_ref[h, :, i * Q_BLOCK:(i + 1) * Q_BLOCK]
             for i, k_ref in enumerate((k0_ref, k1_ref, k2_ref))]
        sm = lax.dot_general(q, km_ref[:, hs], nt, preferred_element_type=_F32)
        m = jnp.max(sm, axis=-1, keepdims=True)
        for si in s:
            m = jnp.maximum(m, jnp.max(si, axis=-1, keepdims=True))
        pm = jnp.exp(sm - m)
        den = jnp.sum(pm, axis=-1, keepdims=True)
        acc = jnp.dot(pm.astype(_BF16), vm_ref[:, hs], preferred_element_type=_F32)
        for si, v_ref in zip(s, (v0_ref, v1_ref, v2_ref)):
            p = jnp.exp(si - m)
            den = den + jnp.sum(p, axis=-1, keepdims=True)
            acc = acc + jnp.dot(p.astype(_BF16), v_ref[:, hs], preferred_element_type=_F32)
        o_ref[:, hs] = (acc / den).astype(o_ref.dtype)


def _na_meta_kernel(q_ref, k_ref, v_ref, o_ref, *, heads):
    nt = (((1,), (1,)), ((), ()))
    for h in range(heads):
        hs = slice(h * HEAD_DIM, (h + 1) * HEAD_DIM)
        s = lax.dot_general(q_ref[:, hs], k_ref[:, hs], nt, preferred_element_type=_F32)
        p = jnp.exp(s - jnp.max(s, axis=-1, keepdims=True))
        acc = jnp.dot(p.astype(_BF16), v_ref[:, hs], preferred_element_type=_F32)
        o_ref[:, hs] = (acc / jnp.sum(p, axis=-1, keepdims=True)).astype(o_ref.dtype)


def _neighbourhood_attention(qkv, trunk, bias):
    d = qkv.shape[1] // 3
    n_heads = d // HEAD_DIM
    hb = min(4, n_heads)
    wd = hb * HEAD_DIM
    nb = trunk.n_tok // Q_BLOCK
    b = trunk.batch
    tok0 = trunk.base + SLOT_PAD + N_META
    meta0 = (trunk.base + SLOT_PAD) // N_META
    slot_meta = trunk.slot // N_META

    def win(col0, shift):
        return _window(Q_BLOCK, wd, lambda h, i, g: (
            tok0 + i * trunk.slot + jnp.clip(g + shift, 0, nb - 1) * Q_BLOCK, col0 + h * wd))

    def meta(col0):
        return _window(N_META, wd, lambda h, i, g: (trunk.base + SLOT_PAD + i * trunk.slot, col0 + h * wd))

    variant = lambda g: jnp.where(g == 0, 0, jnp.where(g == nb - 1, 2, 1))
    out_grid = pl.pallas_call(
        functools.partial(_na_kernel, heads=hb),
        grid=(n_heads // hb, b, nb),
        in_specs=[win(0, 0), win(d, -1), win(d, 0), win(d, 1), win(2 * d, -1), win(2 * d, 0), win(2 * d, 1),
                  meta(d), meta(2 * d),
                  pl.BlockSpec((None, hb, Q_BLOCK, 3 * Q_BLOCK), lambda h, i, g: (variant(g), h, 0, 0))],
        out_specs=pl.BlockSpec((Q_BLOCK, wd), lambda h, i, g: (i * nb + g, h)),
        out_shape=jax.ShapeDtypeStruct((b * trunk.n_tok, d), _BF16),
        compiler_params=_params(("arbitrary", "arbitrary", "arbitrary")),
        name="na_grid",
    )(qkv, qkv, qkv, qkv, qkv, qkv, qkv, qkv, qkv, bias)
    mspec = lambda k: pl.BlockSpec((N_META, d), lambda i: (meta0 + i * slot_meta, k))
    out_meta = pl.pallas_call(
        functools.partial(_na_meta_kernel, heads=n_heads),
        grid=(b,),
        in_specs=[mspec(0), mspec(1), mspec(2)],
        out_specs=pl.BlockSpec((N_META, d), lambda i: (i, 0)),
        out_shape=jax.ShapeDtypeStruct((b * N_META, d), _BF16),
        compiler_params=_params(("arbitrary",)),
        name="na_meta",
    )(qkv, qkv, qkv)
    return jnp.concatenate([out_meta.reshape(b, N_META, d), out_grid.reshape(b, trunk.n_tok, d)], axis=1)


def _na_mixer(h, valid, gamma, trunks, rows, tm, w_qkv, rpb, w_out):
    qkv = _qkv(_plain_norm(h, valid, gamma), w_qkv, tm)
    biases = {}
    parts = []
    for t in trunks:
        grid_rows = t.n_tok // GRID_W
        key = grid_rows if grid_rows < 3 * Q_ROWS else -1
        if key not in biases:
            biases[key] = _attention_bias(rpb, grid_rows)
        parts.append(_neighbourhood_attention(qkv, t, biases[key]))
    return _residual_matmul(_to_flat(parts, trunks, rows, _BF16), w_out, h, tm, "na_out")


def _valid_rows(trunks, rows):
    r = jnp.arange(rows, dtype=jnp.int32)[:, None]
    v = jnp.zeros((rows, 1), jnp.bool_)
    for t in trunks:
        inside = (r >= t.base) & (r < t.base + t.batch * t.slot)
        v = v | (inside & ((r - t.base) % t.slot >= SLOT_PAD))
    return jnp.broadcast_to(v.astype(_F32), (rows, LANES))


def _ffn_up(xp, w_up, w_conv):
    f = w_up.shape[1] // 2
    ws = [w_up[:, :f].astype(_BF16), w_up[:, f:].astype(_BF16)]
    cs = [w_conv[:, :f], w_conv[:, f:]]
    tn = min(_FFN_TN, f)
    nj = f // tn
    a = _conv_matmul(xp, ws, cs, [0, 0], tn, nj, 0, f, _ffn_epilogue, (_BF16,), "ffn_up")
    tail = f - nj * tn
    if tail:
        assert tail % LANES == 0 and (nj * tn) % tail == 0
        a = _conv_matmul(xp, ws, cs, [0, 0], tail, 1, nj * tn // tail, f, _ffn_epilogue, (_BF16,), "ffn_up_tail",
                         onto=a)
    return a[0]


def kernel(x_prompt, x_sample, meta_tokens, norm_mix, norm_ffn, norm_final, hy_w_in, hy_w_conv, hy_f_w1,
           hy_f_w_inner, hy_f_b, hy_f_freq, hy_f_w_out, hy_delta, hy_skip, hy_w_out, na_w_qkv, na_rpb,
           na_w_out, ffn_w_up, ffn_w_conv, ffn_w_down):
    d = x_prompt.shape[-1]
    depth = norm_mix.shape[0]
    xs = (x_prompt, x_sample)
    tm = _TM
    trunks, rows = _plan([(x.shape[0], x.shape[1]) for x in xs], tm)
    valid = _valid_rows(trunks, rows)
    dft = {t.seq: _dft_matrices(t.seq) for t in trunks}

    meta = meta_tokens.astype(_F32)
    seqs = [jnp.concatenate([jnp.broadcast_to(meta[None], (x.shape[0], N_META, d)), x], axis=1) for x in xs]
    h = _to_flat(seqs, trunks, rows, _F32)

    bf = lambda w: w.astype(_BF16)
    for i in range(depth):
        j = i // 2
        if i % 2 == 0:
            filt = (hy_f_w1[j], hy_f_w_inner[j], hy_f_b[j], hy_f_freq[j], hy_f_w_out[j])
            h = _hyena_mixer(h, valid, norm_mix[i], trunks, rows, tm, bf(hy_w_in[j]), hy_w_conv[j], filt,
                             hy_delta[j], hy_skip[j], bf(hy_w_out[j]), dft)
        else:
            h = _na_mixer(h, valid, norm_mix[i], trunks, rows, tm, bf(na_w_qkv[j]), na_rpb[j], bf(na_w_out[j]))
        a = _ffn_up(_halo_norm(h, valid, norm_ffn[i], tm), ffn_w_up[i], ffn_w_conv[i])
        h = _residual_matmul(a, bf(ffn_w_down[i]), h, tm, "ffn_down")
    return tuple(_final_norm(h, norm_final, t) for t in trunks)
```

```python
import functools
import math
from typing import NamedTuple

import numpy as np
import jax
import jax.numpy as jnp
from jax import lax
from jax.experimental import pallas as pl
from jax.experimental.pallas import tpu as pltpu

N_META = 16
GRID_W = 64
HEAD_DIM = 128
WIN_ROWS = 8
WIN_COLS = 16
HY_EMB = 33
HY_BANDS = (HY_EMB - 1) // 2
RMS_EPS = 1e-6

SLOT_PAD = 16
HALO = 16
EPILOGUE_ROWS = 32
MXU_COLS = 256
RESIDENT_WEIGHT_K = 4096
FFT_RADIX = 16
Q_ROWS = 4
Q_BLOCK = Q_ROWS * GRID_W
MASK_VALUE = -1e30
LANES = 128
V7X_VMEM_LIMIT = 56 * 1024 * 1024
DFT_BLOCK_BYTES = 10 * 1024 * 1024
TWIDDLE_LANES = 8

_TM = 640
_FFN_TN = 512
_F32 = jnp.float32
_BF16 = jnp.bfloat16


class _Trunk(NamedTuple):
    batch: int
    n_tok: int
    seq: int
    slot: int
    base: int


def _round_up(x, m):
    return (x + m - 1) // m * m


def _pick(n, target, mult):
    best = None
    for d in range(mult, min(n, target) + 1, mult):
        if n % d == 0:
            best = d
    return best if best is not None else n


def _fft_sizes(seq):
    n1 = _round_up(-(-(2 * seq - 1) // FFT_RADIX), 8)
    n1z = _round_up(-(-seq // FFT_RADIX), 8)
    return n1, n1z


def _plan(shapes, tm):
    trunks, base = [], 0
    for batch, n_tok in shapes:
        seq = N_META + n_tok
        slot = SLOT_PAD + seq
        trunks.append(_Trunk(batch, n_tok, seq, slot, base))
        base += batch * slot
    reach = max(t.base + (t.batch - 1) * t.slot + SLOT_PAD + FFT_RADIX * _fft_sizes(t.seq)[1] for t in trunks)
    rows = _round_up(max(base + SLOT_PAD, reach), tm)
    return trunks, rows


def _params(sem, vmem=V7X_VMEM_LIMIT):
    return pltpu.CompilerParams(dimension_semantics=sem, vmem_limit_bytes=vmem)


def _window(rows, cols, index):
    def aligned(*args):
        r, c = index(*args)
        hint = lambda x, m: x if isinstance(x, int) else pl.multiple_of(x, m)
        return hint(r, SLOT_PAD), hint(c, LANES)
    return pl.BlockSpec((pl.Element(rows), pl.Element(cols)), aligned)


def _norm_rows(h, g, valid):
    y = h * lax.rsqrt(jnp.mean(h * h, axis=-1, keepdims=True) + RMS_EPS)
    return jnp.where(valid[:, 0:1] > 0, y * g, 0.0)


def _final_norm_kernel(h_ref, g_ref, o_ref):
    x = h_ref[...]
    o_ref[...] = x * lax.rsqrt(jnp.mean(x * x, axis=-1, keepdims=True) + RMS_EPS) * g_ref[...]


def _final_norm(h, g, trunk):
    d = h.shape[1]
    tr = _pick(trunk.n_tok, 256, 16)
    nr = trunk.n_tok // tr
    tok0 = trunk.base + SLOT_PAD + N_META
    out = pl.pallas_call(
        _final_norm_kernel,
        grid=(trunk.batch, nr),
        in_specs=[_window(tr, d, lambda i, r: (tok0 + i * trunk.slot + r * tr, 0)),
                  pl.BlockSpec((1, d), lambda i, r: (0, 0))],
        out_specs=pl.BlockSpec((tr, d), lambda i, r: (i * nr + r, 0)),
        out_shape=jax.ShapeDtypeStruct((trunk.batch * trunk.n_tok, d), _F32),
        compiler_params=_params(("arbitrary", "arbitrary")),
        name="final_norm",
    )(h, g.reshape(1, d))
    return out.reshape(trunk.batch, trunk.n_tok, d)


def _conv3(p_ref, k, c, r0, nr):
    at = lambda shift: p_ref[k, HALO + shift + r0:HALO + shift + r0 + nr, :]
    return c[0:1] * at(-1) + c[1:2] * at(0) + c[2:3] * at(1)


def _ffn_epilogue(vals, outs, rows):
    g, v = vals
    outs[0][rows, :] = (g * (1.0 / (1.0 + jnp.exp(-g))) * v).astype(outs[0].dtype)


def _hyena_epilogue(vals, outs, rows):
    x0, x1, v = vals
    outs[0][rows, :] = x0
    outs[1][rows, :] = v * x1


def _halo_norm_kernel(hp_ref, hc_ref, hn_ref, vp_ref, vc_ref, vn_ref, g_ref, o_ref):
    g = g_ref[...]
    packed = lambda h_ref, v_ref: pltpu.bitcast(_norm_rows(h_ref[...], g, v_ref[...]).astype(_BF16), jnp.uint32)
    tm2 = hc_ref.shape[0] // 2
    o_ref[0:HALO // 2] = packed(hp_ref, vp_ref)
    o_ref[HALO // 2:HALO // 2 + tm2] = packed(hc_ref, vc_ref)
    o_ref[HALO // 2 + tm2:HALO + tm2] = packed(hn_ref, vn_ref)


def _halo_norm(h, valid, gamma, tm):
    rows, d = h.shape
    per, last = tm // HALO, rows // HALO - 1

    def panel(width):
        return [pl.BlockSpec((HALO, width), lambda i: (jnp.maximum(i * per - 1, 0), 0)),
                pl.BlockSpec((tm, width), lambda i: (i, 0)),
                pl.BlockSpec((HALO, width), lambda i: (jnp.minimum((i + 1) * per, last), 0))]

    return pl.pallas_call(
        _halo_norm_kernel,
        grid=(rows // tm,),
        in_specs=panel(d) + panel(LANES) + [pl.BlockSpec((1, d), lambda i: (0, 0))],
        out_specs=pl.BlockSpec((None, tm // 2 + HALO, d), lambda i: (i, 0, 0)),
        out_shape=jax.ShapeDtypeStruct((rows // tm, tm // 2 + HALO, d), jnp.uint32),
        compiler_params=_params(("arbitrary",)),
        name="halo_norm",
    )(h, h, h, valid, valid, valid, gamma.reshape(1, d))


def _plain_norm_kernel(h_ref, v_ref, g_ref, o_ref):
    o_ref[...] = _norm_rows(h_ref[...], g_ref[...], v_ref[...]).astype(o_ref.dtype)


def _plain_norm(h, valid, gamma):
    rows, d = h.shape
    tr = _pick(rows, 256, 16)
    return pl.pallas_call(
        _plain_norm_kernel,
        grid=(rows // tr,),
        in_specs=[pl.BlockSpec((tr, d), lambda i: (i, 0)),
                  pl.BlockSpec((tr, LANES), lambda i: (i, 0)),
                  pl.BlockSpec((1, d), lambda i: (0, 0))],
        out_specs=pl.BlockSpec((tr, d), lambda i: (i, 0)),
        out_shape=jax.ShapeDtypeStruct((rows, d), _BF16),
        compiler_params=_params(("arbitrary",)),
        name="rmsnorm",
    )(h, valid, gamma.reshape(1, d))


def _conv_matmul_kernel(*refs, tm, nw, n_onto, epilogue):
    x_ref = refs[0]
    w_refs = refs[1:1 + nw]
    c_refs = refs[1 + nw:1 + 2 * nw]
    out_refs = refs[1 + 2 * nw + n_onto:-2]
    pa_ref, pb_ref = refs[-2:]
    s = pl.program_id(0)

    @pl.when(s == 0)
    def _():
        pb_ref[...] = jnp.zeros_like(pb_ref)

    def step(prev_ref, cur_ref):
        tn = cur_ref.shape[2]
        pieces = [(k, c0) for k in range(nw) for c0 in range(0, tn, MXU_COLS)]
        chunks = list(range(0, tm, EPILOGUE_ROWS))
        per_piece = -(-len(chunks) // len(pieces))
        cs = [c_refs[k][...] for k in range(nw)]
        x = pltpu.bitcast(x_ref[...], _BF16)
        for n, (k, c0) in enumerate(pieces):
            cols = slice(c0, min(c0 + MXU_COLS, tn))
            cur_ref[k, :, cols] = jnp.dot(x, w_refs[k][:, cols], preferred_element_type=_F32)
            for r0 in chunks[n * per_piece:(n + 1) * per_piece]:
                epilogue([_conv3(prev_ref, j, cs[j], r0, EPILOGUE_ROWS) for j in range(nw)], out_refs,
                         slice(r0, r0 + EPILOGUE_ROWS))

    @pl.when(s % 2 == 0)
    def _():
        step(pb_ref, pa_ref)

    @pl.when(s % 2 == 1)
    def _():
        step(pa_ref, pb_ref)


def _conv_matmul(xp, ws, cs, group_tiles, tn, nj, tile0, out_cols, epilogue, out_dtypes, name, onto=()):
    ni, packed_rows, d = xp.shape
    tmh = 2 * packed_rows
    tm = tmh - 2 * HALO
    nw = len(ws)
    steps = ni * nj
    cur = lambda s: jnp.minimum(s, steps - 1)
    lag = lambda s: jnp.maximum(s - 1, 0)
    wspec = lambda k: pl.BlockSpec((d, tn), lambda s: (0, cur(s) // ni + tile0 + group_tiles[k]))
    cspec = lambda k: pl.BlockSpec((3, tn), lambda s: (0, lag(s) // ni + tile0 + group_tiles[k]))
    out = pl.BlockSpec((tm, tn), lambda s: (lag(s) % ni, lag(s) // ni + tile0))
    return pl.pallas_call(
        functools.partial(_conv_matmul_kernel, tm=tm, nw=nw, n_onto=len(onto), epilogue=epilogue),
        grid=(steps + 1,),
        in_specs=[pl.BlockSpec((None, packed_rows, d), lambda s: (cur(s) % ni, 0, 0))]
        + [wspec(k) for k in range(nw)] + [cspec(k) for k in range(nw)]
        + [pl.BlockSpec(memory_space=pl.ANY)] * len(onto),
        out_specs=[out] * len(out_dtypes),
        out_shape=[jax.ShapeDtypeStruct((ni * tm, out_cols), dt) for dt in out_dtypes],
        input_output_aliases={1 + 2 * nw + k: k for k in range(len(onto))},
        scratch_shapes=[pltpu.VMEM((nw, tmh, tn), _F32), pltpu.VMEM((nw, tmh, tn), _F32)],
        compiler_params=_params(("arbitrary",)),
        name=name,
    )(xp, *ws, *cs, *onto)


def _qkv_kernel(x_ref, w_ref, o_ref, *, n_q_tiles, scale):
    acc = jnp.dot(x_ref[...], w_ref[...], preferred_element_type=_F32)
    s = jnp.where(pl.program_id(0) < n_q_tiles, scale, 1.0).astype(_F32)
    o_ref[...] = (acc * s).astype(o_ref.dtype)


def _qkv(hn, w_qkv, tm):
    rows, d = hn.shape
    n = w_qkv.shape[1]
    tn = _pick(d, 1024, LANES)
    return pl.pallas_call(
        functools.partial(_qkv_kernel, n_q_tiles=d // tn, scale=HEAD_DIM ** -0.5),
        grid=(n // tn, rows // tm),
        in_specs=[pl.BlockSpec((tm, d), lambda j, i: (i, 0)),
                  pl.BlockSpec((d, tn), lambda j, i: (0, j))],
        out_specs=pl.BlockSpec((tm, tn), lambda j, i: (i, j)),
        out_shape=jax.ShapeDtypeStruct((rows, n), _BF16),
        compiler_params=_params(("arbitrary", "arbitrary")),
        name="qkv_proj",
    )(hn, w_qkv)


def _residual_kernel(x_ref, w_ref, h_ref, o_ref):
    o_ref[...] = h_ref[...] + jnp.dot(x_ref[...], w_ref[...], preferred_element_type=_F32)


def _residual_matmul(x, w, h, tm, name):
    rows, k = x.shape
    d = w.shape[1]
    weights_resident = k <= RESIDENT_WEIGHT_K
    tn = _pick(d, 1024 if weights_resident else 256, LANES)
    grid = (d // tn, rows // tm) if weights_resident else (rows // tm, d // tn)
    ij = (lambda a, b: (b, a)) if weights_resident else (lambda a, b: (a, b))
    return pl.pallas_call(
        _residual_kernel,
        grid=grid,
        in_specs=[pl.BlockSpec((tm, k), lambda a, b: (ij(a, b)[0], 0)),
                  pl.BlockSpec((k, tn), lambda a, b: (0, ij(a, b)[1])),
                  pl.BlockSpec((tm, tn), lambda a, b: ij(a, b))],
        out_specs=pl.BlockSpec((tm, tn), lambda a, b: ij(a, b)),
        out_shape=jax.ShapeDtypeStruct((rows, d), _F32),
        input_output_aliases={2: 0},
        compiler_params=_params(("arbitrary", "arbitrary")),
        name=name,
    )(x, w, h)


class _Dft(NamedTuple):
    fwd: jax.Array
    inv: jax.Array
    twiddle: jax.Array


def _dft_matrices(seq):
    assert seq % FFT_RADIX == 0
    n1, n1z = _fft_sizes(seq)
    n = FFT_RADIX * n1
    k1 = jnp.arange(n1, dtype=jnp.int32)[:, None]
    m1 = jnp.arange(n1z, dtype=jnp.int32)[None, :]
    phase = ((m1 * k1) % n1).astype(_F32) * (2.0 * math.pi / n1)
    f = jnp.concatenate([jnp.cos(phase), -jnp.sin(phase)], axis=0)
    f = jnp.where(m1 * FFT_RADIX < seq, f, 0.0)
    p = jnp.arange(FFT_RADIX, dtype=jnp.int32)[:, None]
    tphase = (p * k1[:, 0][None, :]).astype(_F32) * (2.0 * math.pi / n)
    tw = jnp.stack([jnp.cos(tphase), -jnp.sin(tphase)], axis=1)
    tw = jnp.broadcast_to(tw[..., None], tw.shape + (TWIDDLE_LANES,))
    return _Dft(f.astype(_BF16), (f.T * (1.0 / n)).astype(_BF16), tw)


def _plane_rows(refs, plane, n1z):
    rows = pl.ds(plane, n1z, stride=FFT_RADIX)
    return jnp.concatenate([ref[rows, :] for ref in refs], axis=1)


def _planes_per_step(n1, dc):
    plane_block = 2 * FFT_RADIX * n1 * dc * 4
    return FFT_RADIX if plane_block <= DFT_BLOCK_BYTES else 1


def _lane_windows(n_rows, dc, row0, row_stride):
    sub = lambda l: _window(n_rows, LANES, lambda i, c, p: (row0 + i * row_stride, c * dc + l * LANES))
    return [sub(l) for l in range(dc // LANES)]


def _paired_dft(f, xa_refs, xb_refs, plane, n1z):
    xa, xb = _plane_rows(xa_refs, plane, n1z), _plane_rows(xb_refs, plane, n1z)
    dc, n1 = xa.shape[1], f.shape[0] // 2
    a = jnp.dot(f, jnp.concatenate([xa, xb], axis=1).astype(_BF16), preferred_element_type=_F32)
    return a[:n1, :dc] - a[n1:, dc:], a[n1:, :dc] + a[:n1, dc:]


def _dft_fwd_kernel(*refs, pp, pair):
    x_refs, f_ref, t_ref, o_ref = refs[:-3], refs[-3], refs[-2], refs[-1]
    n1z, n1 = f_ref.shape[1], o_ref.shape[2]
    half = len(x_refs) // 2
    for q in range(pp):
        plane = pl.program_id(2) * pp + q
        if pair:
            ar, ai = _paired_dft(f_ref[...], x_refs[:half], x_refs[half:], plane, n1z)
        else:
            a = jnp.dot(f_ref[...], _plane_rows(x_refs, plane, n1z).astype(_BF16), preferred_element_type=_F32)
            ar, ai = a[:n1], a[n1:]
        tr, ti = t_ref[q, 0, :, 0:1], t_ref[q, 1, :, 0:1]
        o_ref[0, q] = ar * tr - ai * ti
        o_ref[1, q] = ar * ti + ai * tr


def _dft_fwd(x, dft, n_sig, row0, row_stride, pair=False):
    d = x.shape[1]
    r = FFT_RADIX
    n1, n1z = dft.fwd.shape[0] // 2, dft.fwd.shape[1]
    dc = _pick(d, 256, LANES)
    pp = _planes_per_step(n1, dc)
    if pair:
        assert n_sig % 2 == 0
        n_sig //= 2
        wins = (_lane_windows(n1z * r, dc, row0, 2 * row_stride)
                + _lane_windows(n1z * r, dc, row0 + row_stride, 2 * row_stride))
    else:
        wins = _lane_windows(n1z * r, dc, row0, row_stride)
    return pl.pallas_call(
        functools.partial(_dft_fwd_kernel, pp=pp, pair=pair),
        grid=(n_sig, d // dc, r // pp),
        in_specs=wins + [pl.BlockSpec((2 * n1, n1z), lambda i, c, p: (0, 0)),
                         pl.BlockSpec((pp, 2, n1, TWIDDLE_LANES), lambda i, c, p: (p, 0, 0, 0))],
        out_specs=pl.BlockSpec((2, pp, None, n1, dc), lambda i, c, p: (0, p, i, 0, c)),
        out_shape=jax.ShapeDtypeStruct((2, r, n_sig, n1, d), _F32),
        compiler_params=_params(("arbitrary", "arbitrary", "arbitrary")),
        name="dft_fwd",
    )(*([x] * len(wins)), dft.fwd, dft.twiddle)


def _cmul_const(x, w):
    re, im = x
    wr, wi = float(w.real), float(w.imag)
    if abs(wi) < 1e-12:
        return (re, im) if wr > 0 else (-re, -im)
    if abs(wr) < 1e-12:
        return (-im, re) if wi > 0 else (im, -re)
    return (re * wr - im * wi, re * wi + im * wr)


def _fft_planes(xs, sign):
    n = len(xs)
    if n == 1:
        return xs
    ev = _fft_planes(xs[0::2], sign)
    od = _fft_planes(xs[1::2], sign)
    out = [None] * n
    for k in range(n // 2):
        w = complex(math.cos(2.0 * math.pi * k / n), sign * math.sin(2.0 * math.pi * k / n))
        tr, ti = _cmul_const(od[k], w)
        out[k] = (ev[k][0] + tr, ev[k][1] + ti)
        out[k + n // 2] = (ev[k][0] - tr, ev[k][1] - ti)
    return out


def _tile_loop(tk, dc, body):
    def step(r, carry):
        rows = pl.ds(pl.multiple_of(r * 8, 8), 8)
        for l in range(dc // LANES):
            body(rows, slice(l * LANES, (l + 1) * LANES))
        return carry
    lax.fori_loop(0, tk // 8, step, 0)


def _filter_spectrum_kernel(a_ref, s_ref, o_ref):
    r, tk, dc = o_ref.shape[1], o_ref.shape[2], o_ref.shape[3]

    def body(rows, lanes):
        inv = 1.0 / s_ref[:, lanes]
        xf = _fft_planes([(a_ref[0, n, 0, rows, lanes], a_ref[1, n, 0, rows, lanes]) for n in range(r)], -1)
        xb = _fft_planes([(a_ref[0, n, 1, rows, lanes], a_ref[1, n, 1, rows, lanes]) for n in range(r)], -1)
        for k in range(r):
            o_ref[0, k, rows, lanes] = (xf[k][0] + xb[k][0]) * inv
            o_ref[1, k, rows, lanes] = (xf[k][1] - xb[k][1]) * inv

    _tile_loop(tk, dc, body)


def _plane_tile(n1):
    return _pick(n1, 352, 8)


def _filter_spectrum(planes, norm):
    _, r, _, n1, d = planes.shape
    tk, dc = _plane_tile(n1), LANES
    return pl.pallas_call(
        _filter_spectrum_kernel,
        grid=(d // dc, n1 // tk),
        in_specs=[pl.BlockSpec((2, r, 2, tk, dc), lambda c, k: (0, 0, 0, k, c)),
                  pl.BlockSpec((1, dc), lambda c, k: (0, c))],
        out_specs=pl.BlockSpec((2, r, tk, dc), lambda c, k: (0, 0, k, c)),
        out_shape=jax.ShapeDtypeStruct((2, r, n1, d), _F32),
        compiler_params=_params(("arbitrary", "arbitrary")),
        name="filter_spectrum",
    )(planes, norm)


def _spectral_product_kernel(a_ref, k_ref, o_ref):
    r, tk, dc = o_ref.shape[1], o_ref.shape[2], o_ref.shape[3]

    def body(rows, lanes):
        x = _fft_planes([(a_ref[0, n, rows, lanes], a_ref[1, n, rows, lanes]) for n in range(r)], -1)
        y = []
        for k in range(r):
            kr, ki = k_ref[0, k, rows, lanes], k_ref[1, k, rows, lanes]
            y.append((x[k][0] * kr - x[k][1] * ki, x[k][0] * ki + x[k][1] * kr))
        z = _fft_planes(y, +1)
        for n in range(r):
            o_ref[0, n, rows, lanes] = z[n][0]
            o_ref[1, n, rows, lanes] = z[n][1]

    _tile_loop(tk, dc, body)


def _spectral_product(planes, kspec):
    _, r, b, n1, d = planes.shape
    tk, dc = _plane_tile(n1), LANES
    spec = pl.BlockSpec((2, r, None, tk, dc), lambda c, k, i: (0, 0, i, k, c))
    return pl.pallas_call(
        _spectral_product_kernel,
        grid=(d // dc, n1 // tk, b),
        in_specs=[spec, pl.BlockSpec((2, r, tk, dc), lambda c, k, i: (0, 0, k, c))],
        out_specs=spec,
        out_shape=jax.ShapeDtypeStruct(planes.shape, _F32),
        compiler_params=_params(("arbitrary", "arbitrary", "arbitrary")),
        name="spectral_product",
    )(planes, kspec)


def _untwiddle(br, bi, tr, ti):
    return br * tr + bi * ti, bi * tr - br * ti


def _dft_inv_kernel(*refs, pp, nsub, pair):
    b_ref, h_ref, t_ref, skip_ref, o_ref = refs[0], refs[1], refs[2], refs[-2], refs[-1]
    vx_refs, x0_refs = refs[3:3 + nsub], refs[3 + nsub:3 + 2 * nsub]
    n1z = h_ref.shape[0]
    for q in range(pp):
        plane = pl.program_id(2) * pp + q
        cr, ci = _untwiddle(b_ref[0, q], b_ref[1, q], t_ref[q, 0, :, 0:1], t_ref[q, 1, :, 0:1])
        if pair:
            even = pl.program_id(0) % 2 == 0
            cr, ci = jnp.where(even, cr, ci), jnp.where(even, ci, -cr)
        bb = jnp.concatenate([cr, ci], axis=0).astype(_BF16)
        y = jnp.dot(h_ref[...], bb, preferred_element_type=_F32)
        o_ref[q] = (y + _plane_rows(vx_refs, plane, n1z) * skip_ref[...]) * _plane_rows(x0_refs, plane, n1z)


def _dft_inv(planes, dft, vx, x0, skip, batch, row0, row_stride, pair=False):
    _, r, _, n1, d = planes.shape
    b = batch
    n1z = dft.inv.shape[0]
    dc = _pick(d, 256, LANES)
    pp = _planes_per_step(n1, dc)
    wins = _lane_windows(n1z * r, dc, row0, row_stride)
    which = (lambda i: i // 2) if pair else (lambda i: i)
    return pl.pallas_call(
        functools.partial(_dft_inv_kernel, pp=pp, nsub=len(wins), pair=pair),
        grid=(b, d // dc, r // pp),
        in_specs=[pl.BlockSpec((2, pp, None, n1, dc), lambda i, c, p: (0, p, which(i), 0, c)),
                  pl.BlockSpec((n1z, 2 * n1), lambda i, c, p: (0, 0)),
                  pl.BlockSpec((pp, 2, n1, TWIDDLE_LANES), lambda i, c, p: (p, 0, 0, 0))]
        + wins + wins + [pl.BlockSpec((1, dc), lambda i, c, p: (0, c))],
        out_specs=pl.BlockSpec((None, pp, n1z, dc), lambda i, c, p: (i, p, 0, c)),
        out_shape=jax.ShapeDtypeStruct((b, r, n1z, d), _F32),
        compiler_params=_params(("arbitrary", "arbitrary", "arbitrary")),
        name="dft_inv",
    )(planes, dft.inv, dft.twiddle, *([vx] * len(wins)), *([x0] * len(wins)), skip)


def _fused_conv_kernel(*refs, nsub):
    vxa, vxb, x0a, x0b = (refs[k * nsub:(k + 1) * nsub] for k in range(4))
    f_ref, h_ref, t_ref, k_ref, skip_ref, o_ref, a_ref = refs[4 * nsub:]
    r, n1, dc = a_ref.shape[1], a_ref.shape[2], a_ref.shape[3]
    n1z = f_ref.shape[1]
    for p in range(r):
        ar, ai = _paired_dft(f_ref[...], vxa, vxb, p, n1z)
        tr, ti = t_ref[p, 0, :, 0:1], t_ref[p, 1, :, 0:1]
        a_ref[0, p] = ar * tr - ai * ti
        a_ref[1, p] = ar * ti + ai * tr

    def body(rows, lanes):
        x = _fft_planes([(a_ref[0, n, rows, lanes], a_ref[1, n, rows, lanes]) for n in range(r)], -1)
        y = []
        for k in range(r):
            kr, ki = k_ref[0, k, rows, lanes], k_ref[1, k, rows, lanes]
            y.append((x[k][0] * kr - x[k][1] * ki, x[k][0] * ki + x[k][1] * kr))
        z = _fft_planes(y, +1)
        for n in range(r):
            a_ref[0, n, rows, lanes] = z[n][0]
            a_ref[1, n, rows, lanes] = z[n][1]

    _tile_loop(n1, dc, body)
    for p in range(r):
        cr, ci = _untwiddle(a_ref[0, p], a_ref[1, p], t_ref[p, 0, :, 0:1], t_ref[p, 1, :, 0:1])
        bb = jnp.concatenate([jnp.concatenate([cr, ci], axis=0), jnp.concatenate([ci, -cr], axis=0)], axis=1)
        y = jnp.dot(h_ref[...], bb.astype(_BF16), preferred_element_type=_F32)
        o_ref[0, p] = (y[:, :dc] + _plane_rows(vxa, p, n1z) * skip_ref[...]) * _plane_rows(x0a, p, n1z)
        o_ref[1, p] = (y[:, dc:] + _plane_rows(vxb, p, n1z) * skip_ref[...]) * _plane_rows(x0b, p, n1z)


def _fused_conv(vx, x0, kspec, dft, skip, batch, row0, row_stride):
    assert batch % 2 == 0
    d = vx.shape[1]
    r = FFT_RADIX
    n1, n1z = dft.fwd.shape[0] // 2, dft.fwd.shape[1]
    dc = LANES
    sig = lambda spec: pl.BlockSpec(spec.block_shape, lambda c, i: spec.index_map(i, c, 0))
    wa = [sig(w) for w in _lane_windows(n1z * r, dc, row0, 2 * row_stride)]
    wb = [sig(w) for w in _lane_windows(n1z * r, dc, row0 + row_stride, 2 * row_stride)]
    return pl.pallas_call(
        functools.partial(_fused_conv_kernel, nsub=len(wa)),
        grid=(d // dc, batch // 2),
        in_specs=wa + wb + wa + wb + [
            pl.BlockSpec((2 * n1, n1z), lambda c, i: (0, 0)),
            pl.BlockSpec((n1z, 2 * n1), lambda c, i: (0, 0)),
            pl.BlockSpec((r, 2, n1, TWIDDLE_LANES), lambda c, i: (0, 0, 0, 0)),
            pl.BlockSpec((2, r, n1, dc), lambda c, i: (0, 0, 0, c)),
            pl.BlockSpec((1, dc), lambda c, i: (0, c))],
        out_specs=pl.BlockSpec((2, r, n1z, dc), lambda c, i: (i, 0, 0, c)),
        out_shape=jax.ShapeDtypeStruct((batch, r, n1z, d), _F32),
        scratch_shapes=[pltpu.VMEM((2, r, n1, dc), _F32)],
        compiler_params=_params(("arbitrary", "arbitrary")),
        name="hyena_conv",
    )(*([vx] * (2 * len(wa))), *([x0] * (2 * len(wa))), dft.fwd, dft.inv, dft.twiddle, kspec, skip)


def _filter_mlp_kernel(z_ref, w1_ref, wi_ref, b_ref, f_ref, o_ref):
    hi = lax.Precision.HIGHEST
    h = jnp.sin(f_ref[0:1] * (jnp.dot(z_ref[...], w1_ref[...], precision=hi, preferred_element_type=_F32) + b_ref[0:1]))
    h = jnp.sin(f_ref[1:2] * (jnp.dot(h, wi_ref[0], precision=hi, preferred_element_type=_F32) + b_ref[1:2]))
    h = jnp.sin(f_ref[2:3] * (jnp.dot(h, wi_ref[1], precision=hi, preferred_element_type=_F32) + b_ref[2:3]))
    o_ref[...] = h


def _filter_taps_kernel(hm_ref, z_ref, wf_ref, wb_ref, d_ref, k_ref, s_ref):
    hi = lax.Precision.HIGHEST
    t = z_ref[:, 0:1]
    mf = z_ref[:, HY_EMB:HY_EMB + 1]
    mb = z_ref[:, HY_EMB + 1:HY_EMB + 2]
    hm = hm_ref[...]
    kf = jnp.dot(hm, wf_ref[...], precision=hi, preferred_element_type=_F32) * jnp.exp(-t * jnp.abs(d_ref[0:1])) * mf
    kb = jnp.dot(hm, wb_ref[...], precision=hi, preferred_element_type=_F32) * jnp.exp(-t * jnp.abs(d_ref[1:2])) * mb
    k_ref[0] = kf
    k_ref[1] = kb

    @pl.when(pl.program_id(1) == 0)
    def _():
        s_ref[...] = jnp.zeros_like(s_ref)
    s_ref[...] += jnp.sum(jnp.abs(kf) + jnp.abs(kb), axis=0, keepdims=True)


def _position_features(seq, rows):
    pos = jnp.arange(seq, dtype=_F32)[:, None]
    t = pos / max(seq - 1, 1)
    f = jnp.linspace(1e-4, HY_BANDS - 1, HY_BANDS, dtype=_F32)[None, :]
    ang = f * (2.0 * math.pi / seq) * pos
    fwd = jnp.ones((seq, 1), _F32)
    bwd = (pos >= 1).astype(_F32)
    z = jnp.concatenate([t, jnp.cos(ang), -jnp.sin(ang), fwd, bwd], axis=-1)
    return jnp.pad(z, ((0, rows - seq), (0, LANES - z.shape[1])))


def _hyena_filter(seq, w1, w_inner, b, freq, w_out, delta, dft):
    _, n1z = _fft_sizes(seq)
    lp = n1z * FFT_RADIX
    width = w1.shape[1]
    d = w_out.shape[1] // 2
    z = _position_features(seq, lp)
    w1p = jnp.pad(w1, ((0, LANES - w1.shape[0]), (0, 0)))
    tr = n1z
    hm = pl.pallas_call(
        _filter_mlp_kernel,
        grid=(lp // tr,),
        in_specs=[pl.BlockSpec((tr, LANES), lambda i: (i, 0)),
                  pl.BlockSpec((LANES, width), lambda i: (0, 0)),
                  pl.BlockSpec((2, width, width), lambda i: (0, 0, 0)),
                  pl.BlockSpec((3, width), lambda i: (0, 0)),
                  pl.BlockSpec((3, width), lambda i: (0, 0))],
        out_specs=pl.BlockSpec((tr, width), lambda i: (i, 0)),
        out_shape=jax.ShapeDtypeStruct((lp, width), _F32),
        compiler_params=_params(("arbitrary",)),
        name="filter_mlp",
    )(z, w1p, w_inner, b, freq)
    dc = _pick(d, 512, LANES)
    nc = d // dc
    taps, norm = pl.pallas_call(
        _filter_taps_kernel,
        grid=(nc, lp // tr),
        in_specs=[pl.BlockSpec((tr, width), lambda c, i: (i, 0)),
                  pl.BlockSpec((tr, LANES), lambda c, i: (i, 0)),
                  pl.BlockSpec((width, dc), lambda c, i: (0, c)),
                  pl.BlockSpec((width, dc), lambda c, i: (0, c + nc)),
                  pl.BlockSpec((2, dc), lambda c, i: (0, c))],
        out_specs=[pl.BlockSpec((2, tr, dc), lambda c, i: (0, i, c)), pl.BlockSpec((1, dc), lambda c, i: (0, c))],
        out_shape=[jax.ShapeDtypeStruct((2, lp, d), _F32), jax.ShapeDtypeStruct((1, d), _F32)],
        compiler_params=_params(("arbitrary", "arbitrary")),
        name="filter_taps",
    )(hm, z, w_out, w_out, delta)
    return _filter_spectrum(_dft_fwd(taps.reshape(2 * lp, d), dft, 2, 0, lp), norm)


def _to_flat(parts, trunks, rows, dtype):
    d = parts[0].shape[-1]
    out = []
    for part, t in zip(parts, trunks):
        out.append(jnp.pad(part.astype(dtype), ((0, 0), (SLOT_PAD, 0), (0, 0))).reshape(t.batch * t.slot, d))
    used = sum(t.batch * t.slot for t in trunks)
    out.append(jnp.zeros((rows - used, d), dtype))
    return jnp.concatenate(out, axis=0)


def _hyena_mixer(h, valid, gamma, trunks, rows, tm, w_in, w_conv, filt, delta, skip, w_out, dft):
    d = h.shape[1]
    tn = _pick(d, 512, LANES)
    nj = d // tn
    x0, vx = _conv_matmul(_halo_norm(h, valid, gamma, tm), [w_in] * 3, [w_conv] * 3, [0, nj, 2 * nj], tn, nj, 0, d,
                          _hyena_epilogue, (_F32, _F32), "hyena_in")
    parts = []
    for t in trunks:
        kspec = _hyena_filter(t.seq, *filt, delta, dft[t.seq])
        row0 = t.base + SLOT_PAD
        n1 = dft[t.seq].fwd.shape[0] // 2
        pair = t.batch % 2 == 0
        if pair and _planes_per_step(n1, _pick(d, 256, LANES)) == FFT_RADIX:
            y = _fused_conv(vx, x0, kspec, dft[t.seq], skip.reshape(1, -1), t.batch, row0, t.slot)
        else:
            planes = _spectral_product(_dft_fwd(vx, dft[t.seq], t.batch, row0, t.slot, pair), kspec)
            y = _dft_inv(planes, dft[t.seq], vx, x0, skip.reshape(1, -1), t.batch, row0, t.slot, pair)
        parts.append(jnp.swapaxes(y, 1, 2).reshape(t.batch, -1, d)[:, :t.seq])
    return _residual_matmul(_to_flat(parts, trunks, rows, _BF16), w_out, h, tm, "hyena_out")


def _attention_bias(rpb, rows_in_grid):
    n_blocks = rows_in_grid // Q_ROWS
    kr_win = min(WIN_ROWS, rows_in_grid)
    qr = np.arange(Q_ROWS)[:, None, None]
    kb = np.arange(3)[None, :, None]
    kr = np.arange(Q_ROWS)[None, None, :]
    dr_idx, row_ok = [], []
    for g in (0, 1, n_blocks - 1):
        r = Q_ROWS * g + qr
        key_row = Q_ROWS * (g + kb - 1) + kr
        rs = np.clip(r - kr_win // 2, 0, rows_in_grid - kr_win)
        ok = (key_row >= rs) & (key_row < rs + kr_win) & (g + kb - 1 >= 0) & (g + kb - 1 < n_blocks)
        dr_idx.append(np.clip(key_row - r + (WIN_ROWS - 1), 0, 2 * WIN_ROWS - 2))
        row_ok.append(ok)
    dr_idx = np.stack(dr_idx)
    row_ok = np.stack(row_ok)
    cols = np.arange(GRID_W)
    col_start = np.clip(cols - WIN_COLS // 2, 0, GRID_W - WIN_COLS)
    col_ok = (cols[None, :] >= col_start[:, None]) & (cols[None, :] < col_start[:, None] + WIN_COLS)
    col_idx = np.clip(cols[None, :] - cols[:, None] + WIN_COLS - 1, 0, 2 * WIN_COLS - 2)
    by_col = rpb[:, :, col_idx]
    by_row = by_col[:, dr_idx]
    ok = row_ok[:, :, :, :, None, None] & col_ok[None, None, None, None]
    bias = jnp.where(ok[None], by_row, MASK_VALUE)
    bias = jnp.transpose(bias, (1, 0, 2, 5, 3, 4, 6))
    return bias.reshape(3, rpb.shape[0], Q_BLOCK, 3 * Q_BLOCK).astype(_F32)


def _na_kernel(q_ref, k0_ref, k1_ref, k2_ref, v0_ref, v1_ref, v2_ref, km_ref, vm_ref, bias_ref, o_ref, *, heads):
    nt = (((1,), (1,)), ((), ()))
    for h in range(heads):
        hs = slice(h * HEAD_DIM, (h + 1) * HEAD_DIM)
        q = q_ref[:, hs]
        s = [lax.dot_general(q, k_ref[:, hs], nt, preferred_element_type=_F32)
             + bias_ref[h, :, i * Q_BLOCK:(i + 1) * Q_BLOCK]
             for i, k_ref in enumerate((k0_ref, k1_ref, k2_ref))]
        sm = lax.dot_general(q, km_ref[:, hs], nt, preferred_element_type=_F32)
        m = jnp.max(sm, axis=-1, keepdims=True)
        for si in s:
            m = jnp.maximum(m, jnp.max(si, axis=-1, keepdims=True))
        pm = jnp.exp(sm - m)
        den = jnp.sum(pm, axis=-1, keepdims=True)
        acc = jnp.dot(pm.astype(_BF16), vm_ref[:, hs], preferred_element_type=_F32)
        for si, v_ref in zip(s, (v0_ref, v1_ref, v2_ref)):
            p = jnp.exp(si - m)
            den = den + jnp.sum(p, axis=-1, keepdims=True)
            acc = acc + jnp.dot(p.astype(_BF16), v_ref[:, hs], preferred_element_type=_F32)
        o_ref[:, hs] = (acc / den).astype(o_ref.dtype)


def _na_meta_kernel(q_ref, k_ref, v_ref, o_ref, *, heads):
    nt = (((1,), (1,)), ((), ()))
    for h in range(heads):
        hs = slice(h * HEAD_DIM, (h + 1) * HEAD_DIM)
        s = lax.dot_general(q_ref[:, hs], k_ref[:, hs], nt, preferred_element_type=_F32)
        p = jnp.exp(s - jnp.max(s, axis=-1, keepdims=True))
        acc = jnp.dot(p.astype(_BF16), v_ref[:, hs], preferred_element_type=_F32)
        o_ref[:, hs] = (acc / jnp.sum(p, axis=-1, keepdims=True)).astype(o_ref.dtype)


def _neighbourhood_attention(qkv, trunk, bias):
    d = qkv.shape[1] // 3
    n_heads = d // HEAD_DIM
    hb = min(4, n_heads)
    wd = hb * HEAD_DIM
    nb = trunk.n_tok // Q_BLOCK
    b = trunk.batch
    tok0 = trunk.base + SLOT_PAD + N_META
    meta0 = (trunk.base + SLOT_PAD) // N_META
    slot_meta = trunk.slot // N_META

    def win(col0, shift):
        return _window(Q_BLOCK, wd, lambda h, i, g: (
            tok0 + i * trunk.slot + jnp.clip(g + shift, 0, nb - 1) * Q_BLOCK, col0 + h * wd))

    def meta(col0):
        return _window(N_META, wd, lambda h, i, g: (trunk.base + SLOT_PAD + i * trunk.slot, col0 + h * wd))

    variant = lambda g: jnp.where(g == 0, 0, jnp.where(g == nb - 1, 2, 1))
    out_grid = pl.pallas_call(
        functools.partial(_na_kernel, heads=hb),
        grid=(n_heads // hb, b, nb),
        in_specs=[win(0, 0), win(d, -1), win(d, 0), win(d, 1), win(2 * d, -1), win(2 * d, 0), win(2 * d, 1),
                  meta(d), meta(2 * d),
                  pl.BlockSpec((None, hb, Q_BLOCK, 3 * Q_BLOCK), lambda h, i, g: (variant(g), h, 0, 0))],
        out_specs=pl.BlockSpec((Q_BLOCK, wd), lambda h, i, g: (i * nb + g, h)),
        out_shape=jax.ShapeDtypeStruct((b * trunk.n_tok, d), _BF16),
        compiler_params=_params(("arbitrary", "arbitrary", "arbitrary")),
        name="na_grid",
    )(qkv, qkv, qkv, qkv, qkv, qkv, qkv, qkv, qkv, bias)
    mspec = lambda k: pl.BlockSpec((N_META, d), lambda i: (meta0 + i * slot_meta, k))
    out_meta = pl.pallas_call(
        functools.partial(_na_meta_kernel, heads=n_heads),
        grid=(b,),
        in_specs=[mspec(0), mspec(1), mspec(2)],
        out_specs=pl.BlockSpec((N_META, d), lambda i: (i, 0)),
        out_shape=jax.ShapeDtypeStruct((b * N_META, d), _BF16),
        compiler_params=_params(("arbitrary",)),
        name="na_meta",
    )(qkv, qkv, qkv)
    return jnp.concatenate([out_meta.reshape(b, N_META, d), out_grid.reshape(b, trunk.n_tok, d)], axis=1)


def _na_mixer(h, valid, gamma, trunks, rows, tm, w_qkv, rpb, w_out):
    qkv = _qkv(_plain_norm(h, valid, gamma), w_qkv, tm)
    biases = {}
    parts = []
    for t in trunks:
        grid_rows = t.n_tok // GRID_W
        key = grid_rows if grid_rows < 3 * Q_ROWS else -1
        if key not in biases:
            biases[key] = _attention_bias(rpb, grid_rows)
        parts.append(_neighbourhood_attention(qkv, t, biases[key]))
    return _residual_matmul(_to_flat(parts, trunks, rows, _BF16), w_out, h, tm, "na_out")


def _valid_rows(trunks, rows):
    r = jnp.arange(rows, dtype=jnp.int32)[:, None]
    v = jnp.zeros((rows, 1), jnp.bool_)
    for t in trunks:
        inside = (r >= t.base) & (r < t.base + t.batch * t.slot)
        v = v | (inside & ((r - t.base) % t.slot >= SLOT_PAD))
    return jnp.broadcast_to(v.astype(_F32), (rows, LANES))


def _ffn_up(xp, w_up, w_conv):
    f = w_up.shape[1] // 2
    ws = [w_up[:, :f].astype(_BF16), w_up[:, f:].astype(_BF16)]
    cs = [w_conv[:, :f], w_conv[:, f:]]
    tn = min(_FFN_TN, f)
    nj = f // tn
    a = _conv_matmul(xp, ws, cs, [0, 0], tn, nj, 0, f, _ffn_epilogue, (_BF16,), "ffn_up")
    tail = f - nj * tn
    if tail:
        assert tail % LANES == 0 and (nj * tn) % tail == 0
        a = _conv_matmul(xp, ws, cs, [0, 0], tail, 1, nj * tn // tail, f, _ffn_epilogue, (_BF16,), "ffn_up_tail",
                         onto=a)
    return a[0]


def kernel(x_prompt, x_sample, meta_tokens, norm_mix, norm_ffn, norm_final, hy_w_in, hy_w_conv, hy_f_w1,
           hy_f_w_inner, hy_f_b, hy_f_freq, hy_f_w_out, hy_delta, hy_skip, hy_w_out, na_w_qkv, na_rpb,
           na_w_out, ffn_w_up, ffn_w_conv, ffn_w_down):
    d = x_prompt.shape[-1]
    depth = norm_mix.shape[0]
    xs = (x_prompt, x_sample)
    tm = _TM
    trunks, rows = _plan([(x.shape[0], x.shape[1]) for x in xs], tm)
    valid = _valid_rows(trunks, rows)
    dft = {t.seq: _dft_matrices(t.seq) for t in trunks}

    meta = meta_tokens.astype(_F32)
    seqs = [jnp.concatenate([jnp.broadcast_to(meta[None], (x.shape[0], N_META, d)), x], axis=1) for x in xs]
    h = _to_flat(seqs, trunks, rows, _F32)

    bf = lambda w: w.astype(_BF16)
    for i in range(depth):
        j = i // 2
        if i % 2 == 0:
            filt = (hy_f_w1[j], hy_f_w_inner[j], hy_f_b[j], hy_f_freq[j], hy_f_w_out[j])
            h = _hyena_mixer(h, valid, norm_mix[i], trunks, rows, tm, bf(hy_w_in[j]), hy_w_conv[j], filt,
                             hy_delta[j], hy_skip[j], bf(hy_w_out[j]), dft)
        else:
            h = _na_mixer(h, valid, norm_mix[i], trunks, rows, tm, bf(na_w_qkv[j]), na_rpb[j], bf(na_w_out[j]))
        a = _ffn_up(_halo_norm(h, valid, norm_ffn[i], tm), ffn_w_up[i], ffn_w_conv[i])
        h = _residual_matmul(a, bf(ffn_w_down[i]), h, tm, "ffn_down")
    return tuple(_final_norm(h, norm_final, t) for t in trunks)
```

```python
import functools
import math
from typing import NamedTuple

import numpy as np
import jax
import jax.numpy as jnp
from jax import lax
from jax.experimental import pallas as pl
from jax.experimental.pallas import tpu as pltpu

N_META = 16
GRID_W = 64
HEAD_DIM = 128
WIN_ROWS = 8
WIN_COLS = 16
HY_EMB = 33
HY_BANDS = (HY_EMB - 1) // 2
RMS_EPS = 1e-6

SLOT_PAD = 16
HALO = 16
EPILOGUE_ROWS = 32
MXU_COLS = 256
RESIDENT_WEIGHT_K = 4096
FFT_RADIX = 16
Q_ROWS = 4
Q_BLOCK = Q_ROWS * GRID_W
MASK_VALUE = -1e30
LANES = 128
V7X_VMEM_LIMIT = 56 * 1024 * 1024
DFT_BLOCK_BYTES = 10 * 1024 * 1024
TWIDDLE_LANES = 8

_TM = 640
_FFN_TN = 512
_F32 = jnp.float32
_BF16 = jnp.bfloat16


class _Trunk(NamedTuple):
    batch: int
    n_tok: int
    seq: int
    slot: int
    base: int


def _round_up(x, m):
    return (x + m - 1) // m * m


def _pick(n, target, mult):
    best = None
    for d in range(mult, min(n, target) + 1, mult):
        if n % d == 0:
            best = d
    return best if best is not None else n


def _fft_sizes(seq):
    n1 = _round_up(-(-(2 * seq - 1) // FFT_RADIX), 8)
    n1z = _round_up(-(-seq // FFT_RADIX), 8)
    return n1, n1z


def _plan(shapes, tm):
    trunks, base = [], 0
    for batch, n_tok in shapes:
        seq = N_META + n_tok
        slot = SLOT_PAD + seq
        trunks.append(_Trunk(batch, n_tok, seq, slot, base))
        base += batch * slot
    reach = max(t.base + (t.batch - 1) * t.slot + SLOT_PAD + FFT_RADIX * _fft_sizes(t.seq)[1] for t in trunks)
    rows = _round_up(max(base + SLOT_PAD, reach), tm)
    return trunks, rows


def _params(sem, vmem=V7X_VMEM_LIMIT):
    return pltpu.CompilerParams(dimension_semantics=sem, vmem_limit_bytes=vmem)


def _window(rows, cols, index):
    def aligned(*args):
        r, c = index(*args)
        hint = lambda x, m: x if isinstance(x, int) else pl.multiple_of(x, m)
        return hint(r, SLOT_PAD), hint(c, LANES)
    return pl.BlockSpec((pl.Element(rows), pl.Element(cols)), aligned)


def _norm_rows(h, g, valid):
    y = h * lax.rsqrt(jnp.mean(h * h, axis=-1, keepdims=True) + RMS_EPS)
    return jnp.where(valid[:, 0:1] > 0, y * g, 0.0)


def _final_norm_kernel(h_ref, g_ref, o_ref):
    x = h_ref[...]
    o_ref[...] = x * lax.rsqrt(jnp.mean(x * x, axis=-1, keepdims=True) + RMS_EPS) * g_ref[...]


def _final_norm(h, g, trunk):
    d = h.shape[1]
    tr = _pick(trunk.n_tok, 256, 16)
    nr = trunk.n_tok // tr
    tok0 = trunk.base + SLOT_PAD + N_META
    out = pl.pallas_call(
        _final_norm_kernel,
        grid=(trunk.batch, nr),
        in_specs=[_window(tr, d, lambda i, r: (tok0 + i * trunk.slot + r * tr, 0)),
                  pl.BlockSpec((1, d), lambda i, r: (0, 0))],
        out_specs=pl.BlockSpec((tr, d), lambda i, r: (i * nr + r, 0)),
        out_shape=jax.ShapeDtypeStruct((trunk.batch * trunk.n_tok, d), _F32),
        compiler_params=_params(("arbitrary", "arbitrary")),
        name="final_norm",
    )(h, g.reshape(1, d))
    return out.reshape(trunk.batch, trunk.n_tok, d)


def _conv3(p_ref, k, c, r0, nr):
    at = lambda shift: p_ref[k, HALO + shift + r0:HALO + shift + r0 + nr, :]
    return c[0:1] * at(-1) + c[1:2] * at(0) + c[2:3] * at(1)


def _ffn_epilogue(vals, outs, rows):
    g, v = vals
    outs[0][rows, :] = (g * (1.0 / (1.0 + jnp.exp(-g))) * v).astype(outs[0].dtype)


def _hyena_epilogue(vals, outs, rows):
    x0, x1, v = vals
    outs[0][rows, :] = x0
    outs[1][rows, :] = v * x1


def _halo_norm_kernel(hp_ref, hc_ref, hn_ref, vp_ref, vc_ref, vn_ref, g_ref, o_ref):
    g = g_ref[...]
    packed = lambda h_ref, v_ref: pltpu.bitcast(_norm_rows(h_ref[...], g, v_ref[...]).astype(_BF16), jnp.uint32)
    tm2 = hc_ref.shape[0] // 2
    o_ref[0:HALO // 2] = packed(hp_ref, vp_ref)
    o_ref[HALO // 2:HALO // 2 + tm2] = packed(hc_ref, vc_ref)
    o_ref[HALO // 2 + tm2:HALO + tm2] = packed(hn_ref, vn_ref)


def _halo_norm(h, valid, gamma, tm):
    rows, d = h.shape
    per, last = tm // HALO, rows // HALO - 1

    def panel(width):
        return [pl.BlockSpec((HALO, width), lambda i: (jnp.maximum(i * per - 1, 0), 0)),
                pl.BlockSpec((tm, width), lambda i: (i, 0)),
                pl.BlockSpec((HALO, width), lambda i: (jnp.minimum((i + 1) * per, last), 0))]

    return pl.pallas_call(
        _halo_norm_kernel,
        grid=(rows // tm,),
        in_specs=panel(d) + panel(LANES) + [pl.BlockSpec((1, d), lambda i: (0, 0))],
        out_specs=pl.BlockSpec((None, tm // 2 + HALO, d), lambda i: (i, 0, 0)),
        out_shape=jax.ShapeDtypeStruct((rows // tm, tm // 2 + HALO, d), jnp.uint32),
        compiler_params=_params(("arbitrary",)),
        name="halo_norm",
    )(h, h, h, valid, valid, valid, gamma.reshape(1, d))


def _plain_norm_kernel(h_ref, v_ref, g_ref, o_ref):
    o_ref[...] = _norm_rows(h_ref[...], g_ref[...], v_ref[...]).astype(o_ref.dtype)


def _plain_norm(h, valid, gamma):
    rows, d = h.shape
    tr = _pick(rows, 256, 16)
    return pl.pallas_call(
        _plain_norm_kernel,
        grid=(rows // tr,),
        in_specs=[pl.BlockSpec((tr, d), lambda i: (i, 0)),
                  pl.BlockSpec((tr, LANES), lambda i: (i, 0)),
                  pl.BlockSpec((1, d), lambda i: (0, 0))],
        out_specs=pl.BlockSpec((tr, d), lambda i: (i, 0)),
        out_shape=jax.ShapeDtypeStruct((rows, d), _BF16),
        compiler_params=_params(("arbitrary",)),
        name="rmsnorm",
    )(h, valid, gamma.reshape(1, d))


def _conv_matmul_kernel(*refs, tm, nw, n_onto, epilogue):
    x_ref = refs[0]
    w_refs = refs[1:1 + nw]
    c_refs = refs[1 + nw:1 + 2 * nw]
    out_refs = refs[1 + 2 * nw + n_onto:-2]
    pa_ref, pb_ref = refs[-2:]
    s = pl.program_id(0)

    @pl.when(s == 0)
    def _():
        pb_ref[...] = jnp.zeros_like(pb_ref)

    def step(prev_ref, cur_ref):
        tn = cur_ref.shape[2]
        pieces = [(k, c0) for k in range(nw) for c0 in range(0, tn, MXU_COLS)]
        chunks = list(range(0, tm, EPILOGUE_ROWS))
        per_piece = -(-len(chunks) // len(pieces))
        cs = [c_refs[k][...] for k in range(nw)]
        x = pltpu.bitcast(x_ref[...], _BF16)
        for n, (k, c0) in enumerate(pieces):
            cols = slice(c0, min(c0 + MXU_COLS, tn))
            cur_ref[k, :, cols] = jnp.dot(x, w_refs[k][:, cols], preferred_element_type=_F32)
            for r0 in chunks[n * per_piece:(n + 1) * per_piece]:
                epilogue([_conv3(prev_ref, j, cs[j], r0, EPILOGUE_ROWS) for j in range(nw)], out_refs,
                         slice(r0, r0 + EPILOGUE_ROWS))

    @pl.when(s % 2 == 0)
    def _():
        step(pb_ref, pa_ref)

    @pl.when(s % 2 == 1)
    def _():
        step(pa_ref, pb_ref)


def _conv_matmul(xp, ws, cs, group_tiles, tn, nj, tile0, out_cols, epilogue, out_dtypes, name, onto=()):
    ni, packed_rows, d = xp.shape
    tmh = 2 * packed_rows
    tm = tmh - 2 * HALO
    nw = len(ws)
    steps = ni * nj
    cur = lambda s: jnp.minimum(s, steps - 1)
    lag = lambda s: jnp.maximum(s - 1, 0)
    wspec = lambda k: pl.BlockSpec((d, tn), lambda s: (0, cur(s) // ni + tile0 + group_tiles[k]))
    cspec = lambda k: pl.BlockSpec((3, tn), lambda s: (0, lag(s) // ni + tile0 + group_tiles[k]))
    out = pl.BlockSpec((tm, tn), lambda s: (lag(s) % ni, lag(s) // ni + tile0))
    return pl.pallas_call(
        functools.partial(_conv_matmul_kernel, tm=tm, nw=nw, n_onto=len(onto), epilogue=epilogue),
        grid=(steps + 1,),
        in_specs=[pl.BlockSpec((None, packed_rows, d), lambda s: (cur(s) % ni, 0, 0))]
        + [wspec(k) for k in range(nw)] + [cspec(k) for k in range(nw)]
        + [pl.BlockSpec(memory_space=pl.ANY)] * len(onto),
        out_specs=[out] * len(out_dtypes),
        out_shape=[jax.ShapeDtypeStruct((ni * tm, out_cols), dt) for dt in out_dtypes],
        input_output_aliases={1 + 2 * nw + k: k for k in range(len(onto))},
        scratch_shapes=[pltpu.VMEM((nw, tmh, tn), _F32), pltpu.VMEM((nw, tmh, tn), _F32)],
        compiler_params=_params(("arbitrary",)),
        name=name,
    )(xp, *ws, *cs, *onto)


def _qkv_kernel(x_ref, w_ref, o_ref, *, n_q_tiles, scale):
    acc = jnp.dot(x_ref[...], w_ref[...], preferred_element_type=_F32)
    s = jnp.where(pl.program_id(0) < n_q_tiles, scale, 1.0).astype(_F32)
    o_ref[...] = (acc * s).astype(o_ref.dtype)


def _qkv(hn, w_qkv, tm):
    rows, d = hn.shape
    n = w_qkv.shape[1]
    tn = _pick(d, 1024, LANES)
    return pl.pallas_call(
        functools.partial(_qkv_kernel, n_q_tiles=d // tn, scale=HEAD_DIM ** -0.5),
        grid=(n // tn, rows // tm),
        in_specs=[pl.BlockSpec((tm, d), lambda j, i: (i, 0)),
                  pl.BlockSpec((d, tn), lambda j, i: (0, j))],
        out_specs=pl.BlockSpec((tm, tn), lambda j, i: (i, j)),
        out_shape=jax.ShapeDtypeStruct((rows, n), _BF16),
        compiler_params=_params(("arbitrary", "arbitrary")),
        name="qkv_proj",
    )(hn, w_qkv)


def _residual_kernel(x_ref, w_ref, h_ref, o_ref):
    o_ref[...] = h_ref[...] + jnp.dot(x_ref[...], w_ref[...], preferred_element_type=_F32)


def _residual_matmul(x, w, h, tm, name):
    rows, k = x.shape
    d = w.shape[1]
    weights_resident = k <= RESIDENT_WEIGHT_K
    tn = _pick(d, 1024 if weights_resident else 256, LANES)
    grid = (d // tn, rows // tm) if weights_resident else (rows // tm, d // tn)
    ij = (lambda a, b: (b, a)) if weights_resident else (lambda a, b: (a, b))
    return pl.pallas_call(
        _residual_kernel,
        grid=grid,
        in_specs=[pl.BlockSpec((tm, k), lambda a, b: (ij(a, b)[0], 0)),
                  pl.BlockSpec((k, tn), lambda a, b: (0, ij(a, b)[1])),
                  pl.BlockSpec((tm, tn), lambda a, b: ij(a, b))],
        out_specs=pl.BlockSpec((tm, tn), lambda a, b: ij(a, b)),
        out_shape=jax.ShapeDtypeStruct((rows, d), _F32),
        input_output_aliases={2: 0},
        compiler_params=_params(("arbitrary", "arbitrary")),
        name=name,
    )(x, w, h)


class _Dft(NamedTuple):
    fwd: jax.Array
    inv: jax.Array
    twiddle: jax.Array


def _dft_matrices(seq):
    assert seq % FFT_RADIX == 0
    n1, n1z = _fft_sizes(seq)
    n = FFT_RADIX * n1
    k1 = jnp.arange(n1, dtype=jnp.int32)[:, None]
    m1 = jnp.arange(n1z, dtype=jnp.int32)[None, :]
    phase = ((m1 * k1) % n1).astype(_F32) * (2.0 * math.pi / n1)
    f = jnp.concatenate([jnp.cos(phase), -jnp.sin(phase)], axis=0)
    f = jnp.where(m1 * FFT_RADIX < seq, f, 0.0)
    p = jnp.arange(FFT_RADIX, dtype=jnp.int32)[:, None]
    tphase = (p * k1[:, 0][None, :]).astype(_F32) * (2.0 * math.pi / n)
    tw = jnp.stack([jnp.cos(tphase), -jnp.sin(tphase)], axis=1)
    tw = jnp.broadcast_to(tw[..., None], tw.shape + (TWIDDLE_LANES,))
    return _Dft(f.astype(_BF16), (f.T * (1.0 / n)).astype(_BF16), tw)


def _plane_rows(refs, plane, n1z):
    rows = pl.ds(plane, n1z, stride=FFT_RADIX)
    return jnp.concatenate([ref[rows, :] for ref in refs], axis=1)


def _planes_per_step(n1, dc):
    plane_block = 2 * FFT_RADIX * n1 * dc * 4
    return FFT_RADIX if plane_block <= DFT_BLOCK_BYTES else 1


def _lane_windows(n_rows, dc, row0, row_stride):
    sub = lambda l: _window(n_rows, LANES, lambda i, c, p: (row0 + i * row_stride, c * dc + l * LANES))
    return [sub(l) for l in range(dc // LANES)]


def _paired_dft(f, xa_refs, xb_refs, plane, n1z):
    xa, xb = _plane_rows(xa_refs, plane, n1z), _plane_rows(xb_refs, plane, n1z)
    dc, n1 = xa.shape[1], f.shape[0] // 2
    a = jnp.dot(f, jnp.concatenate([xa, xb], axis=1).astype(_BF16), preferred_element_type=_F32)
    return a[:n1, :dc] - a[n1:, dc:], a[n1:, :dc] + a[:n1, dc:]


def _dft_fwd_kernel(*refs, pp, pair):
    x_refs, f_ref, t_ref, o_ref = refs[:-3], refs[-3], refs[-2], refs[-1]
    n1z, n1 = f_ref.shape[1], o_ref.shape[2]
    half = len(x_refs) // 2
    for q in range(pp):
        plane = pl.program_id(2) * pp + q
        if pair:
            ar, ai = _paired_dft(f_ref[...], x_refs[:half], x_refs[half:], plane, n1z)
        else:
            a = jnp.dot(f_ref[...], _plane_rows(x_refs, plane, n1z).astype(_BF16), preferred_element_type=_F32)
            ar, ai = a[:n1], a[n1:]
        tr, ti = t_ref[q, 0, :, 0:1], t_ref[q, 1, :, 0:1]
        o_ref[0, q] = ar * tr - ai * ti
        o_ref[1, q] = ar * ti + ai * tr


def _dft_fwd(x, dft, n_sig, row0, row_stride, pair=False, lanes=256):
    d = x.shape[1]
    r = FFT_RADIX
    n1, n1z = dft.fwd.shape[0] // 2, dft.fwd.shape[1]
    dc = _pick(d, lanes, LANES)
    pp = _planes_per_step(n1, dc)
    if pair:
        assert n_sig % 2 == 0
        n_sig //= 2
        wins = (_lane_windows(n1z * r, dc, row0, 2 * row_stride)
                + _lane_windows(n1z * r, dc, row0 + row_stride, 2 * row_stride))
    else:
        wins = _lane_windows(n1z * r, dc, row0, row_stride)
    return pl.pallas_call(
        functools.partial(_dft_fwd_kernel, pp=pp, pair=pair),
        grid=(n_sig, d // dc, r // pp),
        in_specs=wins + [pl.BlockSpec((2 * n1, n1z), lambda i, c, p: (0, 0)),
                         pl.BlockSpec((pp, 2, n1, TWIDDLE_LANES), lambda i, c, p: (p, 0, 0, 0))],
        out_specs=pl.BlockSpec((2, pp, None, n1, dc), lambda i, c, p: (0, p, i, 0, c)),
        out_shape=jax.ShapeDtypeStruct((2, r, n_sig, n1, d), _F32),
        compiler_params=_params(("arbitrary", "arbitrary", "arbitrary")),
        name="dft_fwd",
    )(*([x] * len(wins)), dft.fwd, dft.twiddle)


def _cmul_const(x, w):
    re, im = x
    wr, wi = float(w.real), float(w.imag)
    if abs(wi) < 1e-12:
        return (re, im) if wr > 0 else (-re, -im)
    if abs(wr) < 1e-12:
        return (-im, re) if wi > 0 else (im, -re)
    return (re * wr - im * wi, re * wi + im * wr)


def _fft_planes(xs, sign):
    n = len(xs)
    if n == 1:
        return xs
    ev = _fft_planes(xs[0::2], sign)
    od = _fft_planes(xs[1::2], sign)
    out = [None] * n
    for k in range(n // 2):
        w = complex(math.cos(2.0 * math.pi * k / n), sign * math.sin(2.0 * math.pi * k / n))
        tr, ti = _cmul_const(od[k], w)
        out[k] = (ev[k][0] + tr, ev[k][1] + ti)
        out[k + n // 2] = (ev[k][0] - tr, ev[k][1] - ti)
    return out


def _tile_loop(tk, dc, body):
    def step(r, carry):
        rows = pl.ds(pl.multiple_of(r * 8, 8), 8)
        for l in range(dc // LANES):
            body(rows, slice(l * LANES, (l + 1) * LANES))
        return carry
    lax.fori_loop(0, tk // 8, step, 0)


def _filter_spectrum_kernel(a_ref, s_ref, o_ref):
    r, tk, dc = o_ref.shape[1], o_ref.shape[2], o_ref.shape[3]

    def body(rows, lanes):
        inv = 1.0 / s_ref[:, lanes]
        xf = _fft_planes([(a_ref[0, n, 0, rows, lanes], a_ref[1, n, 0, rows, lanes]) for n in range(r)], -1)
        xb = _fft_planes([(a_ref[0, n, 1, rows, lanes], a_ref[1, n, 1, rows, lanes]) for n in range(r)], -1)
        for k in range(r):
            o_ref[0, k, rows, lanes] = (xf[k][0] + xb[k][0]) * inv
            o_ref[1, k, rows, lanes] = (xf[k][1] - xb[k][1]) * inv

    _tile_loop(tk, dc, body)


def _plane_tile(n1):
    return _pick(n1, 352, 8)


def _filter_spectrum(planes, norm):
    _, r, _, n1, d = planes.shape
    tk, dc = _plane_tile(n1), LANES
    return pl.pallas_call(
        _filter_spectrum_kernel,
        grid=(d // dc, n1 // tk),
        in_specs=[pl.BlockSpec((2, r, 2, tk, dc), lambda c, k: (0, 0, 0, k, c)),
                  pl.BlockSpec((1, dc), lambda c, k: (0, c))],
        out_specs=pl.BlockSpec((2, r, tk, dc), lambda c, k: (0, 0, k, c)),
        out_shape=jax.ShapeDtypeStruct((2, r, n1, d), _F32),
        compiler_params=_params(("arbitrary", "arbitrary")),
        name="filter_spectrum",
    )(planes, norm)


def _spectral_product_kernel(a_ref, k_ref, o_ref):
    r, tk, dc = o_ref.shape[1], o_ref.shape[2], o_ref.shape[3]

    def body(rows, lanes):
        x = _fft_planes([(a_ref[0, n, rows, lanes], a_ref[1, n, rows, lanes]) for n in range(r)], -1)
        y = []
        for k in range(r):
            kr, ki = k_ref[0, k, rows, lanes], k_ref[1, k, rows, lanes]
            y.append((x[k][0] * kr - x[k][1] * ki, x[k][0] * ki + x[k][1] * kr))
        z = _fft_planes(y, +1)
        for n in range(r):
            o_ref[0, n, rows, lanes] = z[n][0]
            o_ref[1, n, rows, lanes] = z[n][1]

    _tile_loop(tk, dc, body)


def _spectral_product(planes, kspec):
    _, r, b, n1, d = planes.shape
    tk, dc = _plane_tile(n1), LANES
    spec = pl.BlockSpec((2, r, None, tk, dc), lambda c, k, i: (0, 0, i, k, c))
    return pl.pallas_call(
        _spectral_product_kernel,
        grid=(d // dc, n1 // tk, b),
        in_specs=[spec, pl.BlockSpec((2, r, tk, dc), lambda c, k, i: (0, 0, k, c))],
        out_specs=spec,
        out_shape=jax.ShapeDtypeStruct(planes.shape, _F32),
        compiler_params=_params(("arbitrary", "arbitrary", "arbitrary")),
        name="spectral_product",
    )(planes, kspec)


def _untwiddle(br, bi, tr, ti):
    return br * tr + bi * ti, bi * tr - br * ti


def _dft_inv_kernel(*refs, pp, nsub, pair):
    b_ref, h_ref, t_ref, skip_ref, o_ref = refs[0], refs[1], refs[2], refs[-2], refs[-1]
    sigs = refs[3:-2]
    n1z = h_ref.shape[0]
    dc = skip_ref.shape[1]
    for q in range(pp):
        plane = pl.program_id(2) * pp + q
        cr, ci = _untwiddle(b_ref[0, q], b_ref[1, q], t_ref[q, 0, :, 0:1], t_ref[q, 1, :, 0:1])
        if pair:
            vxa, vxb, x0a, x0b = (sigs[k * nsub:(k + 1) * nsub] for k in range(4))
            bb = jnp.concatenate([jnp.concatenate([cr, ci], axis=0), jnp.concatenate([ci, -cr], axis=0)], axis=1)
            y = jnp.dot(h_ref[...], bb.astype(_BF16), preferred_element_type=_F32)
            o_ref[0, q] = (y[:, :dc] + _plane_rows(vxa, plane, n1z) * skip_ref[...]) * _plane_rows(x0a, plane, n1z)
            o_ref[1, q] = (y[:, dc:] + _plane_rows(vxb, plane, n1z) * skip_ref[...]) * _plane_rows(x0b, plane, n1z)
        else:
            vx_refs, x0_refs = sigs[:nsub], sigs[nsub:]
            bb = jnp.concatenate([cr, ci], axis=0).astype(_BF16)
            y = jnp.dot(h_ref[...], bb, preferred_element_type=_F32)
            o_ref[0, q] = (y + _plane_rows(vx_refs, plane, n1z) * skip_ref[...]) * _plane_rows(x0_refs, plane, n1z)


def _dft_inv(planes, dft, vx, x0, skip, batch, row0, row_stride, pair=False):
    _, r, n_sets, n1, d = planes.shape
    n1z = dft.inv.shape[0]
    per = 2 if pair else 1
    assert n_sets * per == batch
    dc = _pick(d, 256 // per, LANES)
    pp = _planes_per_step(n1, dc)
    wins = [w for k in range(per) for w in _lane_windows(n1z * r, dc, row0 + k * row_stride, per * row_stride)]
    return pl.pallas_call(
        functools.partial(_dft_inv_kernel, pp=pp, nsub=len(wins) // per, pair=pair),
        grid=(n_sets, d // dc, r // pp),
        in_specs=[pl.BlockSpec((2, pp, None, n1, dc), lambda i, c, p: (0, p, i, 0, c)),
                  pl.BlockSpec((n1z, 2 * n1), lambda i, c, p: (0, 0)),
                  pl.BlockSpec((pp, 2, n1, TWIDDLE_LANES), lambda i, c, p: (p, 0, 0, 0))]
        + wins + wins + [pl.BlockSpec((1, dc), lambda i, c, p: (0, c))],
        out_specs=pl.BlockSpec((per, pp, n1z, dc), lambda i, c, p: (i, p, 0, c)),
        out_shape=jax.ShapeDtypeStruct((batch, r, n1z, d), _F32),
        compiler_params=_params(("arbitrary", "arbitrary", "arbitrary")),
        name="dft_inv",
    )(planes, dft.inv, dft.twiddle, *([vx] * len(wins)), *([x0] * len(wins)), skip)


def _fused_conv_kernel(*refs, nsub):
    vxa, vxb, x0a, x0b = (refs[k * nsub:(k + 1) * nsub] for k in range(4))
    f_ref, h_ref, t_ref, k_ref, skip_ref, o_ref, a_ref = refs[4 * nsub:]
    r, n1, dc = a_ref.shape[1], a_ref.shape[2], a_ref.shape[3]
    n1z = f_ref.shape[1]
    for p in range(r):
        ar, ai = _paired_dft(f_ref[...], vxa, vxb, p, n1z)
        tr, ti = t_ref[p, 0, :, 0:1], t_ref[p, 1, :, 0:1]
        a_ref[0, p] = ar * tr - ai * ti
        a_ref[1, p] = ar * ti + ai * tr

    def body(rows, lanes):
        x = _fft_planes([(a_ref[0, n, rows, lanes], a_ref[1, n, rows, lanes]) for n in range(r)], -1)
        y = []
        for k in range(r):
            kr, ki = k_ref[0, k, rows, lanes], k_ref[1, k, rows, lanes]
            y.append((x[k][0] * kr - x[k][1] * ki, x[k][0] * ki + x[k][1] * kr))
        z = _fft_planes(y, +1)
        for n in range(r):
            a_ref[0, n, rows, lanes] = z[n][0]
            a_ref[1, n, rows, lanes] = z[n][1]

    _tile_loop(n1, dc, body)
    for p in range(r):
        cr, ci = _untwiddle(a_ref[0, p], a_ref[1, p], t_ref[p, 0, :, 0:1], t_ref[p, 1, :, 0:1])
        bb = jnp.concatenate([jnp.concatenate([cr, ci], axis=0), jnp.concatenate([ci, -cr], axis=0)], axis=1)
        y = jnp.dot(h_ref[...], bb.astype(_BF16), preferred_element_type=_F32)
        o_ref[0, p] = (y[:, :dc] + _plane_rows(vxa, p, n1z) * skip_ref[...]) * _plane_rows(x0a, p, n1z)
        o_ref[1, p] = (y[:, dc:] + _plane_rows(vxb, p, n1z) * skip_ref[...]) * _plane_rows(x0b, p, n1z)


def _fused_conv(vx, x0, kspec, dft, skip, batch, row0, row_stride):
    assert batch % 2 == 0
    d = vx.shape[1]
    r = FFT_RADIX
    n1, n1z = dft.fwd.shape[0] // 2, dft.fwd.shape[1]
    dc = LANES
    sig = lambda spec: pl.BlockSpec(spec.block_shape, lambda c, i: spec.index_map(i, c, 0))
    wa = [sig(w) for w in _lane_windows(n1z * r, dc, row0, 2 * row_stride)]
    wb = [sig(w) for w in _lane_windows(n1z * r, dc, row0 + row_stride, 2 * row_stride)]
    return pl.pallas_call(
        functools.partial(_fused_conv_kernel, nsub=len(wa)),
        grid=(d // dc, batch // 2),
        in_specs=wa + wb + wa + wb + [
            pl.BlockSpec((2 * n1, n1z), lambda c, i: (0, 0)),
            pl.BlockSpec((n1z, 2 * n1), lambda c, i: (0, 0)),
            pl.BlockSpec((r, 2, n1, TWIDDLE_LANES), lambda c, i: (0, 0, 0, 0)),
            pl.BlockSpec((2, r, n1, dc), lambda c, i: (0, 0, 0, c)),
            pl.BlockSpec((1, dc), lambda c, i: (0, c))],
        out_specs=pl.BlockSpec((2, r, n1z, dc), lambda c, i: (i, 0, 0, c)),
        out_shape=jax.ShapeDtypeStruct((batch, r, n1z, d), _F32),
        scratch_shapes=[pltpu.VMEM((2, r, n1, dc), _F32)],
        compiler_params=_params(("arbitrary", "arbitrary")),
        name="hyena_conv",
    )(*([vx] * (2 * len(wa))), *([x0] * (2 * len(wa))), dft.fwd, dft.inv, dft.twiddle, kspec, skip)


def _filter_mlp_kernel(z_ref, w1_ref, wi_ref, b_ref, f_ref, o_ref):
    hi = lax.Precision.HIGHEST
    h = jnp.sin(f_ref[0:1] * (jnp.dot(z_ref[...], w1_ref[...], precision=hi, preferred_element_type=_F32) + b_ref[0:1]))
    h = jnp.sin(f_ref[1:2] * (jnp.dot(h, wi_ref[0], precision=hi, preferred_element_type=_F32) + b_ref[1:2]))
    h = jnp.sin(f_ref[2:3] * (jnp.dot(h, wi_ref[1], precision=hi, preferred_element_type=_F32) + b_ref[2:3]))
    o_ref[...] = h


def _filter_taps_kernel(hm_ref, z_ref, wf_ref, wb_ref, d_ref, k_ref, s_ref):
    hi = lax.Precision.HIGHEST
    t = z_ref[:, 0:1]
    mf = z_ref[:, HY_EMB:HY_EMB + 1]
    mb = z_ref[:, HY_EMB + 1:HY_EMB + 2]
    hm = hm_ref[...]
    kf = jnp.dot(hm, wf_ref[...], precision=hi, preferred_element_type=_F32) * jnp.exp(-t * jnp.abs(d_ref[0:1])) * mf
    kb = jnp.dot(hm, wb_ref[...], precision=hi, preferred_element_type=_F32) * jnp.exp(-t * jnp.abs(d_ref[1:2])) * mb
    k_ref[0] = kf
    k_ref[1] = kb

    @pl.when(pl.program_id(1) == 0)
    def _():
        s_ref[...] = jnp.zeros_like(s_ref)
    s_ref[...] += jnp.sum(jnp.abs(kf) + jnp.abs(kb), axis=0, keepdims=True)


def _position_features(seq, rows):
    pos = jnp.arange(seq, dtype=_F32)[:, None]
    t = pos / max(seq - 1, 1)
    f = jnp.linspace(1e-4, HY_BANDS - 1, HY_BANDS, dtype=_F32)[None, :]
    ang = f * (2.0 * math.pi / seq) * pos
    fwd = jnp.ones((seq, 1), _F32)
    bwd = (pos >= 1).astype(_F32)
    z = jnp.concatenate([t, jnp.cos(ang), -jnp.sin(ang), fwd, bwd], axis=-1)
    return jnp.pad(z, ((0, rows - seq), (0, LANES - z.shape[1])))


def _hyena_filter(seq, w1, w_inner, b, freq, w_out, delta, dft):
    _, n1z = _fft_sizes(seq)
    lp = n1z * FFT_RADIX
    width = w1.shape[1]
    d = w_out.shape[1] // 2
    z = _position_features(seq, lp)
    w1p = jnp.pad(w1, ((0, LANES - w1.shape[0]), (0, 0)))
    tr = n1z
    hm = pl.pallas_call(
        _filter_mlp_kernel,
        grid=(lp // tr,),
        in_specs=[pl.BlockSpec((tr, LANES), lambda i: (i, 0)),
                  pl.BlockSpec((LANES, width), lambda i: (0, 0)),
                  pl.BlockSpec((2, width, width), lambda i: (0, 0, 0)),
                  pl.BlockSpec((3, width), lambda i: (0, 0)),
                  pl.BlockSpec((3, width), lambda i: (0, 0))],
        out_specs=pl.BlockSpec((tr, width), lambda i: (i, 0)),
        out_shape=jax.ShapeDtypeStruct((lp, width), _F32),
        compiler_params=_params(("arbitrary",)),
        name="filter_mlp",
    )(z, w1p, w_inner, b, freq)
    dc = _pick(d, 512, LANES)
    nc = d // dc
    taps, norm = pl.pallas_call(
        _filter_taps_kernel,
        grid=(nc, lp // tr),
        in_specs=[pl.BlockSpec((tr, width), lambda c, i: (i, 0)),
                  pl.BlockSpec((tr, LANES), lambda c, i: (i, 0)),
                  pl.BlockSpec((width, dc), lambda c, i: (0, c)),
                  pl.BlockSpec((width, dc), lambda c, i: (0, c + nc)),
                  pl.BlockSpec((2, dc), lambda c, i: (0, c))],
        out_specs=[pl.BlockSpec((2, tr, dc), lambda c, i: (0, i, c)), pl.BlockSpec((1, dc), lambda c, i: (0, c))],
        out_shape=[jax.ShapeDtypeStruct((2, lp, d), _F32), jax.ShapeDtypeStruct((1, d), _F32)],
        compiler_params=_params(("arbitrary", "arbitrary")),
        name="filter_taps",
    )(hm, z, w_out, w_out, delta)
    return _filter_spectrum(_dft_fwd(taps.reshape(2 * lp, d), dft, 2, 0, lp, lanes=512), norm)


def _to_flat(parts, trunks, rows, dtype):
    d = parts[0].shape[-1]
    out = []
    for part, t in zip(parts, trunks):
        out.append(jnp.pad(part.astype(dtype), ((0, 0), (SLOT_PAD, 0), (0, 0))).reshape(t.batch * t.slot, d))
    used = sum(t.batch * t.slot for t in trunks)
    out.append(jnp.zeros((rows - used, d), dtype))
    return jnp.concatenate(out, axis=0)


def _hyena_mixer(h, valid, gamma, trunks, rows, tm, w_in, w_conv, filt, delta, skip, w_out, dft):
    d = h.shape[1]
    tn = _pick(d, 512, LANES)
    nj = d // tn
    x0, vx = _conv_matmul(_halo_norm(h, valid, gamma, tm), [w_in] * 3, [w_conv] * 3, [0, nj, 2 * nj], tn, nj, 0, d,
                          _hyena_epilogue, (_F32, _F32), "hyena_in")
    parts = []
    for t in trunks:
        kspec = _hyena_filter(t.seq, *filt, delta, dft[t.seq])
        row0 = t.base + SLOT_PAD
        n1 = dft[t.seq].fwd.shape[0] // 2
        pair = t.batch % 2 == 0
        if pair and _planes_per_step(n1, _pick(d, 256, LANES)) == FFT_RADIX:
            y = _fused_conv(vx, x0, kspec, dft[t.seq], skip.reshape(1, -1), t.batch, row0, t.slot)
        else:
            planes = _spectral_product(_dft_fwd(vx, dft[t.seq], t.batch, row0, t.slot, pair), kspec)
            y = _dft_inv(planes, dft[t.seq], vx, x0, skip.reshape(1, -1), t.batch, row0, t.slot, pair)
        parts.append(jnp.swapaxes(y, 1, 2).reshape(t.batch, -1, d)[:, :t.seq])
    return _residual_matmul(_to_flat(parts, trunks, rows, _BF16), w_out, h, tm, "hyena_out")


def _attention_bias(rpb, rows_in_grid):
    n_blocks = rows_in_grid // Q_ROWS
    kr_win = min(WIN_ROWS, rows_in_grid)
    qr = np.arange(Q_ROWS)[:, None, None]
    kb = np.arange(3)[None, :, None]
    kr = np.arange(Q_ROWS)[None, None, :]
    dr_idx, row_ok = [], []
    for g in (0, 1, n_blocks - 1):
        r = Q_ROWS * g + qr
        key_row = Q_ROWS * (g + kb - 1) + kr
        rs = np.clip(r - kr_win // 2, 0, rows_in_grid - kr_win)
        ok = (key_row >= rs) & (key_row < rs + kr_win) & (g + kb - 1 >= 0) & (g + kb - 1 < n_blocks)
        dr_idx.append(np.clip(key_row - r + (WIN_ROWS - 1), 0, 2 * WIN_ROWS - 2))
        row_ok.append(ok)
    dr_idx = np.stack(dr_idx)
    row_ok = np.stack(row_ok)
    cols = np.arange(GRID_W)
    col_start = np.clip(cols - WIN_COLS // 2, 0, GRID_W - WIN_COLS)
    col_ok = (cols[None, :] >= col_start[:, None]) & (cols[None, :] < col_start[:, None] + WIN_COLS)
    col_idx = np.clip(cols[None, :] - cols[:, None] + WIN_COLS - 1, 0, 2 * WIN_COLS - 2)
    by_col = rpb[:, :, col_idx]
    by_row = by_col[:, dr_idx]
    ok = row_ok[:, :, :, :, None, None] & col_ok[None, None, None, None]
    bias = jnp.where(ok[None], by_row, MASK_VALUE)
    bias = jnp.transpose(bias, (1, 0, 2, 5, 3, 4, 6))
    return bias.reshape(3, rpb.shape[0], Q_BLOCK, 3 * Q_BLOCK).astype(_F32)


def _na_kernel(q_ref, k0_ref, k1_ref, k2_ref, v0_ref, v1_ref, v2_ref, km_ref, vm_ref, bias_ref, o_ref, *, heads):
    nt = (((1,), (1,)), ((), ()))
    for h in range(heads):
        hs = slice(h * HEAD_DIM, (h + 1) * HEAD_DIM)
        q = q_ref[:, hs]
        s = [lax.dot_general(q, k_ref[:, hs], nt, preferred_element_type=_F32)
             + bias_ref[h, :, i * Q_BLOCK:(i + 1) * Q_BLOCK]
             for i, k_ref in enumerate((k0_ref, k1_ref, k2_ref))]
        sm = lax.dot_general(q, km_ref[:, hs], nt, preferred_element_type=_F32)
        m = jnp.max(sm, axis=-1, keepdims=True)
        for si in s:
            m = jnp.maximum(m, jnp.max(si, axis=-1, keepdims=True))
        pm = jnp.exp(sm - m)
        den = jnp.sum(pm, axis=-1, keepdims=True)
        acc = jnp.dot(pm.astype(_BF16), vm_ref[:, hs], preferred_element_type=_F32)
        for si, v_ref in zip(s, (v0_ref, v1_ref, v2_ref)):
            p = jnp.exp(si - m)
            den = den + jnp.sum(p, axis=-1, keepdims=True)
            acc = acc + jnp.dot(p.astype(_BF16), v_ref[:, hs], preferred_element_type=_F32)
        o_ref[:, hs] = (acc / den).astype(o_ref.dtype)


def _na_meta_kernel(q_ref, k_ref, v_ref, o_ref, *, heads):
    nt = (((1,), (1,)), ((), ()))
    for h in range(heads):
        hs = slice(h * HEAD_DIM, (h + 1) * HEAD_DIM)
        s = lax.dot_general(q_ref[:, hs], k_ref[:, hs], nt, preferred_element_type=_F32)
        p = jnp.exp(s - jnp.max(s, axis=-1, keepdims=True))
        acc = jnp.dot(p.astype(_BF16), v_ref[:, hs], preferred_element_type=_F32)
        o_ref[:, hs] = (acc / jnp.sum(p, axis=-1, keepdims=True)).astype(o_ref.dtype)


def _neighbourhood_attention(qkv, trunk, bias):
    d = qkv.shape[1] // 3
    n_heads = d // HEAD_DIM
    hb = min(4, n_heads)
    wd = hb * HEAD_DIM
    nb = trunk.n_tok // Q_BLOCK
    b = trunk.batch
    tok0 = trunk.base + SLOT_PAD + N_META
    meta0 = (trunk.base + SLOT_PAD) // N_META
    slot_meta = trunk.slot // N_META

    def win(col0, shift):
        return _window(Q_BLOCK, wd, lambda h, i, g: (
            tok0 + i * trunk.slot + jnp.clip(g + shift, 0, nb - 1) * Q_BLOCK, col0 + h * wd))

    def meta(col0):
        return _window(N_META, wd, lambda h, i, g: (trunk.base + SLOT_PAD + i * trunk.slot, col0 + h * wd))

    variant = lambda g: jnp.where(g == 0, 0, jnp.where(g == nb - 1, 2, 1))
    by_row = lambda spec: pl.BlockSpec(spec.block_shape, lambda h, g, i: spec.index_map(h, i, g))
    out_grid = pl.pallas_call(
        functools.partial(_na_kernel, heads=hb),
        grid=(n_heads // hb, nb, b),
        in_specs=[by_row(s) for s in (win(0, 0), win(d, -1), win(d, 0), win(d, 1), win(2 * d, -1), win(2 * d, 0),
                                      win(2 * d, 1), meta(d), meta(2 * d))]
        + [pl.BlockSpec((None, hb, Q_BLOCK, 3 * Q_BLOCK), lambda h, g, i: (variant(g), h, 0, 0))],
        out_specs=pl.BlockSpec((Q_BLOCK, wd), lambda h, g, i: (i * nb + g, h)),
        out_shape=jax.ShapeDtypeStruct((b * trunk.n_tok, d), _BF16),
        compiler_params=_params(("arbitrary", "arbitrary", "arbitrary")),
        name="na_grid",
    )(qkv, qkv, qkv, qkv, qkv, qkv, qkv, qkv, qkv, bias)
    mspec = lambda k: pl.BlockSpec((N_META, d), lambda i: (meta0 + i * slot_meta, k))
    out_meta = pl.pallas_call(
        functools.partial(_na_meta_kernel, heads=n_heads),
        grid=(b,),
        in_specs=[mspec(0), mspec(1), mspec(2)],
        out_specs=pl.BlockSpec((N_META, d), lambda i: (i, 0)),
        out_shape=jax.ShapeDtypeStruct((b * N_META, d), _BF16),
        compiler_params=_params(("arbitrary",)),
        name="na_meta",
    )(qkv, qkv, qkv)
    return jnp.concatenate([out_meta.reshape(b, N_META, d), out_grid.reshape(b, trunk.n_tok, d)], axis=1)


def _na_mixer(h, valid, gamma, trunks, rows, tm, w_qkv, rpb, w_out):
    qkv = _qkv(_plain_norm(h, valid, gamma), w_qkv, tm)
    biases = {}
    parts = []
    for t in trunks:
        grid_rows = t.n_tok // GRID_W
        key = grid_rows if grid_rows < 3 * Q_ROWS else -1
        if key not in biases:
            biases[key] = _attention_bias(rpb, grid_rows)
        parts.append(_neighbourhood_attention(qkv, t, biases[key]))
    return _residual_matmul(_to_flat(parts, trunks, rows, _BF16), w_out, h, tm, "na_out")


def _valid_rows(trunks, rows):
    r = jnp.arange(rows, dtype=jnp.int32)[:, None]
    v = jnp.zeros((rows, 1), jnp.bool_)
    for t in trunks:
        inside = (r >= t.base) & (r < t.base + t.batch * t.slot)
        v = v | (inside & ((r - t.base) % t.slot >= SLOT_PAD))
    return jnp.broadcast_to(v.astype(_F32), (rows, LANES))


def _ffn_up(xp, w_up, w_conv):
    f = w_up.shape[1] // 2
    ws = [w_up[:, :f].astype(_BF16), w_up[:, f:].astype(_BF16)]
    cs = [w_conv[:, :f], w_conv[:, f:]]
    tn = min(_FFN_TN, f)
    nj = f // tn
    a = _conv_matmul(xp, ws, cs, [0, 0], tn, nj, 0, f, _ffn_epilogue, (_BF16,), "ffn_up")
    tail = f - nj * tn
    if tail:
        assert tail % LANES == 0 and (nj * tn) % tail == 0
        a = _conv_matmul(xp, ws, cs, [0, 0], tail, 1, nj * tn // tail, f, _ffn_epilogue, (_BF16,), "ffn_up_tail",
                         onto=a)
    return a[0]


def kernel(x_prompt, x_sample, meta_tokens, norm_mix, norm_ffn, norm_final, hy_w_in, hy_w_conv, hy_f_w1,
           hy_f_w_inner, hy_f_b, hy_f_freq, hy_f_w_out, hy_delta, hy_skip, hy_w_out, na_w_qkv, na_rpb,
           na_w_out, ffn_w_up, ffn_w_conv, ffn_w_down):
    d = x_prompt.shape[-1]
    depth = norm_mix.shape[0]
    xs = (x_prompt, x_sample)
    tm = _TM
    trunks, rows = _plan([(x.shape[0], x.shape[1]) for x in xs], tm)
    valid = _valid_rows(trunks, rows)
    dft = {t.seq: _dft_matrices(t.seq) for t in trunks}

    meta = meta_tokens.astype(_F32)
    seqs = [jnp.concatenate([jnp.broadcast_to(meta[None], (x.shape[0], N_META, d)), x], axis=1) for x in xs]
    h = _to_flat(seqs, trunks, rows, _F32)

    bf = lambda w: w.astype(_BF16)
    for i in range(depth):
        j = i // 2
        if i % 2 == 0:
            filt = (hy_f_w1[j], hy_f_w_inner[j], hy_f_b[j], hy_f_freq[j], hy_f_w_out[j])
            h = _hyena_mixer(h, valid, norm_mix[i], trunks, rows, tm, bf(hy_w_in[j]), hy_w_conv[j], filt,
                             hy_delta[j], hy_skip[j], bf(hy_w_out[j]), dft)
        else:
            h = _na_mixer(h, valid, norm_mix[i], trunks, rows, tm, bf(na_w_qkv[j]), na_rpb[j], bf(na_w_out[j]))
        a = _ffn_up(_halo_norm(h, valid, norm_ffn[i], tm), ffn_w_up[i], ffn_w_conv[i])
        h = _residual_matmul(a, bf(ffn_w_down[i]), h, tm, "ffn_down")
    return tuple(_final_norm(h, norm_final, t) for t in trunks)
```

```python
import functools
import math
from typing import NamedTuple

import numpy as np
import jax
import jax.numpy as jnp
from jax import lax
from jax.experimental import pallas as pl
from jax.experimental.pallas import tpu as pltpu

N_META = 16
GRID_W = 64
HEAD_DIM = 128
WIN_ROWS = 8
WIN_COLS = 16
HY_EMB = 33
HY_BANDS = (HY_EMB - 1) // 2
RMS_EPS = 1e-6

SLOT_PAD = 16
HALO = 8
EPILOGUE_ROWS = 32
MXU_COLS = 256
RESIDENT_WEIGHT_K = 4096
FFT_RADIX = 16
Q_ROWS = 4
Q_BLOCK = Q_ROWS * GRID_W
MASK_VALUE = -1e30
LANES = 128
V7X_VMEM_LIMIT = 56 * 1024 * 1024
DFT_BLOCK_BYTES = 10 * 1024 * 1024
TWIDDLE_LANES = 8

_TM = 640
_FFN_TN = 512
_F32 = jnp.float32
_BF16 = jnp.bfloat16


class _Trunk(NamedTuple):
    batch: int
    n_tok: int
    seq: int
    slot: int
    base: int


def _round_up(x, m):
    return (x + m - 1) // m * m


def _pick(n, target, mult):
    best = None
    for d in range(mult, min(n, target) + 1, mult):
        if n % d == 0:
            best = d
    return best if best is not None else n


def _fft_sizes(seq):
    n1 = _round_up(-(-(2 * seq - 1) // FFT_RADIX), 8)
    n1z = _round_up(-(-seq // FFT_RADIX), 8)
    return n1, n1z


def _plan(shapes, tm):
    trunks, base = [], 0
    for batch, n_tok in shapes:
        seq = N_META + n_tok
        slot = SLOT_PAD + seq
        trunks.append(_Trunk(batch, n_tok, seq, slot, base))
        base += batch * slot
    reach = max(t.base + (t.batch - 1) * t.slot + SLOT_PAD + FFT_RADIX * _fft_sizes(t.seq)[1] for t in trunks)
    rows = _round_up(max(base + SLOT_PAD, reach), tm)
    return trunks, rows


def _params(sem, vmem=V7X_VMEM_LIMIT):
    return pltpu.CompilerParams(dimension_semantics=sem, vmem_limit_bytes=vmem)


def _window(rows, cols, index):
    def aligned(*args):
        r, c = index(*args)
        hint = lambda x, m: x if isinstance(x, int) else pl.multiple_of(x, m)
        return hint(r, SLOT_PAD), hint(c, LANES)
    return pl.BlockSpec((pl.Element(rows), pl.Element(cols)), aligned)


def _norm_rows(h, g, valid):
    y = h * lax.rsqrt(jnp.mean(h * h, axis=-1, keepdims=True) + RMS_EPS)
    return jnp.where(valid[:, 0:1] > 0, y * g, 0.0)


def _final_norm_kernel(h_ref, g_ref, o_ref):
    x = h_ref[...]
    o_ref[...] = x * lax.rsqrt(jnp.mean(x * x, axis=-1, keepdims=True) + RMS_EPS) * g_ref[...]


def _final_norm(h, g, trunk):
    d = h.shape[1]
    tr = _pick(trunk.n_tok, 256, 16)
    nr = trunk.n_tok // tr
    tok0 = trunk.base + SLOT_PAD + N_META
    out = pl.pallas_call(
        _final_norm_kernel,
        grid=(trunk.batch, nr),
        in_specs=[_window(tr, d, lambda i, r: (tok0 + i * trunk.slot + r * tr, 0)),
                  pl.BlockSpec((1, d), lambda i, r: (0, 0))],
        out_specs=pl.BlockSpec((tr, d), lambda i, r: (i * nr + r, 0)),
        out_shape=jax.ShapeDtypeStruct((trunk.batch * trunk.n_tok, d), _F32),
        compiler_params=_params(("arbitrary", "arbitrary")),
        name="final_norm",
    )(h, g.reshape(1, d))
    return out.reshape(trunk.batch, trunk.n_tok, d)


def _conv3(p_ref, k, c, r0, nr):
    at = lambda shift: p_ref[k, HALO + shift + r0:HALO + shift + r0 + nr, :]
    return c[0:1] * at(-1) + c[1:2] * at(0) + c[2:3] * at(1)


def _ffn_epilogue(vals, outs, rows):
    g, v = vals
    outs[0][rows, :] = (g * (1.0 / (1.0 + jnp.exp(-g))) * v).astype(outs[0].dtype)


def _hyena_epilogue(vals, outs, rows):
    x0, x1, v = vals
    outs[0][rows, :] = x0
    outs[1][rows, :] = v * x1


def _halo_norm_kernel(hp_ref, hc_ref, hn_ref, vp_ref, vc_ref, vn_ref, g_ref, o_ref):
    g = g_ref[...]
    rows = jnp.concatenate([_norm_rows(hp_ref[...], g, vp_ref[...]), _norm_rows(hc_ref[...], g, vc_ref[...]),
                            _norm_rows(hn_ref[...], g, vn_ref[...])], axis=0)
    o_ref[...] = pltpu.bitcast(rows.astype(_BF16), jnp.uint32)


def _halo_norm(h, valid, gamma, tm):
    rows, d = h.shape
    per, last = tm // HALO, rows // HALO - 1

    def panel(width):
        return [pl.BlockSpec((HALO, width), lambda i: (jnp.maximum(i * per - 1, 0), 0)),
                pl.BlockSpec((tm, width), lambda i: (i, 0)),
                pl.BlockSpec((HALO, width), lambda i: (jnp.minimum((i + 1) * per, last), 0))]

    return pl.pallas_call(
        _halo_norm_kernel,
        grid=(rows // tm,),
        in_specs=panel(d) + panel(LANES) + [pl.BlockSpec((1, d), lambda i: (0, 0))],
        out_specs=pl.BlockSpec((None, tm // 2 + HALO, d), lambda i: (i, 0, 0)),
        out_shape=jax.ShapeDtypeStruct((rows // tm, tm // 2 + HALO, d), jnp.uint32),
        compiler_params=_params(("arbitrary",)),
        name="halo_norm",
    )(h, h, h, valid, valid, valid, gamma.reshape(1, d))


def _plain_norm_kernel(h_ref, v_ref, g_ref, o_ref):
    o_ref[...] = _norm_rows(h_ref[...], g_ref[...], v_ref[...]).astype(o_ref.dtype)


def _plain_norm(h, valid, gamma):
    rows, d = h.shape
    tr = _pick(rows, 256, 16)
    return pl.pallas_call(
        _plain_norm_kernel,
        grid=(rows // tr,),
        in_specs=[pl.BlockSpec((tr, d), lambda i: (i, 0)),
                  pl.BlockSpec((tr, LANES), lambda i: (i, 0)),
                  pl.BlockSpec((1, d), lambda i: (0, 0))],
        out_specs=pl.BlockSpec((tr, d), lambda i: (i, 0)),
        out_shape=jax.ShapeDtypeStruct((rows, d), _BF16),
        compiler_params=_params(("arbitrary",)),
        name="rmsnorm",
    )(h, valid, gamma.reshape(1, d))


def _conv_matmul_kernel(*refs, tm, nw, n_onto, epilogue):
    x_ref = refs[0]
    w_refs = refs[1:1 + nw]
    c_refs = refs[1 + nw:1 + 2 * nw]
    out_refs = refs[1 + 2 * nw + n_onto:-2]
    pa_ref, pb_ref = refs[-2:]
    s = pl.program_id(0)

    @pl.when(s == 0)
    def _():
        pb_ref[...] = jnp.zeros_like(pb_ref)

    def step(prev_ref, cur_ref):
        tn = cur_ref.shape[2]
        pieces = [(k, c0) for k in range(nw) for c0 in range(0, tn, MXU_COLS)]
        chunks = list(range(0, tm, EPILOGUE_ROWS))
        per_piece = -(-len(chunks) // len(pieces))
        cs = [c_refs[k][...] for k in range(nw)]
        x = pltpu.bitcast(x_ref[...], _BF16)
        for n, (k, c0) in enumerate(pieces):
            cols = slice(c0, min(c0 + MXU_COLS, tn))
            cur_ref[k, :, cols] = jnp.dot(x, w_refs[k][:, cols], preferred_element_type=_F32)
            for r0 in chunks[n * per_piece:(n + 1) * per_piece]:
                epilogue([_conv3(prev_ref, j, cs[j], r0, EPILOGUE_ROWS) for j in range(nw)], out_refs,
                         slice(r0, r0 + EPILOGUE_ROWS))

    @pl.when(s % 2 == 0)
    def _():
        step(pb_ref, pa_ref)

    @pl.when(s % 2 == 1)
    def _():
        step(pa_ref, pb_ref)


def _conv_matmul(xp, ws, cs, group_tiles, tn, nj, tile0, out_cols, epilogue, out_dtypes, name, onto=()):
    ni, packed_rows, d = xp.shape
    tmh = 2 * packed_rows
    tm = tmh - 2 * HALO
    nw = len(ws)
    steps = ni * nj
    cur = lambda s: jnp.minimum(s, steps - 1)
    lag = lambda s: jnp.maximum(s - 1, 0)
    wspec = lambda k: pl.BlockSpec((d, tn), lambda s: (0, cur(s) // ni + tile0 + group_tiles[k]))
    cspec = lambda k: pl.BlockSpec((3, tn), lambda s: (0, lag(s) // ni + tile0 + group_tiles[k]))
    out = pl.BlockSpec((tm, tn), lambda s: (lag(s) % ni, lag(s) // ni + tile0))
    return pl.pallas_call(
        functools.partial(_conv_matmul_kernel, tm=tm, nw=nw, n_onto=len(onto), epilogue=epilogue),
        grid=(steps + 1,),
        in_specs=[pl.BlockSpec((None, packed_rows, d), lambda s: (cur(s) % ni, 0, 0))]
        + [wspec(k) for k in range(nw)] + [cspec(k) for k in range(nw)]
        + [pl.BlockSpec(memory_space=pl.ANY)] * len(onto),
        out_specs=[out] * len(out_dtypes),
        out_shape=[jax.ShapeDtypeStruct((ni * tm, out_cols), dt) for dt in out_dtypes],
        input_output_aliases={1 + 2 * nw + k: k for k in range(len(onto))},
        scratch_shapes=[pltpu.VMEM((nw, tmh, tn), _F32), pltpu.VMEM((nw, tmh, tn), _F32)],
        compiler_params=_params(("arbitrary",)),
        name=name,
    )(xp, *ws, *cs, *onto)


def _qkv_kernel(x_ref, w_ref, o_ref, *, n_q_tiles, scale):
    acc = jnp.dot(x_ref[...], w_ref[...], preferred_element_type=_F32)
    s = jnp.where(pl.program_id(0) < n_q_tiles, scale, 1.0).astype(_F32)
    o_ref[...] = (acc * s).astype(o_ref.dtype)


def _qkv(hn, w_qkv, tm):
    rows, d = hn.shape
    n = w_qkv.shape[1]
    tn = _pick(d, 1024, LANES)
    return pl.pallas_call(
        functools.partial(_qkv_kernel, n_q_tiles=d // tn, scale=HEAD_DIM ** -0.5),
        grid=(n // tn, rows // tm),
        in_specs=[pl.BlockSpec((tm, d), lambda j, i: (i, 0)),
                  pl.BlockSpec((d, tn), lambda j, i: (0, j))],
        out_specs=pl.BlockSpec((tm, tn), lambda j, i: (i, j)),
        out_shape=jax.ShapeDtypeStruct((rows, n), _BF16),
        compiler_params=_params(("arbitrary", "arbitrary")),
        name="qkv_proj",
    )(hn, w_qkv)


def _residual_kernel(x_ref, w_ref, h_ref, o_ref):
    o_ref[...] = h_ref[...] + jnp.dot(x_ref[...], w_ref[...], preferred_element_type=_F32)


def _residual_matmul(x, w, h, tm, name):
    rows, k = x.shape
    d = w.shape[1]
    weights_resident = k <= RESIDENT_WEIGHT_K
    tn = _pick(d, 1024 if weights_resident else 256, LANES)
    grid = (d // tn, rows // tm) if weights_resident else (rows // tm, d // tn)
    ij = (lambda a, b: (b, a)) if weights_resident else (lambda a, b: (a, b))
    return pl.pallas_call(
        _residual_kernel,
        grid=grid,
        in_specs=[pl.BlockSpec((tm, k), lambda a, b: (ij(a, b)[0], 0)),
                  pl.BlockSpec((k, tn), lambda a, b: (0, ij(a, b)[1])),
                  pl.BlockSpec((tm, tn), lambda a, b: ij(a, b))],
        out_specs=pl.BlockSpec((tm, tn), lambda a, b: ij(a, b)),
        out_shape=jax.ShapeDtypeStruct((rows, d), _F32),
        input_output_aliases={2: 0},
        compiler_params=_params(("arbitrary", "arbitrary")),
        name=name,
    )(x, w, h)


class _Dft(NamedTuple):
    fwd: jax.Array
    inv: jax.Array
    twiddle: jax.Array


def _dft_matrices(seq):
    assert seq % FFT_RADIX == 0
    n1, n1z = _fft_sizes(seq)
    n = FFT_RADIX * n1
    k1 = jnp.arange(n1, dtype=jnp.int32)[:, None]
    m1 = jnp.arange(n1z, dtype=jnp.int32)[None, :]
    phase = ((m1 * k1) % n1).astype(_F32) * (2.0 * math.pi / n1)
    f = jnp.concatenate([jnp.cos(phase), -jnp.sin(phase)], axis=0)
    f = jnp.where(m1 * FFT_RADIX < seq, f, 0.0)
    p = jnp.arange(FFT_RADIX, dtype=jnp.int32)[:, None]
    tphase = (p * k1[:, 0][None, :]).astype(_F32) * (2.0 * math.pi / n)
    tw = jnp.stack([jnp.cos(tphase), -jnp.sin(tphase)], axis=1)
    tw = jnp.broadcast_to(tw[..., None], tw.shape + (TWIDDLE_LANES,))
    return _Dft(f.astype(_BF16), (f.T * (1.0 / n)).astype(_BF16), tw)


def _plane_rows(refs, plane, n1z):
    rows = pl.ds(plane, n1z, stride=FFT_RADIX)
    return jnp.concatenate([ref[rows, :] for ref in refs], axis=1)


def _planes_per_step(n1, dc):
    plane_block = 2 * FFT_RADIX * n1 * dc * 4
    return FFT_RADIX if plane_block <= DFT_BLOCK_BYTES else 1


def _lane_windows(n_rows, dc, row0, row_stride):
    sub = lambda l: _window(n_rows, LANES, lambda i, c, p: (row0 + i * row_stride, c * dc + l * LANES))
    return [sub(l) for l in range(dc // LANES)]


def _paired_dft(f, xa_refs, xb_refs, plane, n1z):
    xa, xb = _plane_rows(xa_refs, plane, n1z), _plane_rows(xb_refs, plane, n1z)
    dc, n1 = xa.shape[1], f.shape[0] // 2
    a = jnp.dot(f, jnp.concatenate([xa, xb], axis=1).astype(_BF16), preferred_element_type=_F32)
    return a[:n1, :dc] - a[n1:, dc:], a[n1:, :dc] + a[:n1, dc:]


def _dft_fwd_kernel(*refs, pp, pair):
    x_refs, f_ref, t_ref, o_ref = refs[:-3], refs[-3], refs[-2], refs[-1]
    n1z, n1 = f_ref.shape[1], o_ref.shape[2]
    half = len(x_refs) // 2
    for q in range(pp):
        plane = pl.program_id(2) * pp + q
        if pair:
            ar, ai = _paired_dft(f_ref[...], x_refs[:half], x_refs[half:], plane, n1z)
        else:
            a = jnp.dot(f_ref[...], _plane_rows(x_refs, plane, n1z).astype(_BF16), preferred_element_type=_F32)
            ar, ai = a[:n1], a[n1:]
        tr, ti = t_ref[q, 0, :, 0:1], t_ref[q, 1, :, 0:1]
        o_ref[0, q] = ar * tr - ai * ti
        o_ref[1, q] = ar * ti + ai * tr


def _dft_fwd(x, dft, n_sig, row0, row_stride, pair=False, lanes=256):
    d = x.shape[1]
    r = FFT_RADIX
    n1, n1z = dft.fwd.shape[0] // 2, dft.fwd.shape[1]
    dc = _pick(d, lanes, LANES)
    pp = _planes_per_step(n1, dc)
    if pair:
        assert n_sig % 2 == 0
        n_sig //= 2
        wins = (_lane_windows(n1z * r, dc, row0, 2 * row_stride)
                + _lane_windows(n1z * r, dc, row0 + row_stride, 2 * row_stride))
    else:
        wins = _lane_windows(n1z * r, dc, row0, row_stride)
    return pl.pallas_call(
        functools.partial(_dft_fwd_kernel, pp=pp, pair=pair),
        grid=(n_sig, d // dc, r // pp),
        in_specs=wins + [pl.BlockSpec((2 * n1, n1z), lambda i, c, p: (0, 0)),
                         pl.BlockSpec((pp, 2, n1, TWIDDLE_LANES), lambda i, c, p: (p, 0, 0, 0))],
        out_specs=pl.BlockSpec((2, pp, None, n1, dc), lambda i, c, p: (0, p, i, 0, c)),
        out_shape=jax.ShapeDtypeStruct((2, r, n_sig, n1, d), _F32),
        compiler_params=_params(("arbitrary", "arbitrary", "arbitrary")),
        name="dft_fwd",
    )(*([x] * len(wins)), dft.fwd, dft.twiddle)


def _cmul_const(x, w):
    re, im = x
    wr, wi = float(w.real), float(w.imag)
    if abs(wi) < 1e-12:
        return (re, im) if wr > 0 else (-re, -im)
    if abs(wr) < 1e-12:
        return (-im, re) if wi > 0 else (im, -re)
    return (re * wr - im * wi, re * wi + im * wr)


def _fft_planes(xs, sign):
    n = len(xs)
    if n == 1:
        return xs
    ev = _fft_planes(xs[0::2], sign)
    od = _fft_planes(xs[1::2], sign)
    out = [None] * n
    for k in range(n // 2):
        w = complex(math.cos(2.0 * math.pi * k / n), sign * math.sin(2.0 * math.pi * k / n))
        tr, ti = _cmul_const(od[k], w)
        out[k] = (ev[k][0] + tr, ev[k][1] + ti)
        out[k + n // 2] = (ev[k][0] - tr, ev[k][1] - ti)
    return out


def _tile_loop(tk, dc, body):
    def step(r, carry):
        rows = pl.ds(pl.multiple_of(r * 8, 8), 8)
        for l in range(dc // LANES):
            body(rows, slice(l * LANES, (l + 1) * LANES))
        return carry
    lax.fori_loop(0, tk // 8, step, 0)


def _filter_spectrum_kernel(a_ref, s_ref, o_ref):
    r, tk, dc = o_ref.shape[1], o_ref.shape[2], o_ref.shape[3]

    def body(rows, lanes):
        inv = 1.0 / s_ref[:, lanes]
        xf = _fft_planes([(a_ref[0, n, 0, rows, lanes], a_ref[1, n, 0, rows, lanes]) for n in range(r)], -1)
        xb = _fft_planes([(a_ref[0, n, 1, rows, lanes], a_ref[1, n, 1, rows, lanes]) for n in range(r)], -1)
        for k in range(r):
            o_ref[0, k, rows, lanes] = (xf[k][0] + xb[k][0]) * inv
            o_ref[1, k, rows, lanes] = (xf[k][1] - xb[k][1]) * inv

    _tile_loop(tk, dc, body)


def _plane_tile(n1):
    return _pick(n1, 352, 8)


def _filter_spectrum(planes, norm):
    _, r, _, n1, d = planes.shape
    tk, dc = _plane_tile(n1), LANES
    return pl.pallas_call(
        _filter_spectrum_kernel,
        grid=(d // dc, n1 // tk),
        in_specs=[pl.BlockSpec((2, r, 2, tk, dc), lambda c, k: (0, 0, 0, k, c)),
                  pl.BlockSpec((1, dc), lambda c, k: (0, c))],
        out_specs=pl.BlockSpec((2, r, tk, dc), lambda c, k: (0, 0, k, c)),
        out_shape=jax.ShapeDtypeStruct((2, r, n1, d), _F32),
        compiler_params=_params(("arbitrary", "arbitrary")),
        name="filter_spectrum",
    )(planes, norm)


def _spectral_product_kernel(a_ref, k_ref, o_ref):
    r, tk, dc = o_ref.shape[1], o_ref.shape[2], o_ref.shape[3]

    def body(rows, lanes):
        x = _fft_planes([(a_ref[0, n, rows, lanes], a_ref[1, n, rows, lanes]) for n in range(r)], -1)
        y = []
        for k in range(r):
            kr, ki = k_ref[0, k, rows, lanes], k_ref[1, k, rows, lanes]
            y.append((x[k][0] * kr - x[k][1] * ki, x[k][0] * ki + x[k][1] * kr))
        z = _fft_planes(y, +1)
        for n in range(r):
            o_ref[0, n, rows, lanes] = z[n][0]
            o_ref[1, n, rows, lanes] = z[n][1]

    _tile_loop(tk, dc, body)


def _spectral_product(planes, kspec):
    _, r, b, n1, d = planes.shape
    tk, dc = _plane_tile(n1), LANES
    spec = pl.BlockSpec((2, r, None, tk, dc), lambda c, k, i: (0, 0, i, k, c))
    return pl.pallas_call(
        _spectral_product_kernel,
        grid=(d // dc, n1 // tk, b),
        in_specs=[spec, pl.BlockSpec((2, r, tk, dc), lambda c, k, i: (0, 0, k, c))],
        out_specs=spec,
        out_shape=jax.ShapeDtypeStruct(planes.shape, _F32),
        compiler_params=_params(("arbitrary", "arbitrary", "arbitrary")),
        name="spectral_product",
    )(planes, kspec)


def _untwiddle(br, bi, tr, ti):
    return br * tr + bi * ti, bi * tr - br * ti


def _dft_inv_kernel(*refs, pp, nsub, pair):
    b_ref, h_ref, t_ref, skip_ref, o_ref = refs[0], refs[1], refs[2], refs[-2], refs[-1]
    sigs = refs[3:-2]
    n1z = h_ref.shape[0]
    dc = skip_ref.shape[1]
    for q in range(pp):
        plane = pl.program_id(2) * pp + q
        cr, ci = _untwiddle(b_ref[0, q], b_ref[1, q], t_ref[q, 0, :, 0:1], t_ref[q, 1, :, 0:1])
        if pair:
            vxa, vxb, x0a, x0b = (sigs[k * nsub:(k + 1) * nsub] for k in range(4))
            bb = jnp.concatenate([jnp.concatenate([cr, ci], axis=0), jnp.concatenate([ci, -cr], axis=0)], axis=1)
            y = jnp.dot(h_ref[...], bb.astype(_BF16), preferred_element_type=_F32)
            o_ref[0, q] = (y[:, :dc] + _plane_rows(vxa, plane, n1z) * skip_ref[...]) * _plane_rows(x0a, plane, n1z)
            o_ref[1, q] = (y[:, dc:] + _plane_rows(vxb, plane, n1z) * skip_ref[...]) * _plane_rows(x0b, plane, n1z)
        else:
            vx_refs, x0_refs = sigs[:nsub], sigs[nsub:]
            bb = jnp.concatenate([cr, ci], axis=0).astype(_BF16)
            y = jnp.dot(h_ref[...], bb, preferred_element_type=_F32)
            o_ref[0, q] = (y + _plane_rows(vx_refs, plane, n1z) * skip_ref[...]) * _plane_rows(x0_refs, plane, n1z)


def _dft_inv(planes, dft, vx, x0, skip, batch, row0, row_stride, pair=False):
    _, r, n_sets, n1, d = planes.shape
    n1z = dft.inv.shape[0]
    per = 2 if pair else 1
    assert n_sets * per == batch
    dc = _pick(d, 256 // per, LANES)
    pp = _planes_per_step(n1, dc)
    wins = [w for k in range(per) for w in _lane_windows(n1z * r, dc, row0 + k * row_stride, per * row_stride)]
    return pl.pallas_call(
        functools.partial(_dft_inv_kernel, pp=pp, nsub=len(wins) // per, pair=pair),
        grid=(n_sets, d // dc, r // pp),
        in_specs=[pl.BlockSpec((2, pp, None, n1, dc), lambda i, c, p: (0, p, i, 0, c)),
                  pl.BlockSpec((n1z, 2 * n1), lambda i, c, p: (0, 0)),
                  pl.BlockSpec((pp, 2, n1, TWIDDLE_LANES), lambda i, c, p: (p, 0, 0, 0))]
        + wins + wins + [pl.BlockSpec((1, dc), lambda i, c, p: (0, c))],
        out_specs=pl.BlockSpec((per, pp, n1z, dc), lambda i, c, p: (i, p, 0, c)),
        out_shape=jax.ShapeDtypeStruct((batch, r, n1z, d), _F32),
        compiler_params=_params(("arbitrary", "arbitrary", "arbitrary")),
        name="dft_inv",
    )(planes, dft.inv, dft.twiddle, *([vx] * len(wins)), *([x0] * len(wins)), skip)


def _fused_conv_kernel(*refs, nsub):
    vxa, vxb, x0a, x0b = (refs[k * nsub:(k + 1) * nsub] for k in range(4))
    f_ref, h_ref, t_ref, k_ref, skip_ref, o_ref, a_ref = refs[4 * nsub:]
    r, n1, dc = a_ref.shape[1], a_ref.shape[2], a_ref.shape[3]
    n1z = f_ref.shape[1]
    for p in range(r):
        ar, ai = _paired_dft(f_ref[...], vxa, vxb, p, n1z)
        tr, ti = t_ref[p, 0, :, 0:1], t_ref[p, 1, :, 0:1]
        a_ref[0, p] = ar * tr - ai * ti
        a_ref[1, p] = ar * ti + ai * tr

    def body(rows, lanes):
        x = _fft_planes([(a_ref[0, n, rows, lanes], a_ref[1, n, rows, lanes]) for n in range(r)], -1)
        y = []
        for k in range(r):
            kr, ki = k_ref[0, k, rows, lanes], k_ref[1, k, rows, lanes]
            y.append((x[k][0] * kr - x[k][1] * ki, x[k][0] * ki + x[k][1] * kr))
        z = _fft_planes(y, +1)
        for n in range(r):
            a_ref[0, n, rows, lanes] = z[n][0]
            a_ref[1, n, rows, lanes] = z[n][1]

    _tile_loop(n1, dc, body)
    for p in range(r):
        cr, ci = _untwiddle(a_ref[0, p], a_ref[1, p], t_ref[p, 0, :, 0:1], t_ref[p, 1, :, 0:1])
        bb = jnp.concatenate([jnp.concatenate([cr, ci], axis=0), jnp.concatenate([ci, -cr], axis=0)], axis=1)
        y = jnp.dot(h_ref[...], bb.astype(_BF16), preferred_element_type=_F32)
        o_ref[0, p] = (y[:, :dc] + _plane_rows(vxa, p, n1z) * skip_ref[...]) * _plane_rows(x0a, p, n1z)
        o_ref[1, p] = (y[:, dc:] + _plane_rows(vxb, p, n1z) * skip_ref[...]) * _plane_rows(x0b, p, n1z)


def _fused_conv(vx, x0, kspec, dft, skip, batch, row0, row_stride):
    assert batch % 2 == 0
    d = vx.shape[1]
    r = FFT_RADIX
    n1, n1z = dft.fwd.shape[0] // 2, dft.fwd.shape[1]
    dc = LANES
    sig = lambda spec: pl.BlockSpec(spec.block_shape, lambda c, i: spec.index_map(i, c, 0))
    wa = [sig(w) for w in _lane_windows(n1z * r, dc, row0, 2 * row_stride)]
    wb = [sig(w) for w in _lane_windows(n1z * r, dc, row0 + row_stride, 2 * row_stride)]
    return pl.pallas_call(
        functools.partial(_fused_conv_kernel, nsub=len(wa)),
        grid=(d // dc, batch // 2),
        in_specs=wa + wb + wa + wb + [
            pl.BlockSpec((2 * n1, n1z), lambda c, i: (0, 0)),
            pl.BlockSpec((n1z, 2 * n1), lambda c, i: (0, 0)),
            pl.BlockSpec((r, 2, n1, TWIDDLE_LANES), lambda c, i: (0, 0, 0, 0)),
            pl.BlockSpec((2, r, n1, dc), lambda c, i: (0, 0, 0, c)),
            pl.BlockSpec((1, dc), lambda c, i: (0, c))],
        out_specs=pl.BlockSpec((2, r, n1z, dc), lambda c, i: (i, 0, 0, c)),
        out_shape=jax.ShapeDtypeStruct((batch, r, n1z, d), _F32),
        scratch_shapes=[pltpu.VMEM((2, r, n1, dc), _F32)],
        compiler_params=_params(("arbitrary", "arbitrary")),
        name="hyena_conv",
    )(*([vx] * (2 * len(wa))), *([x0] * (2 * len(wa))), dft.fwd, dft.inv, dft.twiddle, kspec, skip)


def _filter_mlp_kernel(z_ref, w1_ref, wi_ref, b_ref, f_ref, o_ref):
    hi = lax.Precision.HIGHEST
    h = jnp.sin(f_ref[0:1] * (jnp.dot(z_ref[...], w1_ref[...], precision=hi, preferred_element_type=_F32) + b_ref[0:1]))
    h = jnp.sin(f_ref[1:2] * (jnp.dot(h, wi_ref[0], precision=hi, preferred_element_type=_F32) + b_ref[1:2]))
    h = jnp.sin(f_ref[2:3] * (jnp.dot(h, wi_ref[1], precision=hi, preferred_element_type=_F32) + b_ref[2:3]))
    o_ref[...] = h


def _filter_taps_kernel(hm_ref, z_ref, wf_ref, wb_ref, d_ref, k_ref, s_ref):
    hi = lax.Precision.HIGHEST
    t = z_ref[:, 0:1]
    mf = z_ref[:, HY_EMB:HY_EMB + 1]
    mb = z_ref[:, HY_EMB + 1:HY_EMB + 2]
    hm = hm_ref[...]
    kf = jnp.dot(hm, wf_ref[...], precision=hi, preferred_element_type=_F32) * jnp.exp(-t * jnp.abs(d_ref[0:1])) * mf
    kb = jnp.dot(hm, wb_ref[...], precision=hi, preferred_element_type=_F32) * jnp.exp(-t * jnp.abs(d_ref[1:2])) * mb
    k_ref[0] = kf
    k_ref[1] = kb

    @pl.when(pl.program_id(1) == 0)
    def _():
        s_ref[...] = jnp.zeros_like(s_ref)
    s_ref[...] += jnp.sum(jnp.abs(kf) + jnp.abs(kb), axis=0, keepdims=True)


def _position_features(seq, rows):
    pos = jnp.arange(seq, dtype=_F32)[:, None]
    t = pos / max(seq - 1, 1)
    f = jnp.linspace(1e-4, HY_BANDS - 1, HY_BANDS, dtype=_F32)[None, :]
    ang = f * (2.0 * math.pi / seq) * pos
    fwd = jnp.ones((seq, 1), _F32)
    bwd = (pos >= 1).astype(_F32)
    z = jnp.concatenate([t, jnp.cos(ang), -jnp.sin(ang), fwd, bwd], axis=-1)
    return jnp.pad(z, ((0, rows - seq), (0, LANES - z.shape[1])))


def _hyena_filter(seq, w1, w_inner, b, freq, w_out, delta, dft):
    _, n1z = _fft_sizes(seq)
    lp = n1z * FFT_RADIX
    width = w1.shape[1]
    d = w_out.shape[1] // 2
    z = _position_features(seq, lp)
    w1p = jnp.pad(w1, ((0, LANES - w1.shape[0]), (0, 0)))
    tr = n1z
    hm = pl.pallas_call(
        _filter_mlp_kernel,
        grid=(lp // tr,),
        in_specs=[pl.BlockSpec((tr, LANES), lambda i: (i, 0)),
                  pl.BlockSpec((LANES, width), lambda i: (0, 0)),
                  pl.BlockSpec((2, width, width), lambda i: (0, 0, 0)),
                  pl.BlockSpec((3, width), lambda i: (0, 0)),
                  pl.BlockSpec((3, width), lambda i: (0, 0))],
        out_specs=pl.BlockSpec((tr, width), lambda i: (i, 0)),
        out_shape=jax.ShapeDtypeStruct((lp, width), _F32),
        compiler_params=_params(("arbitrary",)),
        name="filter_mlp",
    )(z, w1p, w_inner, b, freq)
    dc = _pick(d, 512, LANES)
    nc = d // dc
    taps, norm = pl.pallas_call(
        _filter_taps_kernel,
        grid=(nc, lp // tr),
        in_specs=[pl.BlockSpec((tr, width), lambda c, i: (i, 0)),
                  pl.BlockSpec((tr, LANES), lambda c, i: (i, 0)),
                  pl.BlockSpec((width, dc), lambda c, i: (0, c)),
                  pl.BlockSpec((width, dc), lambda c, i: (0, c + nc)),
                  pl.BlockSpec((2, dc), lambda c, i: (0, c))],
        out_specs=[pl.BlockSpec((2, tr, dc), lambda c, i: (0, i, c)), pl.BlockSpec((1, dc), lambda c, i: (0, c))],
        out_shape=[jax.ShapeDtypeStruct((2, lp, d), _F32), jax.ShapeDtypeStruct((1, d), _F32)],
        compiler_params=_params(("arbitrary", "arbitrary")),
        name="filter_taps",
    )(hm, z, w_out, w_out, delta)
    n1 = dft.fwd.shape[0] // 2
    lanes = 256 if _planes_per_step(n1, _pick(d, 256, LANES)) == FFT_RADIX else 512
    return _filter_spectrum(_dft_fwd(taps.reshape(2 * lp, d), dft, 2, 0, lp, lanes=lanes), norm)


def _to_flat(parts, trunks, rows, dtype):
    d = parts[0].shape[-1]
    out = []
    for part, t in zip(parts, trunks):
        out.append(jnp.pad(part.astype(dtype), ((0, 0), (SLOT_PAD, 0), (0, 0))).reshape(t.batch * t.slot, d))
    used = sum(t.batch * t.slot for t in trunks)
    out.append(jnp.zeros((rows - used, d), dtype))
    return jnp.concatenate(out, axis=0)


def _hyena_mixer(h, valid, gamma, trunks, rows, tm, w_in, w_conv, filt, delta, skip, w_out, dft):
    d = h.shape[1]
    tn = _pick(d, 512, LANES)
    nj = d // tn
    x0, vx = _conv_matmul(_halo_norm(h, valid, gamma, tm), [w_in] * 3, [w_conv] * 3, [0, nj, 2 * nj], tn, nj, 0, d,
                          _hyena_epilogue, (_F32, _F32), "hyena_in")
    parts = []
    for t in trunks:
        kspec = _hyena_filter(t.seq, *filt, delta, dft[t.seq])
        row0 = t.base + SLOT_PAD
        n1 = dft[t.seq].fwd.shape[0] // 2
        pair = t.batch % 2 == 0
        if pair and _planes_per_step(n1, _pick(d, 256, LANES)) == FFT_RADIX:
            y = _fused_conv(vx, x0, kspec, dft[t.seq], skip.reshape(1, -1), t.batch, row0, t.slot)
        else:
            planes = _spectral_product(_dft_fwd(vx, dft[t.seq], t.batch, row0, t.slot, pair), kspec)
            y = _dft_inv(planes, dft[t.seq], vx, x0, skip.reshape(1, -1), t.batch, row0, t.slot, pair)
        parts.append(jnp.swapaxes(y, 1, 2).reshape(t.batch, -1, d)[:, :t.seq])
    return _residual_matmul(_to_flat(parts, trunks, rows, _BF16), w_out, h, tm, "hyena_out")


def _attention_bias(rpb, rows_in_grid):
    n_blocks = rows_in_grid // Q_ROWS
    kr_win = min(WIN_ROWS, rows_in_grid)
    qr = np.arange(Q_ROWS)[:, None, None]
    kb = np.arange(3)[None, :, None]
    kr = np.arange(Q_ROWS)[None, None, :]
    dr_idx, row_ok = [], []
    for g in (0, 1, n_blocks - 1):
        r = Q_ROWS * g + qr
        key_row = Q_ROWS * (g + kb - 1) + kr
        rs = np.clip(r - kr_win // 2, 0, rows_in_grid - kr_win)
        ok = (key_row >= rs) & (key_row < rs + kr_win) & (g + kb - 1 >= 0) & (g + kb - 1 < n_blocks)
        dr_idx.append(np.clip(key_row - r + (WIN_ROWS - 1), 0, 2 * WIN_ROWS - 2))
        row_ok.append(ok)
    dr_idx = np.stack(dr_idx)
    row_ok = np.stack(row_ok)
    cols = np.arange(GRID_W)
    col_start = np.clip(cols - WIN_COLS // 2, 0, GRID_W - WIN_COLS)
    col_ok = (cols[None, :] >= col_start[:, None]) & (cols[None, :] < col_start[:, None] + WIN_COLS)
    col_idx = np.clip(cols[None, :] - cols[:, None] + WIN_COLS - 1, 0, 2 * WIN_COLS - 2)
    by_col = rpb[:, :, col_idx]
    by_row = by_col[:, dr_idx]
    ok = row_ok[:, :, :, :, None, None] & col_ok[None, None, None, None]
    bias = jnp.where(ok[None], by_row, MASK_VALUE)
    bias = jnp.transpose(bias, (1, 0, 2, 5, 3, 4, 6))
    return bias.reshape(3, rpb.shape[0], Q_BLOCK, 3 * Q_BLOCK).astype(_F32)


def _na_kernel(q_ref, k0_ref, k1_ref, k2_ref, v0_ref, v1_ref, v2_ref, km_ref, vm_ref, bias_ref, o_ref, *, heads):
    nt = (((1,), (1,)), ((), ()))
    for h in range(heads):
        hs = slice(h * HEAD_DIM, (h + 1) * HEAD_DIM)
        q = q_ref[:, hs]
        s = [lax.dot_general(q, k_ref[:, hs], nt, preferred_element_type=_F32)
             + bias_ref[h, :, i * Q_BLOCK:(i + 1) * Q_BLOCK]
             for i, k_ref in enumerate((k0_ref, k1_ref, k2_ref))]
        sm = lax.dot_general(q, km_ref[:, hs], nt, preferred_element_type=_F32)
        m = jnp.max(sm, axis=-1, keepdims=True)
        for si in s:
            m = jnp.maximum(m, jnp.max(si, axis=-1, keepdims=True))
        pm = jnp.exp(sm - m)
        den = jnp.sum(pm, axis=-1, keepdims=True)
        acc = jnp.dot(pm.astype(_BF16), vm_ref[:, hs], preferred_element_type=_F32)
        for si, v_ref in zip(s, (v0_ref, v1_ref, v2_ref)):
            p = jnp.exp(si - m)
            den = den + jnp.sum(p, axis=-1, keepdims=True)
            acc = acc + jnp.dot(p.astype(_BF16), v_ref[:, hs], preferred_element_type=_F32)
        o_ref[:, hs] = (acc / den).astype(o_ref.dtype)


def _na_meta_kernel(q_ref, k_ref, v_ref, o_ref, *, heads):
    nt = (((1,), (1,)), ((), ()))
    for h in range(heads):
        hs = slice(h * HEAD_DIM, (h + 1) * HEAD_DIM)
        s = lax.dot_general(q_ref[:, hs], k_ref[:, hs], nt, preferred_element_type=_F32)
        p = jnp.exp(s - jnp.max(s, axis=-1, keepdims=True))
        acc = jnp.dot(p.astype(_BF16), v_ref[:, hs], preferred_element_type=_F32)
        o_ref[:, hs] = (acc / jnp.sum(p, axis=-1, keepdims=True)).astype(o_ref.dtype)


def _neighbourhood_attention(qkv, trunk, bias, flat):
    d = qkv.shape[1] // 3
    n_heads = d // HEAD_DIM
    hb = min(4, n_heads)
    wd = hb * HEAD_DIM
    nb = trunk.n_tok // Q_BLOCK
    b = trunk.batch
    tok0 = trunk.base + SLOT_PAD + N_META
    meta0 = (trunk.base + SLOT_PAD) // N_META
    slot_meta = trunk.slot // N_META

    def win(col0, shift):
        return _window(Q_BLOCK, wd, lambda h, i, g: (
            tok0 + i * trunk.slot + jnp.clip(g + shift, 0, nb - 1) * Q_BLOCK, col0 + h * wd))

    def meta(col0):
        return _window(N_META, wd, lambda h, i, g: (trunk.base + SLOT_PAD + i * trunk.slot, col0 + h * wd))

    variant = lambda g: jnp.where(g == 0, 0, jnp.where(g == nb - 1, 2, 1))
    by_row = lambda spec: pl.BlockSpec(spec.block_shape, lambda h, g, i: spec.index_map(h, i, g))
    flat = pl.pallas_call(
        lambda *refs: _na_kernel(*refs[:10], refs[-1], heads=hb),
        grid=(n_heads // hb, nb, b),
        in_specs=[by_row(s) for s in (win(0, 0), win(d, -1), win(d, 0), win(d, 1), win(2 * d, -1), win(2 * d, 0),
                                      win(2 * d, 1), meta(d), meta(2 * d))]
        + [pl.BlockSpec((None, hb, Q_BLOCK, 3 * Q_BLOCK), lambda h, g, i: (variant(g), h, 0, 0)),
           pl.BlockSpec(memory_space=pl.ANY)],
        out_specs=by_row(_window(Q_BLOCK, wd, lambda h, i, g: (tok0 + i * trunk.slot + g * Q_BLOCK, h * wd))),
        out_shape=jax.ShapeDtypeStruct(flat.shape, flat.dtype),
        input_output_aliases={10: 0},
        compiler_params=_params(("arbitrary", "arbitrary", "arbitrary")),
        name="na_grid",
    )(qkv, qkv, qkv, qkv, qkv, qkv, qkv, qkv, qkv, bias, flat)
    mspec = lambda k: pl.BlockSpec((N_META, d), lambda i: (meta0 + i * slot_meta, k))
    return pl.pallas_call(
        lambda q_ref, k_ref, v_ref, flat_ref, o_ref: _na_meta_kernel(q_ref, k_ref, v_ref, o_ref, heads=n_heads),
        grid=(b,),
        in_specs=[mspec(0), mspec(1), mspec(2), pl.BlockSpec(memory_space=pl.ANY)],
        out_specs=mspec(0),
        out_shape=jax.ShapeDtypeStruct(flat.shape, flat.dtype),
        input_output_aliases={3: 0},
        compiler_params=_params(("arbitrary",)),
        name="na_meta",
    )(qkv, qkv, qkv, flat)


def _na_mixer(h, valid, gamma, trunks, rows, tm, w_qkv, rpb, w_out):
    qkv = _qkv(_plain_norm(h, valid, gamma), w_qkv, tm)
    biases = {}
    flat = jnp.zeros((rows, h.shape[1]), _BF16)
    for t in trunks:
        grid_rows = t.n_tok // GRID_W
        key = grid_rows if grid_rows < 3 * Q_ROWS else -1
        if key not in biases:
            biases[key] = _attention_bias(rpb, grid_rows)
        flat = _neighbourhood_attention(qkv, t, biases[key], flat)
    return _residual_matmul(flat, w_out, h, tm, "na_out")


def _valid_rows(trunks, rows):
    r = jnp.arange(rows, dtype=jnp.int32)[:, None]
    v = jnp.zeros((rows, 1), jnp.bool_)
    for t in trunks:
        inside = (r >= t.base) & (r < t.base + t.batch * t.slot)
        v = v | (inside & ((r - t.base) % t.slot >= SLOT_PAD))
    return jnp.broadcast_to(v.astype(_F32), (rows, LANES))


def _ffn_up(xp, w_up, w_conv):
    f = w_up.shape[1] // 2
    ws = [w_up[:, :f].astype(_BF16), w_up[:, f:].astype(_BF16)]
    cs = [w_conv[:, :f], w_conv[:, f:]]
    tn = min(_FFN_TN, f)
    nj = f // tn
    a = _conv_matmul(xp, ws, cs, [0, 0], tn, nj, 0, f, _ffn_epilogue, (_BF16,), "ffn_up")
    tail = f - nj * tn
    if tail:
        assert tail % LANES == 0 and (nj * tn) % tail == 0
        a = _conv_matmul(xp, ws, cs, [0, 0], tail, 1, nj * tn // tail, f, _ffn_epilogue, (_BF16,), "ffn_up_tail",
                         onto=a)
    return a[0]


def kernel(x_prompt, x_sample, meta_tokens, norm_mix, norm_ffn, norm_final, hy_w_in, hy_w_conv, hy_f_w1,
           hy_f_w_inner, hy_f_b, hy_f_freq, hy_f_w_out, hy_delta, hy_skip, hy_w_out, na_w_qkv, na_rpb,
           na_w_out, ffn_w_up, ffn_w_conv, ffn_w_down):
    d = x_prompt.shape[-1]
    depth = norm_mix.shape[0]
    xs = (x_prompt, x_sample)
    tm = _TM
    trunks, rows = _plan([(x.shape[0], x.shape[1]) for x in xs], tm)
    valid = _valid_rows(trunks, rows)
    dft = {t.seq: _dft_matrices(t.seq) for t in trunks}

    meta = meta_tokens.astype(_F32)
    seqs = [jnp.concatenate([jnp.broadcast_to(meta[None], (x.shape[0], N_META, d)), x], axis=1) for x in xs]
    h = _to_flat(seqs, trunks, rows, _F32)

    bf = lambda w: w.astype(_BF16)
    for i in range(depth):
        j = i // 2
        if i % 2 == 0:
            filt = (hy_f_w1[j], hy_f_w_inner[j], hy_f_b[j], hy_f_freq[j], hy_f_w_out[j])
            h = _hyena_mixer(h, valid, norm_mix[i], trunks, rows, tm, bf(hy_w_in[j]), hy_w_conv[j], filt,
                             hy_delta[j], hy_skip[j], bf(hy_w_out[j]), dft)
        else:
            h = _na_mixer(h, valid, norm_mix[i], trunks, rows, tm, bf(na_w_qkv[j]), na_rpb[j], bf(na_w_out[j]))
        a = _ffn_up(_halo_norm(h, valid, norm_ffn[i], tm), ffn_w_up[i], ffn_w_conv[i])
        h = _residual_matmul(a, bf(ffn_w_down[i]), h, tm, "ffn_down")
    return tuple(_final_norm(h, norm_final, t) for t in trunks)
```

```python
import functools
import math
from typing import NamedTuple

import numpy as np
import jax
import jax.numpy as jnp
from jax import lax
from jax.experimental import pallas as pl
from jax.experimental.pallas import tpu as pltpu

N_META = 16
GRID_W = 64
HEAD_DIM = 128
WIN_ROWS = 8
WIN_COLS = 16
HY_EMB = 33
HY_BANDS = (HY_EMB - 1) // 2
RMS_EPS = 1e-6

SLOT_PAD = 16
HALO = 8
EPILOGUE_ROWS = 32
MXU_COLS = 256
RESIDENT_WEIGHT_K = 4096
FFT_RADIX = 16
Q_ROWS = 4
Q_BLOCK = Q_ROWS * GRID_W
MASK_VALUE = -1e30
LANES = 128
V7X_VMEM_LIMIT = 56 * 1024 * 1024
DFT_BLOCK_BYTES = 10 * 1024 * 1024
TWIDDLE_LANES = 8

_TM = 832
_HYENA_TM = 640
_FFN_TN = 512
_F32 = jnp.float32
_BF16 = jnp.bfloat16


class _Trunk(NamedTuple):
    batch: int
    n_tok: int
    seq: int
    slot: int
    base: int


def _round_up(x, m):
    return (x + m - 1) // m * m


def _pick(n, target, mult):
    best = None
    for d in range(mult, min(n, target) + 1, mult):
        if n % d == 0:
            best = d
    return best if best is not None else n


def _fft_sizes(seq):
    n1 = _round_up(-(-(2 * seq - 1) // FFT_RADIX), 8)
    n1z = _round_up(-(-seq // FFT_RADIX), 8)
    return n1, n1z


def _plan(shapes, tm):
    trunks, base = [], 0
    for batch, n_tok in shapes:
        seq = N_META + n_tok
        slot = SLOT_PAD + seq
        trunks.append(_Trunk(batch, n_tok, seq, slot, base))
        base += batch * slot
    reach = max(t.base + (t.batch - 1) * t.slot + SLOT_PAD + FFT_RADIX * _fft_sizes(t.seq)[1] for t in trunks)
    rows = _round_up(max(base + SLOT_PAD, reach), tm)
    return trunks, rows


def _params(sem, vmem=V7X_VMEM_LIMIT):
    return pltpu.CompilerParams(dimension_semantics=sem, vmem_limit_bytes=vmem)


def _window(rows, cols, index):
    def aligned(*args):
        r, c = index(*args)
        hint = lambda x, m: x if isinstance(x, int) else pl.multiple_of(x, m)
        return hint(r, SLOT_PAD), hint(c, LANES)
    return pl.BlockSpec((pl.Element(rows), pl.Element(cols)), aligned)


def _norm_rows(h, g, valid):
    y = h * lax.rsqrt(jnp.mean(h * h, axis=-1, keepdims=True) + RMS_EPS)
    return jnp.where(valid[:, 0:1] > 0, y * g, 0.0)


def _final_norm_kernel(h_ref, g_ref, o_ref):
    x = h_ref[...]
    o_ref[...] = x * lax.rsqrt(jnp.mean(x * x, axis=-1, keepdims=True) + RMS_EPS) * g_ref[...]


def _final_norm(h, g, trunk):
    d = h.shape[1]
    tr = _pick(trunk.n_tok, 256, 16)
    nr = trunk.n_tok // tr
    tok0 = trunk.base + SLOT_PAD + N_META
    out = pl.pallas_call(
        _final_norm_kernel,
        grid=(trunk.batch, nr),
        in_specs=[_window(tr, d, lambda i, r: (tok0 + i * trunk.slot + r * tr, 0)),
                  pl.BlockSpec((1, d), lambda i, r: (0, 0))],
        out_specs=pl.BlockSpec((tr, d), lambda i, r: (i * nr + r, 0)),
        out_shape=jax.ShapeDtypeStruct((trunk.batch * trunk.n_tok, d), _F32),
        compiler_params=_params(("arbitrary", "arbitrary")),
        name="final_norm",
    )(h, g.reshape(1, d))
    return out.reshape(trunk.batch, trunk.n_tok, d)


def _conv3(p_ref, k, c, r0, nr):
    at = lambda shift: p_ref[k, HALO + shift + r0:HALO + shift + r0 + nr, :]
    return c[0:1] * at(-1) + c[1:2] * at(0) + c[2:3] * at(1)


def _ffn_epilogue(vals, outs, rows):
    g, v = vals
    outs[0][rows, :] = (g * (1.0 / (1.0 + jnp.exp(-g))) * v).astype(outs[0].dtype)


def _hyena_epilogue(vals, outs, rows):
    x0, x1, v = vals
    outs[0][rows, :] = x0
    outs[1][rows, :] = v * x1


def _halo_norm_kernel(hp_ref, hc_ref, hn_ref, vp_ref, vc_ref, vn_ref, g_ref, o_ref):
    g = g_ref[...]
    rows = jnp.concatenate([_norm_rows(hp_ref[...], g, vp_ref[...]), _norm_rows(hc_ref[...], g, vc_ref[...]),
                            _norm_rows(hn_ref[...], g, vn_ref[...])], axis=0)
    o_ref[...] = pltpu.bitcast(rows.astype(_BF16), jnp.uint32)


def _halo_norm(h, valid, gamma, tm):
    rows, d = h.shape
    per, last = tm // HALO, rows // HALO - 1

    def panel(width):
        return [pl.BlockSpec((HALO, width), lambda i: (jnp.maximum(i * per - 1, 0), 0)),
                pl.BlockSpec((tm, width), lambda i: (i, 0)),
                pl.BlockSpec((HALO, width), lambda i: (jnp.minimum((i + 1) * per, last), 0))]

    return pl.pallas_call(
        _halo_norm_kernel,
        grid=(rows // tm,),
        in_specs=panel(d) + panel(LANES) + [pl.BlockSpec((1, d), lambda i: (0, 0))],
        out_specs=pl.BlockSpec((None, tm // 2 + HALO, d), lambda i: (i, 0, 0)),
        out_shape=jax.ShapeDtypeStruct((rows // tm, tm // 2 + HALO, d), jnp.uint32),
        compiler_params=_params(("arbitrary",)),
        name="halo_norm",
    )(h, h, h, valid, valid, valid, gamma.reshape(1, d))


def _plain_norm_kernel(h_ref, v_ref, g_ref, o_ref):
    o_ref[...] = _norm_rows(h_ref[...], g_ref[...], v_ref[...]).astype(o_ref.dtype)


def _plain_norm(h, valid, gamma):
    rows, d = h.shape
    tr = _pick(rows, 256, 16)
    return pl.pallas_call(
        _plain_norm_kernel,
        grid=(rows // tr,),
        in_specs=[pl.BlockSpec((tr, d), lambda i: (i, 0)),
                  pl.BlockSpec((tr, LANES), lambda i: (i, 0)),
                  pl.BlockSpec((1, d), lambda i: (0, 0))],
        out_specs=pl.BlockSpec((tr, d), lambda i: (i, 0)),
        out_shape=jax.ShapeDtypeStruct((rows, d), _BF16),
        compiler_params=_params(("arbitrary",)),
        name="rmsnorm",
    )(h, valid, gamma.reshape(1, d))


def _conv_matmul_kernel(*refs, tm, nw, n_onto, epilogue):
    x_ref = refs[0]
    w_refs = refs[1:1 + nw]
    c_refs = refs[1 + nw:1 + 2 * nw]
    out_refs = refs[1 + 2 * nw + n_onto:-2]
    pa_ref, pb_ref = refs[-2:]
    s = pl.program_id(0)

    @pl.when(s == 0)
    def _():
        pb_ref[...] = jnp.zeros_like(pb_ref)

    def step(prev_ref, cur_ref):
        tn = cur_ref.shape[2]
        pieces = [(k, c0) for k in range(nw) for c0 in range(0, tn, MXU_COLS)]
        chunks = list(range(0, tm, EPILOGUE_ROWS))
        per_piece = -(-len(chunks) // len(pieces))
        cs = [c_refs[k][...] for k in range(nw)]
        x = pltpu.bitcast(x_ref[...], _BF16)
        for n, (k, c0) in enumerate(pieces):
            cols = slice(c0, min(c0 + MXU_COLS, tn))
            cur_ref[k, :, cols] = jnp.dot(x, w_refs[k][:, cols], preferred_element_type=_F32)
            for r0 in chunks[n * per_piece:(n + 1) * per_piece]:
                epilogue([_conv3(prev_ref, j, cs[j], r0, EPILOGUE_ROWS) for j in range(nw)], out_refs,
                         slice(r0, r0 + EPILOGUE_ROWS))

    @pl.when(s % 2 == 0)
    def _():
        step(pb_ref, pa_ref)

    @pl.when(s % 2 == 1)
    def _():
        step(pa_ref, pb_ref)


def _conv_matmul(xp, ws, cs, group_tiles, tn, nj, tile0, out_cols, epilogue, out_dtypes, name, onto=()):
    ni, packed_rows, d = xp.shape
    tmh = 2 * packed_rows
    tm = tmh - 2 * HALO
    nw = len(ws)
    steps = ni * nj
    cur = lambda s: jnp.minimum(s, steps - 1)
    lag = lambda s: jnp.maximum(s - 1, 0)
    wspec = lambda k: pl.BlockSpec((d, tn), lambda s: (0, cur(s) // ni + tile0 + group_tiles[k]))
    cspec = lambda k: pl.BlockSpec((3, tn), lambda s: (0, lag(s) // ni + tile0 + group_tiles[k]))
    out = pl.BlockSpec((tm, tn), lambda s: (lag(s) % ni, lag(s) // ni + tile0))
    return pl.pallas_call(
        functools.partial(_conv_matmul_kernel, tm=tm, nw=nw, n_onto=len(onto), epilogue=epilogue),
        grid=(steps + 1,),
        in_specs=[pl.BlockSpec((None, packed_rows, d), lambda s: (cur(s) % ni, 0, 0))]
        + [wspec(k) for k in range(nw)] + [cspec(k) for k in range(nw)]
        + [pl.BlockSpec(memory_space=pl.ANY)] * len(onto),
        out_specs=[out] * len(out_dtypes),
        out_shape=[jax.ShapeDtypeStruct((ni * tm, out_cols), dt) for dt in out_dtypes],
        input_output_aliases={1 + 2 * nw + k: k for k in range(len(onto))},
        scratch_shapes=[pltpu.VMEM((nw, tmh, tn), _F32), pltpu.VMEM((nw, tmh, tn), _F32)],
        compiler_params=_params(("arbitrary",)),
        name=name,
    )(xp, *ws, *cs, *onto)


def _qkv_kernel(x_ref, w_ref, o_ref, *, n_q_tiles, scale):
    acc = jnp.dot(x_ref[...], w_ref[...], preferred_element_type=_F32)
    s = jnp.where(pl.program_id(0) < n_q_tiles, scale, 1.0).astype(_F32)
    o_ref[...] = (acc * s).astype(o_ref.dtype)


def _qkv(hn, w_qkv, tm):
    rows, d = hn.shape
    n = w_qkv.shape[1]
    tn = _pick(d, 1024, LANES)
    return pl.pallas_call(
        functools.partial(_qkv_kernel, n_q_tiles=d // tn, scale=HEAD_DIM ** -0.5),
        grid=(n // tn, rows // tm),
        in_specs=[pl.BlockSpec((tm, d), lambda j, i: (i, 0)),
                  pl.BlockSpec((d, tn), lambda j, i: (0, j))],
        out_specs=pl.BlockSpec((tm, tn), lambda j, i: (i, j)),
        out_shape=jax.ShapeDtypeStruct((rows, n), _BF16),
        compiler_params=_params(("arbitrary", "arbitrary")),
        name="qkv_proj",
    )(hn, w_qkv)


def _residual_kernel(x_ref, w_ref, h_ref, o_ref):
    o_ref[...] = h_ref[...] + jnp.dot(x_ref[...], w_ref[...], preferred_element_type=_F32)


def _residual_matmul(x, w, h, tm, name):
    rows, k = x.shape
    d = w.shape[1]
    weights_resident = k <= RESIDENT_WEIGHT_K
    tn = _pick(d, 1024 if weights_resident else 256, LANES)
    grid = (d // tn, rows // tm) if weights_resident else (rows // tm, d // tn)
    ij = (lambda a, b: (b, a)) if weights_resident else (lambda a, b: (a, b))
    return pl.pallas_call(
        _residual_kernel,
        grid=grid,
        in_specs=[pl.BlockSpec((tm, k), lambda a, b: (ij(a, b)[0], 0)),
                  pl.BlockSpec((k, tn), lambda a, b: (0, ij(a, b)[1])),
                  pl.BlockSpec((tm, tn), lambda a, b: ij(a, b))],
        out_specs=pl.BlockSpec((tm, tn), lambda a, b: ij(a, b)),
        out_shape=jax.ShapeDtypeStruct((rows, d), _F32),
        input_output_aliases={2: 0},
        compiler_params=_params(("arbitrary", "arbitrary")),
        name=name,
    )(x, w, h)


class _Dft(NamedTuple):
    fwd: jax.Array
    inv: jax.Array
    twiddle: jax.Array


def _dft_matrices(seq):
    assert seq % FFT_RADIX == 0
    n1, n1z = _fft_sizes(seq)
    n = FFT_RADIX * n1
    k1 = jnp.arange(n1, dtype=jnp.int32)[:, None]
    m1 = jnp.arange(n1z, dtype=jnp.int32)[None, :]
    phase = ((m1 * k1) % n1).astype(_F32) * (2.0 * math.pi / n1)
    f = jnp.concatenate([jnp.cos(phase), -jnp.sin(phase)], axis=0)
    f = jnp.where(m1 * FFT_RADIX < seq, f, 0.0)
    p = jnp.arange(FFT_RADIX, dtype=jnp.int32)[:, None]
    tphase = (p * k1[:, 0][None, :]).astype(_F32) * (2.0 * math.pi / n)
    tw = jnp.stack([jnp.cos(tphase), -jnp.sin(tphase)], axis=1)
    tw = jnp.broadcast_to(tw[..., None], tw.shape + (TWIDDLE_LANES,))
    return _Dft(f.astype(_BF16), (f.T * (1.0 / n)).astype(_BF16), tw)


def _plane_rows(refs, plane, n1z):
    rows = pl.ds(plane, n1z, stride=FFT_RADIX)
    return jnp.concatenate([ref[rows, :] for ref in refs], axis=1)


def _planes_per_step(n1, dc):
    plane_block = 2 * FFT_RADIX * n1 * dc * 4
    return FFT_RADIX if plane_block <= DFT_BLOCK_BYTES else 1


def _lane_windows(n_rows, dc, row0, row_stride):
    sub = lambda l: _window(n_rows, LANES, lambda i, c, p: (row0 + i * row_stride, c * dc + l * LANES))
    return [sub(l) for l in range(dc // LANES)]


def _paired_dft(f, xa_refs, xb_refs, plane, n1z):
    xa, xb = _plane_rows(xa_refs, plane, n1z), _plane_rows(xb_refs, plane, n1z)
    dc, n1 = xa.shape[1], f.shape[0] // 2
    a = jnp.dot(f, jnp.concatenate([xa, xb], axis=1).astype(_BF16), preferred_element_type=_F32)
    return a[:n1, :dc] - a[n1:, dc:], a[n1:, :dc] + a[:n1, dc:]


def _dft_fwd_kernel(*refs, pp, pair):
    x_refs, f_ref, t_ref, o_ref = refs[:-3], refs[-3], refs[-2], refs[-1]
    n1z, n1 = f_ref.shape[1], o_ref.shape[2]
    half = len(x_refs) // 2
    for q in range(pp):
        plane = pl.program_id(2) * pp + q
        if pair:
            ar, ai = _paired_dft(f_ref[...], x_refs[:half], x_refs[half:], plane, n1z)
        else:
            a = jnp.dot(f_ref[...], _plane_rows(x_refs, plane, n1z).astype(_BF16), preferred_element_type=_F32)
            ar, ai = a[:n1], a[n1:]
        tr, ti = t_ref[q, 0, :, 0:1], t_ref[q, 1, :, 0:1]
        o_ref[0, q] = ar * tr - ai * ti
        o_ref[1, q] = ar * ti + ai * tr


def _dft_fwd(x, dft, n_sig, row0, row_stride, pair=False, lanes=256):
    d = x.shape[1]
    r = FFT_RADIX
    n1, n1z = dft.fwd.shape[0] // 2, dft.fwd.shape[1]
    dc = _pick(d, lanes, LANES)
    pp = _planes_per_step(n1, dc)
    if pair:
        assert n_sig % 2 == 0
        n_sig //= 2
        wins = (_lane_windows(n1z * r, dc, row0, 2 * row_stride)
                + _lane_windows(n1z * r, dc, row0 + row_stride, 2 * row_stride))
    else:
        wins = _lane_windows(n1z * r, dc, row0, row_stride)
    return pl.pallas_call(
        functools.partial(_dft_fwd_kernel, pp=pp, pair=pair),
        grid=(n_sig, d // dc, r // pp),
        in_specs=wins + [pl.BlockSpec((2 * n1, n1z), lambda i, c, p: (0, 0)),
                         pl.BlockSpec((pp, 2, n1, TWIDDLE_LANES), lambda i, c, p: (p, 0, 0, 0))],
        out_specs=pl.BlockSpec((2, pp, None, n1, dc), lambda i, c, p: (0, p, i, 0, c)),
        out_shape=jax.ShapeDtypeStruct((2, r, n_sig, n1, d), _F32),
        compiler_params=_params(("arbitrary", "arbitrary", "arbitrary")),
        name="dft_fwd",
    )(*([x] * len(wins)), dft.fwd, dft.twiddle)


def _cmul_const(x, w):
    re, im = x
    wr, wi = float(w.real), float(w.imag)
    if abs(wi) < 1e-12:
        return (re, im) if wr > 0 else (-re, -im)
    if abs(wr) < 1e-12:
        return (-im, re) if wi > 0 else (im, -re)
    return (re * wr - im * wi, re * wi + im * wr)


def _fft_planes(xs, sign):
    n = len(xs)
    if n == 1:
        return xs
    ev = _fft_planes(xs[0::2], sign)
    od = _fft_planes(xs[1::2], sign)
    out = [None] * n
    for k in range(n // 2):
        w = complex(math.cos(2.0 * math.pi * k / n), sign * math.sin(2.0 * math.pi * k / n))
        tr, ti = _cmul_const(od[k], w)
        out[k] = (ev[k][0] + tr, ev[k][1] + ti)
        out[k + n // 2] = (ev[k][0] - tr, ev[k][1] - ti)
    return out


def _tile_loop(tk, dc, body):
    def step(r, carry):
        rows = pl.ds(pl.multiple_of(r * 8, 8), 8)
        for l in range(dc // LANES):
            body(rows, slice(l * LANES, (l + 1) * LANES))
        return carry
    lax.fori_loop(0, tk // 8, step, 0)


def _filter_spectrum_kernel(a_ref, s_ref, o_ref):
    r, tk, dc = o_ref.shape[1], o_ref.shape[2], o_ref.shape[3]

    def body(rows, lanes):
        inv = 1.0 / s_ref[:, lanes]
        xf = _fft_planes([(a_ref[0, n, 0, rows, lanes], a_ref[1, n, 0, rows, lanes]) for n in range(r)], -1)
        xb = _fft_planes([(a_ref[0, n, 1, rows, lanes], a_ref[1, n, 1, rows, lanes]) for n in range(r)], -1)
        for k in range(r):
            o_ref[0, k, rows, lanes] = (xf[k][0] + xb[k][0]) * inv
            o_ref[1, k, rows, lanes] = (xf[k][1] - xb[k][1]) * inv

    _tile_loop(tk, dc, body)


def _plane_tile(n1):
    return _pick(n1, 352, 8)


def _filter_spectrum(planes, norm):
    _, r, _, n1, d = planes.shape
    tk, dc = _plane_tile(n1), LANES
    return pl.pallas_call(
        _filter_spectrum_kernel,
        grid=(d // dc, n1 // tk),
        in_specs=[pl.BlockSpec((2, r, 2, tk, dc), lambda c, k: (0, 0, 0, k, c)),
                  pl.BlockSpec((1, dc), lambda c, k: (0, c))],
        out_specs=pl.BlockSpec((2, r, tk, dc), lambda c, k: (0, 0, k, c)),
        out_shape=jax.ShapeDtypeStruct((2, r, n1, d), _F32),
        compiler_params=_params(("arbitrary", "arbitrary")),
        name="filter_spectrum",
    )(planes, norm)


def _spectral_product_kernel(a_ref, k_ref, o_ref):
    r, tk, dc = o_ref.shape[1], o_ref.shape[2], o_ref.shape[3]

    def body(rows, lanes):
        x = _fft_planes([(a_ref[0, n, rows, lanes], a_ref[1, n, rows, lanes]) for n in range(r)], -1)
        y = []
        for k in range(r):
            kr, ki = k_ref[0, k, rows, lanes], k_ref[1, k, rows, lanes]
            y.append((x[k][0] * kr - x[k][1] * ki, x[k][0] * ki + x[k][1] * kr))
        z = _fft_planes(y, +1)
        for n in range(r):
            o_ref[0, n, rows, lanes] = z[n][0]
            o_ref[1, n, rows, lanes] = z[n][1]

    _tile_loop(tk, dc, body)


def _spectral_product(planes, kspec):
    _, r, b, n1, d = planes.shape
    tk, dc = _plane_tile(n1), LANES
    spec = pl.BlockSpec((2, r, None, tk, dc), lambda c, k, i: (0, 0, i, k, c))
    return pl.pallas_call(
        _spectral_product_kernel,
        grid=(d // dc, n1 // tk, b),
        in_specs=[spec, pl.BlockSpec((2, r, tk, dc), lambda c, k, i: (0, 0, k, c))],
        out_specs=spec,
        out_shape=jax.ShapeDtypeStruct(planes.shape, _F32),
        compiler_params=_params(("arbitrary", "arbitrary", "arbitrary")),
        name="spectral_product",
    )(planes, kspec)


def _untwiddle(br, bi, tr, ti):
    return br * tr + bi * ti, bi * tr - br * ti


def _dft_inv_kernel(*refs, pp, nsub, pair):
    b_ref, h_ref, t_ref, skip_ref, o_ref = refs[0], refs[1], refs[2], refs[-2], refs[-1]
    sigs = refs[3:-2]
    n1z = h_ref.shape[0]
    dc = skip_ref.shape[1]
    for q in range(pp):
        plane = pl.program_id(2) * pp + q
        cr, ci = _untwiddle(b_ref[0, q], b_ref[1, q], t_ref[q, 0, :, 0:1], t_ref[q, 1, :, 0:1])
        if pair:
            vxa, vxb, x0a, x0b = (sigs[k * nsub:(k + 1) * nsub] for k in range(4))
            bb = jnp.concatenate([jnp.concatenate([cr, ci], axis=0), jnp.concatenate([ci, -cr], axis=0)], axis=1)
            y = jnp.dot(h_ref[...], bb.astype(_BF16), preferred_element_type=_F32)
            o_ref[0, q] = (y[:, :dc] + _plane_rows(vxa, plane, n1z) * skip_ref[...]) * _plane_rows(x0a, plane, n1z)
            o_ref[1, q] = (y[:, dc:] + _plane_rows(vxb, plane, n1z) * skip_ref[...]) * _plane_rows(x0b, plane, n1z)
        else:
            vx_refs, x0_refs = sigs[:nsub], sigs[nsub:]
            bb = jnp.concatenate([cr, ci], axis=0).astype(_BF16)
            y = jnp.dot(h_ref[...], bb, preferred_element_type=_F32)
            o_ref[0, q] = (y + _plane_rows(vx_refs, plane, n1z) * skip_ref[...]) * _plane_rows(x0_refs, plane, n1z)


def _dft_inv(planes, dft, vx, x0, skip, batch, row0, row_stride, pair=False):
    _, r, n_sets, n1, d = planes.shape
    n1z = dft.inv.shape[0]
    per = 2 if pair else 1
    assert n_sets * per == batch
    dc = _pick(d, 256 // per, LANES)
    pp = _planes_per_step(n1, dc)
    wins = [w for k in range(per) for w in _lane_windows(n1z * r, dc, row0 + k * row_stride, per * row_stride)]
    return pl.pallas_call(
        functools.partial(_dft_inv_kernel, pp=pp, nsub=len(wins) // per, pair=pair),
        grid=(n_sets, d // dc, r // pp),
        in_specs=[pl.BlockSpec((2, pp, None, n1, dc), lambda i, c, p: (0, p, i, 0, c)),
                  pl.BlockSpec((n1z, 2 * n1), lambda i, c, p: (0, 0)),
                  pl.BlockSpec((pp, 2, n1, TWIDDLE_LANES), lambda i, c, p: (p, 0, 0, 0))]
        + wins + wins + [pl.BlockSpec((1, dc), lambda i, c, p: (0, c))],
        out_specs=pl.BlockSpec((per, pp, n1z, dc), lambda i, c, p: (i, p, 0, c)),
        out_shape=jax.ShapeDtypeStruct((batch, r, n1z, d), _F32),
        compiler_params=_params(("arbitrary", "arbitrary", "arbitrary")),
        name="dft_inv",
    )(planes, dft.inv, dft.twiddle, *([vx] * len(wins)), *([x0] * len(wins)), skip)


def _fused_conv_kernel(*refs, nsub):
    vxa, vxb, x0a, x0b = (refs[k * nsub:(k + 1) * nsub] for k in range(4))
    f_ref, h_ref, t_ref, k_ref, skip_ref, o_ref, a_ref = refs[4 * nsub:]
    r, n1, dc = a_ref.shape[1], a_ref.shape[2], a_ref.shape[3]
    n1z = f_ref.shape[1]
    for p in range(r):
        ar, ai = _paired_dft(f_ref[...], vxa, vxb, p, n1z)
        tr, ti = t_ref[p, 0, :, 0:1], t_ref[p, 1, :, 0:1]
        a_ref[0, p] = ar * tr - ai * ti
        a_ref[1, p] = ar * ti + ai * tr

    def body(rows, lanes):
        x = _fft_planes([(a_ref[0, n, rows, lanes], a_ref[1, n, rows, lanes]) for n in range(r)], -1)
        y = []
        for k in range(r):
            kr, ki = k_ref[0, k, rows, lanes], k_ref[1, k, rows, lanes]
            y.append((x[k][0] * kr - x[k][1] * ki, x[k][0] * ki + x[k][1] * kr))
        z = _fft_planes(y, +1)
        for n in range(r):
            a_ref[0, n, rows, lanes] = z[n][0]
            a_ref[1, n, rows, lanes] = z[n][1]

    _tile_loop(n1, dc, body)
    for p in range(r):
        cr, ci = _untwiddle(a_ref[0, p], a_ref[1, p], t_ref[p, 0, :, 0:1], t_ref[p, 1, :, 0:1])
        bb = jnp.concatenate([jnp.concatenate([cr, ci], axis=0), jnp.concatenate([ci, -cr], axis=0)], axis=1)
        y = jnp.dot(h_ref[...], bb.astype(_BF16), preferred_element_type=_F32)
        o_ref[0, p] = (y[:, :dc] + _plane_rows(vxa, p, n1z) * skip_ref[...]) * _plane_rows(x0a, p, n1z)
        o_ref[1, p] = (y[:, dc:] + _plane_rows(vxb, p, n1z) * skip_ref[...]) * _plane_rows(x0b, p, n1z)


def _fused_conv(vx, x0, kspec, dft, skip, batch, row0, row_stride):
    assert batch % 2 == 0
    d = vx.shape[1]
    r = FFT_RADIX
    n1, n1z = dft.fwd.shape[0] // 2, dft.fwd.shape[1]
    dc = LANES
    sig = lambda spec: pl.BlockSpec(spec.block_shape, lambda c, i: spec.index_map(i, c, 0))
    wa = [sig(w) for w in _lane_windows(n1z * r, dc, row0, 2 * row_stride)]
    wb = [sig(w) for w in _lane_windows(n1z * r, dc, row0 + row_stride, 2 * row_stride)]
    return pl.pallas_call(
        functools.partial(_fused_conv_kernel, nsub=len(wa)),
        grid=(d // dc, batch // 2),
        in_specs=wa + wb + wa + wb + [
            pl.BlockSpec((2 * n1, n1z), lambda c, i: (0, 0)),
            pl.BlockSpec((n1z, 2 * n1), lambda c, i: (0, 0)),
            pl.BlockSpec((r, 2, n1, TWIDDLE_LANES), lambda c, i: (0, 0, 0, 0)),
            pl.BlockSpec((2, r, n1, dc), lambda c, i: (0, 0, 0, c)),
            pl.BlockSpec((1, dc), lambda c, i: (0, c))],
        out_specs=pl.BlockSpec((2, r, n1z, dc), lambda c, i: (i, 0, 0, c)),
        out_shape=jax.ShapeDtypeStruct((batch, r, n1z, d), _F32),
        scratch_shapes=[pltpu.VMEM((2, r, n1, dc), _F32)],
        compiler_params=_params(("arbitrary", "arbitrary")),
        name="hyena_conv",
    )(*([vx] * (2 * len(wa))), *([x0] * (2 * len(wa))), dft.fwd, dft.inv, dft.twiddle, kspec, skip)


def _filter_mlp_kernel(z_ref, w1_ref, wi_ref, b_ref, f_ref, o_ref):
    hi = lax.Precision.HIGHEST
    h = jnp.sin(f_ref[0:1] * (jnp.dot(z_ref[...], w1_ref[...], precision=hi, preferred_element_type=_F32) + b_ref[0:1]))
    h = jnp.sin(f_ref[1:2] * (jnp.dot(h, wi_ref[0], precision=hi, preferred_element_type=_F32) + b_ref[1:2]))
    h = jnp.sin(f_ref[2:3] * (jnp.dot(h, wi_ref[1], precision=hi, preferred_element_type=_F32) + b_ref[2:3]))
    o_ref[...] = h


def _filter_taps_kernel(hm_ref, z_ref, wf_ref, wb_ref, d_ref, k_ref, s_ref):
    hi = lax.Precision.HIGHEST
    t = z_ref[:, 0:1]
    mf = z_ref[:, HY_EMB:HY_EMB + 1]
    mb = z_ref[:, HY_EMB + 1:HY_EMB + 2]
    hm = hm_ref[...]
    kf = jnp.dot(hm, wf_ref[...], precision=hi, preferred_element_type=_F32) * jnp.exp(-t * jnp.abs(d_ref[0:1])) * mf
    kb = jnp.dot(hm, wb_ref[...], precision=hi, preferred_element_type=_F32) * jnp.exp(-t * jnp.abs(d_ref[1:2])) * mb
    k_ref[0] = kf
    k_ref[1] = kb

    @pl.when(pl.program_id(1) == 0)
    def _():
        s_ref[...] = jnp.zeros_like(s_ref)
    s_ref[...] += jnp.sum(jnp.abs(kf) + jnp.abs(kb), axis=0, keepdims=True)


def _position_features(seq, rows):
    pos = jnp.arange(seq, dtype=_F32)[:, None]
    t = pos / max(seq - 1, 1)
    f = jnp.linspace(1e-4, HY_BANDS - 1, HY_BANDS, dtype=_F32)[None, :]
    ang = f * (2.0 * math.pi / seq) * pos
    fwd = jnp.ones((seq, 1), _F32)
    bwd = (pos >= 1).astype(_F32)
    z = jnp.concatenate([t, jnp.cos(ang), -jnp.sin(ang), fwd, bwd], axis=-1)
    return jnp.pad(z, ((0, rows - seq), (0, LANES - z.shape[1])))


def _hyena_filter(seq, w1, w_inner, b, freq, w_out, delta, dft):
    _, n1z = _fft_sizes(seq)
    lp = n1z * FFT_RADIX
    width = w1.shape[1]
    d = w_out.shape[1] // 2
    z = _position_features(seq, lp)
    w1p = jnp.pad(w1, ((0, LANES - w1.shape[0]), (0, 0)))
    tr = n1z
    hm = pl.pallas_call(
        _filter_mlp_kernel,
        grid=(lp // tr,),
        in_specs=[pl.BlockSpec((tr, LANES), lambda i: (i, 0)),
                  pl.BlockSpec((LANES, width), lambda i: (0, 0)),
                  pl.BlockSpec((2, width, width), lambda i: (0, 0, 0)),
                  pl.BlockSpec((3, width), lambda i: (0, 0)),
                  pl.BlockSpec((3, width), lambda i: (0, 0))],
        out_specs=pl.BlockSpec((tr, width), lambda i: (i, 0)),
        out_shape=jax.ShapeDtypeStruct((lp, width), _F32),
        compiler_params=_params(("arbitrary",)),
        name="filter_mlp",
    )(z, w1p, w_inner, b, freq)
    dc = _pick(d, 512, LANES)
    nc = d // dc
    taps, norm = pl.pallas_call(
        _filter_taps_kernel,
        grid=(nc, lp // tr),
        in_specs=[pl.BlockSpec((tr, width), lambda c, i: (i, 0)),
                  pl.BlockSpec((tr, LANES), lambda c, i: (i, 0)),
                  pl.BlockSpec((width, dc), lambda c, i: (0, c)),
                  pl.BlockSpec((width, dc), lambda c, i: (0, c + nc)),
                  pl.BlockSpec((2, dc), lambda c, i: (0, c))],
        out_specs=[pl.BlockSpec((2, tr, dc), lambda c, i: (0, i, c)), pl.BlockSpec((1, dc), lambda c, i: (0, c))],
        out_shape=[jax.ShapeDtypeStruct((2, lp, d), _F32), jax.ShapeDtypeStruct((1, d), _F32)],
        compiler_params=_params(("arbitrary", "arbitrary")),
        name="filter_taps",
    )(hm, z, w_out, w_out, delta)
    n1 = dft.fwd.shape[0] // 2
    lanes = 256 if _planes_per_step(n1, _pick(d, 256, LANES)) == FFT_RADIX else 512
    return _filter_spectrum(_dft_fwd(taps.reshape(2 * lp, d), dft, 2, 0, lp, lanes=lanes), norm)


def _to_flat(parts, trunks, rows, dtype):
    d = parts[0].shape[-1]
    out = []
    for part, t in zip(parts, trunks):
        out.append(jnp.pad(part.astype(dtype), ((0, 0), (SLOT_PAD, 0), (0, 0))).reshape(t.batch * t.slot, d))
    used = sum(t.batch * t.slot for t in trunks)
    out.append(jnp.zeros((rows - used, d), dtype))
    return jnp.concatenate(out, axis=0)


def _hyena_mixer(h, valid, gamma, trunks, rows, tm, w_in, w_conv, filt, delta, skip, w_out, dft):
    d = h.shape[1]
    tn = _pick(d, 512, LANES)
    nj = d // tn
    tm_in = _pick(rows, _HYENA_TM, 2 * HALO)
    x0, vx = _conv_matmul(_halo_norm(h, valid, gamma, tm_in), [w_in] * 3, [w_conv] * 3, [0, nj, 2 * nj], tn, nj, 0, d,
                          _hyena_epilogue, (_F32, _F32), "hyena_in")
    parts = []
    for t in trunks:
        kspec = _hyena_filter(t.seq, *filt, delta, dft[t.seq])
        row0 = t.base + SLOT_PAD
        n1 = dft[t.seq].fwd.shape[0] // 2
        pair = t.batch % 2 == 0
        if pair and _planes_per_step(n1, _pick(d, 256, LANES)) == FFT_RADIX:
            y = _fused_conv(vx, x0, kspec, dft[t.seq], skip.reshape(1, -1), t.batch, row0, t.slot)
        else:
            planes = _spectral_product(_dft_fwd(vx, dft[t.seq], t.batch, row0, t.slot, pair), kspec)
            y = _dft_inv(planes, dft[t.seq], vx, x0, skip.reshape(1, -1), t.batch, row0, t.slot, pair)
        parts.append(jnp.swapaxes(y, 1, 2).reshape(t.batch, -1, d)[:, :t.seq])
    return _residual_matmul(_to_flat(parts, trunks, rows, _BF16), w_out, h, tm, "hyena_out")


def _attention_bias(rpb, rows_in_grid):
    n_blocks = rows_in_grid // Q_ROWS
    kr_win = min(WIN_ROWS, rows_in_grid)
    qr = np.arange(Q_ROWS)[:, None, None]
    kb = np.arange(3)[None, :, None]
    kr = np.arange(Q_ROWS)[None, None, :]
    dr_idx, row_ok = [], []
    for g in (0, 1, n_blocks - 1):
        r = Q_ROWS * g + qr
        key_row = Q_ROWS * (g + kb - 1) + kr
        rs = np.clip(r - kr_win // 2, 0, rows_in_grid - kr_win)
        ok = (key_row >= rs) & (key_row < rs + kr_win) & (g + kb - 1 >= 0) & (g + kb - 1 < n_blocks)
        dr_idx.append(np.clip(key_row - r + (WIN_ROWS - 1), 0, 2 * WIN_ROWS - 2))
        row_ok.append(ok)
    dr_idx = np.stack(dr_idx)
    row_ok = np.stack(row_ok)
    cols = np.arange(GRID_W)
    col_start = np.clip(cols - WIN_COLS // 2, 0, GRID_W - WIN_COLS)
    col_ok = (cols[None, :] >= col_start[:, None]) & (cols[None, :] < col_start[:, None] + WIN_COLS)
    col_idx = np.clip(cols[None, :] - cols[:, None] + WIN_COLS - 1, 0, 2 * WIN_COLS - 2)
    by_col = rpb[:, :, col_idx]
    by_row = by_col[:, dr_idx]
    ok = row_ok[:, :, :, :, None, None] & col_ok[None, None, None, None]
    bias = jnp.where(ok[None], by_row, MASK_VALUE)
    bias = jnp.transpose(bias, (1, 0, 2, 5, 3, 4, 6))
    return bias.reshape(3, rpb.shape[0], Q_BLOCK, 3 * Q_BLOCK).astype(_F32)


def _na_kernel(q_ref, k0_ref, k1_ref, k2_ref, v0_ref, v1_ref, v2_ref, km_ref, vm_ref, bias_ref, o_ref, *, heads):
    nt = (((1,), (1,)), ((), ()))
    for h in range(heads):
        hs = slice(h * HEAD_DIM, (h + 1) * HEAD_DIM)
        q = q_ref[:, hs]
        s = [lax.dot_general(q, k_ref[:, hs], nt, preferred_element_type=_F32)
             + bias_ref[h, :, i * Q_BLOCK:(i + 1) * Q_BLOCK]
             for i, k_ref in enumerate((k0_ref, k1_ref, k2_ref))]
        sm = lax.dot_general(q, km_ref[:, hs], nt, preferred_element_type=_F32)
        m = jnp.max(sm, axis=-1, keepdims=True)
        for si in s:
            m = jnp.maximum(m, jnp.max(si, axis=-1, keepdims=True))
        pm = jnp.exp(sm - m)
        den = jnp.sum(pm, axis=-1, keepdims=True)
        acc = jnp.dot(pm.astype(_BF16), vm_ref[:, hs], preferred_element_type=_F32)
        for si, v_ref in zip(s, (v0_ref, v1_ref, v2_ref)):
            p = jnp.exp(si - m)
            den = den + jnp.sum(p, axis=-1, keepdims=True)
            acc = acc + jnp.dot(p.astype(_BF16), v_ref[:, hs], preferred_element_type=_F32)
        o_ref[:, hs] = (acc / den).astype(o_ref.dtype)


def _na_meta_kernel(q_ref, k_ref, v_ref, o_ref, *, heads):
    nt = (((1,), (1,)), ((), ()))
    for h in range(heads):
        hs = slice(h * HEAD_DIM, (h + 1) * HEAD_DIM)
        s = lax.dot_general(q_ref[:, hs], k_ref[:, hs], nt, preferred_element_type=_F32)
        p = jnp.exp(s - jnp.max(s, axis=-1, keepdims=True))
        acc = jnp.dot(p.astype(_BF16), v_ref[:, hs], preferred_element_type=_F32)
        o_ref[:, hs] = (acc / jnp.sum(p, axis=-1, keepdims=True)).astype(o_ref.dtype)


def _neighbourhood_attention(qkv, trunk, bias, flat):
    d = qkv.shape[1] // 3
    n_heads = d // HEAD_DIM
    hb = min(4, n_heads)
    wd = hb * HEAD_DIM
    nb = trunk.n_tok // Q_BLOCK
    b = trunk.batch
    tok0 = trunk.base + SLOT_PAD + N_META
    meta0 = (trunk.base + SLOT_PAD) // N_META
    slot_meta = trunk.slot // N_META

    def win(col0, shift):
        return _window(Q_BLOCK, wd, lambda h, i, g: (
            tok0 + i * trunk.slot + jnp.clip(g + shift, 0, nb - 1) * Q_BLOCK, col0 + h * wd))

    def meta(col0):
        return _window(N_META, wd, lambda h, i, g: (trunk.base + SLOT_PAD + i * trunk.slot, col0 + h * wd))

    variant = lambda g: jnp.where(g == 0, 0, jnp.where(g == nb - 1, 2, 1))
    by_row = lambda spec: pl.BlockSpec(spec.block_shape, lambda h, g, i: spec.index_map(h, i, g))
    flat = pl.pallas_call(
        lambda *refs: _na_kernel(*refs[:10], refs[-1], heads=hb),
        grid=(n_heads // hb, nb, b),
        in_specs=[by_row(s) for s in (win(0, 0), win(d, -1), win(d, 0), win(d, 1), win(2 * d, -1), win(2 * d, 0),
                                      win(2 * d, 1), meta(d), meta(2 * d))]
        + [pl.BlockSpec((None, hb, Q_BLOCK, 3 * Q_BLOCK), lambda h, g, i: (variant(g), h, 0, 0)),
           pl.BlockSpec(memory_space=pl.ANY)],
        out_specs=by_row(_window(Q_BLOCK, wd, lambda h, i, g: (tok0 + i * trunk.slot + g * Q_BLOCK, h * wd))),
        out_shape=jax.ShapeDtypeStruct(flat.shape, flat.dtype),
        input_output_aliases={10: 0},
        compiler_params=_params(("arbitrary", "arbitrary", "arbitrary")),
        name="na_grid",
    )(qkv, qkv, qkv, qkv, qkv, qkv, qkv, qkv, qkv, bias, flat)
    mspec = lambda k: pl.BlockSpec((N_META, d), lambda i: (meta0 + i * slot_meta, k))
    return pl.pallas_call(
        lambda q_ref, k_ref, v_ref, flat_ref, o_ref: _na_meta_kernel(q_ref, k_ref, v_ref, o_ref, heads=n_heads),
        grid=(b,),
        in_specs=[mspec(0), mspec(1), mspec(2), pl.BlockSpec(memory_space=pl.ANY)],
        out_specs=mspec(0),
        out_shape=jax.ShapeDtypeStruct(flat.shape, flat.dtype),
        input_output_aliases={3: 0},
        compiler_params=_params(("arbitrary",)),
        name="na_meta",
    )(qkv, qkv, qkv, flat)


def _na_mixer(h, valid, gamma, trunks, rows, tm, w_qkv, rpb, w_out):
    qkv = _qkv(_plain_norm(h, valid, gamma), w_qkv, tm)
    biases = {}
    flat = jnp.zeros((rows, h.shape[1]), _BF16)
    for t in trunks:
        grid_rows = t.n_tok // GRID_W
        key = grid_rows if grid_rows < 3 * Q_ROWS else -1
        if key not in biases:
            biases[key] = _attention_bias(rpb, grid_rows)
        flat = _neighbourhood_attention(qkv, t, biases[key], flat)
    return _residual_matmul(flat, w_out, h, tm, "na_out")


def _valid_rows(trunks, rows):
    r = jnp.arange(rows, dtype=jnp.int32)[:, None]
    v = jnp.zeros((rows, 1), jnp.bool_)
    for t in trunks:
        inside = (r >= t.base) & (r < t.base + t.batch * t.slot)
        v = v | (inside & ((r - t.base) % t.slot >= SLOT_PAD))
    return jnp.broadcast_to(v.astype(_F32), (rows, LANES))


def _ffn_up(xp, w_up, w_conv):
    f = w_up.shape[1] // 2
    ws = [w_up[:, :f].astype(_BF16), w_up[:, f:].astype(_BF16)]
    cs = [w_conv[:, :f], w_conv[:, f:]]
    tn = min(_FFN_TN, f)
    nj = f // tn
    a = _conv_matmul(xp, ws, cs, [0, 0], tn, nj, 0, f, _ffn_epilogue, (_BF16,), "ffn_up")
    tail = f - nj * tn
    if tail:
        assert tail % LANES == 0 and (nj * tn) % tail == 0
        a = _conv_matmul(xp, ws, cs, [0, 0], tail, 1, nj * tn // tail, f, _ffn_epilogue, (_BF16,), "ffn_up_tail",
                         onto=a)
    return a[0]


def kernel(x_prompt, x_sample, meta_tokens, norm_mix, norm_ffn, norm_final, hy_w_in, hy_w_conv, hy_f_w1,
           hy_f_w_inner, hy_f_b, hy_f_freq, hy_f_w_out, hy_delta, hy_skip, hy_w_out, na_w_qkv, na_rpb,
           na_w_out, ffn_w_up, ffn_w_conv, ffn_w_down):
    d = x_prompt.shape[-1]
    depth = norm_mix.shape[0]
    xs = (x_prompt, x_sample)
    tm = _TM
    trunks, rows = _plan([(x.shape[0], x.shape[1]) for x in xs], tm)
    valid = _valid_rows(trunks, rows)
    dft = {t.seq: _dft_matrices(t.seq) for t in trunks}

    meta = meta_tokens.astype(_F32)
    seqs = [jnp.concatenate([jnp.broadcast_to(meta[None], (x.shape[0], N_META, d)), x], axis=1) for x in xs]
    h = _to_flat(seqs, trunks, rows, _F32)

    bf = lambda w: w.astype(_BF16)
    for i in range(depth):
        j = i // 2
        if i % 2 == 0:
            filt = (hy_f_w1[j], hy_f_w_inner[j], hy_f_b[j], hy_f_freq[j], hy_f_w_out[j])
            h = _hyena_mixer(h, valid, norm_mix[i], trunks, rows, tm, bf(hy_w_in[j]), hy_w_conv[j], filt,
                             hy_delta[j], hy_skip[j], bf(hy_w_out[j]), dft)
        else:
            h = _na_mixer(h, valid, norm_mix[i], trunks, rows, tm, bf(na_w_qkv[j]), na_rpb[j], bf(na_w_out[j]))
        a = _ffn_up(_halo_norm(h, valid, norm_ffn[i], tm), ffn_w_up[i], ffn_w_conv[i])
        h = _residual_matmul(a, bf(ffn_w_down[i]), h, tm, "ffn_down")
    return tuple(_final_norm(h, norm_final, t) for t in trunks)
```

```python
import functools
import math
from typing import NamedTuple

import numpy as np
import jax
import jax.numpy as jnp
from jax import lax
from jax.experimental import pallas as pl
from jax.experimental.pallas import tpu as pltpu

N_META = 16
GRID_W = 64
HEAD_DIM = 128
WIN_ROWS = 8
WIN_COLS = 16
HY_EMB = 33
HY_BANDS = (HY_EMB - 1) // 2
RMS_EPS = 1e-6

SLOT_PAD = 16
HALO = 8
EPILOGUE_ROWS = 32
MXU_COLS = 256
RESIDENT_WEIGHT_K = 4096
FFT_RADIX = 16
Q_ROWS = 4
Q_BLOCK = Q_ROWS * GRID_W
MASK_VALUE = -1e30
LANES = 128
V7X_VMEM_LIMIT = 56 * 1024 * 1024
DFT_BLOCK_BYTES = 10 * 1024 * 1024
TWIDDLE_LANES = 8

_TM = 832
_HYENA_TM = 640
_FFN_TN = 512
_F32 = jnp.float32
_BF16 = jnp.bfloat16


class _Trunk(NamedTuple):
    batch: int
    n_tok: int
    seq: int
    slot: int
    base: int


def _round_up(x, m):
    return (x + m - 1) // m * m


def _pick(n, target, mult):
    best = None
    for d in range(mult, min(n, target) + 1, mult):
        if n % d == 0:
            best = d
    return best if best is not None else n


def _fft_sizes(seq):
    n1 = _round_up(-(-(2 * seq - 1) // FFT_RADIX), 8)
    n1z = _round_up(-(-seq // FFT_RADIX), 8)
    return n1, n1z


def _plan(shapes, tm):
    trunks, base = [], 0
    for batch, n_tok in shapes:
        seq = N_META + n_tok
        slot = SLOT_PAD + seq
        trunks.append(_Trunk(batch, n_tok, seq, slot, base))
        base += batch * slot
    reach = max(t.base + (t.batch - 1) * t.slot + SLOT_PAD + FFT_RADIX * _fft_sizes(t.seq)[1] for t in trunks)
    rows = _round_up(max(base + SLOT_PAD, reach), tm)
    return trunks, rows


def _params(sem, vmem=V7X_VMEM_LIMIT):
    return pltpu.CompilerParams(dimension_semantics=sem, vmem_limit_bytes=vmem)


def _window(rows, cols, index):
    def aligned(*args):
        r, c = index(*args)
        hint = lambda x, m: x if isinstance(x, int) else pl.multiple_of(x, m)
        return hint(r, SLOT_PAD), hint(c, LANES)
    return pl.BlockSpec((pl.Element(rows), pl.Element(cols)), aligned)


def _norm_rows(h, g, valid):
    y = h * lax.rsqrt(jnp.mean(h * h, axis=-1, keepdims=True) + RMS_EPS)
    return jnp.where(valid[:, 0:1] > 0, y * g, 0.0)


def _final_norm_kernel(h_ref, g_ref, o_ref):
    x = h_ref[...]
    o_ref[...] = x * lax.rsqrt(jnp.mean(x * x, axis=-1, keepdims=True) + RMS_EPS) * g_ref[...]


def _final_norm(h, g, trunk):
    d = h.shape[1]
    tr = _pick(trunk.n_tok, 256, 16)
    nr = trunk.n_tok // tr
    tok0 = trunk.base + SLOT_PAD + N_META
    out = pl.pallas_call(
        _final_norm_kernel,
        grid=(trunk.batch, nr),
        in_specs=[_window(tr, d, lambda i, r: (tok0 + i * trunk.slot + r * tr, 0)),
                  pl.BlockSpec((1, d), lambda i, r: (0, 0))],
        out_specs=pl.BlockSpec((tr, d), lambda i, r: (i * nr + r, 0)),
        out_shape=jax.ShapeDtypeStruct((trunk.batch * trunk.n_tok, d), _F32),
        compiler_params=_params(("arbitrary", "arbitrary")),
        name="final_norm",
    )(h, g.reshape(1, d))
    return out.reshape(trunk.batch, trunk.n_tok, d)


def _conv3(p_ref, k, c, r0, nr):
    at = lambda shift: p_ref[k, HALO + shift + r0:HALO + shift + r0 + nr, :]
    return c[0:1] * at(-1) + c[1:2] * at(0) + c[2:3] * at(1)


def _ffn_epilogue(vals, outs, rows):
    g, v = vals
    outs[0][rows, :] = (g * (1.0 / (1.0 + jnp.exp(-g))) * v).astype(outs[0].dtype)


def _hyena_epilogue(vals, outs, rows):
    x0, x1, v = vals
    outs[0][rows, :] = x0
    outs[1][rows, :] = v * x1


def _halo_norm_kernel(hp_ref, hc_ref, hn_ref, vp_ref, vc_ref, vn_ref, g_ref, o_ref):
    g = g_ref[...]
    rows = jnp.concatenate([_norm_rows(hp_ref[...], g, vp_ref[...]), _norm_rows(hc_ref[...], g, vc_ref[...]),
                            _norm_rows(hn_ref[...], g, vn_ref[...])], axis=0)
    o_ref[...] = pltpu.bitcast(rows.astype(_BF16), jnp.uint32)


def _halo_norm(h, valid, gamma, tm):
    rows, d = h.shape
    per, last = tm // HALO, rows // HALO - 1

    def panel(width):
        return [pl.BlockSpec((HALO, width), lambda i: (jnp.maximum(i * per - 1, 0), 0)),
                pl.BlockSpec((tm, width), lambda i: (i, 0)),
                pl.BlockSpec((HALO, width), lambda i: (jnp.minimum((i + 1) * per, last), 0))]

    return pl.pallas_call(
        _halo_norm_kernel,
        grid=(rows // tm,),
        in_specs=panel(d) + panel(LANES) + [pl.BlockSpec((1, d), lambda i: (0, 0))],
        out_specs=pl.BlockSpec((None, tm // 2 + HALO, d), lambda i: (i, 0, 0)),
        out_shape=jax.ShapeDtypeStruct((rows // tm, tm // 2 + HALO, d), jnp.uint32),
        compiler_params=_params(("arbitrary",)),
        name="halo_norm",
    )(h, h, h, valid, valid, valid, gamma.reshape(1, d))


def _plain_norm_kernel(h_ref, v_ref, g_ref, o_ref):
    o_ref[...] = _norm_rows(h_ref[...], g_ref[...], v_ref[...]).astype(o_ref.dtype)


def _plain_norm(h, valid, gamma):
    rows, d = h.shape
    tr = _pick(rows, 256, 16)
    return pl.pallas_call(
        _plain_norm_kernel,
        grid=(rows // tr,),
        in_specs=[pl.BlockSpec((tr, d), lambda i: (i, 0)),
                  pl.BlockSpec((tr, LANES), lambda i: (i, 0)),
                  pl.BlockSpec((1, d), lambda i: (0, 0))],
        out_specs=pl.BlockSpec((tr, d), lambda i: (i, 0)),
        out_shape=jax.ShapeDtypeStruct((rows, d), _BF16),
        compiler_params=_params(("arbitrary",)),
        name="rmsnorm",
    )(h, valid, gamma.reshape(1, d))


def _conv_matmul_kernel(*refs, tm, nw, n_onto, epilogue):
    x_ref = refs[0]
    w_refs = refs[1:1 + nw]
    c_refs = refs[1 + nw:1 + 2 * nw]
    out_refs = refs[1 + 2 * nw + n_onto:-2]
    pa_ref, pb_ref = refs[-2:]
    s = pl.program_id(0)

    @pl.when(s == 0)
    def _():
        pb_ref[...] = jnp.zeros_like(pb_ref)

    def step(prev_ref, cur_ref):
        tn = cur_ref.shape[2]
        pieces = [(k, c0) for k in range(nw) for c0 in range(0, tn, MXU_COLS)]
        chunks = list(range(0, tm, EPILOGUE_ROWS))
        per_piece = -(-len(chunks) // len(pieces))
        cs = [c_refs[k][...] for k in range(nw)]
        x = pltpu.bitcast(x_ref[...], _BF16)
        for n, (k, c0) in enumerate(pieces):
            cols = slice(c0, min(c0 + MXU_COLS, tn))
            cur_ref[k, :, cols] = jnp.dot(x, w_refs[k][:, cols], preferred_element_type=_F32)
            for r0 in chunks[n * per_piece:(n + 1) * per_piece]:
                epilogue([_conv3(prev_ref, j, cs[j], r0, EPILOGUE_ROWS) for j in range(nw)], out_refs,
                         slice(r0, r0 + EPILOGUE_ROWS))

    @pl.when(s % 2 == 0)
    def _():
        step(pb_ref, pa_ref)

    @pl.when(s % 2 == 1)
    def _():
        step(pa_ref, pb_ref)


def _conv_matmul(xp, ws, cs, group_tiles, tn, nj, tile0, out_cols, epilogue, out_dtypes, name, onto=()):
    ni, packed_rows, d = xp.shape
    tmh = 2 * packed_rows
    tm = tmh - 2 * HALO
    nw = len(ws)
    steps = ni * nj
    cur = lambda s: jnp.minimum(s, steps - 1)
    lag = lambda s: jnp.maximum(s - 1, 0)
    wspec = lambda k: pl.BlockSpec((d, tn), lambda s: (0, cur(s) // ni + tile0 + group_tiles[k]))
    cspec = lambda k: pl.BlockSpec((3, tn), lambda s: (0, lag(s) // ni + tile0 + group_tiles[k]))
    out = pl.BlockSpec((tm, tn), lambda s: (lag(s) % ni, lag(s) // ni + tile0))
    return pl.pallas_call(
        functools.partial(_conv_matmul_kernel, tm=tm, nw=nw, n_onto=len(onto), epilogue=epilogue),
        grid=(steps + 1,),
        in_specs=[pl.BlockSpec((None, packed_rows, d), lambda s: (cur(s) % ni, 0, 0))]
        + [wspec(k) for k in range(nw)] + [cspec(k) for k in range(nw)]
        + [pl.BlockSpec(memory_space=pl.ANY)] * len(onto),
        out_specs=[out] * len(out_dtypes),
        out_shape=[jax.ShapeDtypeStruct((ni * tm, out_cols), dt) for dt in out_dtypes],
        input_output_aliases={1 + 2 * nw + k: k for k in range(len(onto))},
        scratch_shapes=[pltpu.VMEM((nw, tmh, tn), _F32), pltpu.VMEM((nw, tmh, tn), _F32)],
        compiler_params=_params(("arbitrary",)),
        name=name,
    )(xp, *ws, *cs, *onto)


def _qkv_kernel(x_ref, w_ref, o_ref, *, n_q_tiles, scale):
    acc = jnp.dot(x_ref[...], w_ref[...], preferred_element_type=_F32)
    s = jnp.where(pl.program_id(0) < n_q_tiles, scale, 1.0).astype(_F32)
    o_ref[...] = (acc * s).astype(o_ref.dtype)


def _qkv(hn, w_qkv, tm):
    rows, d = hn.shape
    n = w_qkv.shape[1]
    tn = _pick(d, 1024, LANES)
    return pl.pallas_call(
        functools.partial(_qkv_kernel, n_q_tiles=d // tn, scale=HEAD_DIM ** -0.5),
        grid=(n // tn, rows // tm),
        in_specs=[pl.BlockSpec((tm, d), lambda j, i: (i, 0)),
                  pl.BlockSpec((d, tn), lambda j, i: (0, j))],
        out_specs=pl.BlockSpec((tm, tn), lambda j, i: (i, j)),
        out_shape=jax.ShapeDtypeStruct((rows, n), _BF16),
        compiler_params=_params(("arbitrary", "arbitrary")),
        name="qkv_proj",
    )(hn, w_qkv)


def _residual_kernel(x_ref, w_ref, h_ref, o_ref):
    o_ref[...] = h_ref[...] + jnp.dot(x_ref[...], w_ref[...], preferred_element_type=_F32)


def _residual_matmul(x, w, h, tm, name):
    rows, k = x.shape
    d = w.shape[1]
    weights_resident = k <= RESIDENT_WEIGHT_K
    tn = _pick(d, 1024 if weights_resident else 256, LANES)
    grid = (d // tn, rows // tm) if weights_resident else (rows // tm, d // tn)
    ij = (lambda a, b: (b, a)) if weights_resident else (lambda a, b: (a, b))
    return pl.pallas_call(
        _residual_kernel,
        grid=grid,
        in_specs=[pl.BlockSpec((tm, k), lambda a, b: (ij(a, b)[0], 0)),
                  pl.BlockSpec((k, tn), lambda a, b: (0, ij(a, b)[1])),
                  pl.BlockSpec((tm, tn), lambda a, b: ij(a, b))],
        out_specs=pl.BlockSpec((tm, tn), lambda a, b: ij(a, b)),
        out_shape=jax.ShapeDtypeStruct((rows, d), _F32),
        input_output_aliases={2: 0},
        compiler_params=_params(("arbitrary", "arbitrary")),
        name=name,
    )(x, w, h)


class _Dft(NamedTuple):
    fwd: jax.Array
    inv: jax.Array
    twiddle: jax.Array


def _dft_matrices(seq):
    assert seq % FFT_RADIX == 0
    n1, n1z = _fft_sizes(seq)
    n = FFT_RADIX * n1
    k1 = jnp.arange(n1, dtype=jnp.int32)[:, None]
    m1 = jnp.arange(n1z, dtype=jnp.int32)[None, :]
    phase = ((m1 * k1) % n1).astype(_F32) * (2.0 * math.pi / n1)
    f = jnp.concatenate([jnp.cos(phase), -jnp.sin(phase)], axis=0)
    f = jnp.where(m1 * FFT_RADIX < seq, f, 0.0)
    p = jnp.arange(FFT_RADIX, dtype=jnp.int32)[:, None]
    tphase = (p * k1[:, 0][None, :]).astype(_F32) * (2.0 * math.pi / n)
    tw = jnp.stack([jnp.cos(tphase), -jnp.sin(tphase)], axis=1)
    tw = jnp.broadcast_to(tw[..., None], tw.shape + (TWIDDLE_LANES,))
    return _Dft(f.astype(_BF16), (f.T * (1.0 / n)).astype(_BF16), tw)


def _plane_rows(refs, plane, n1z):
    rows = pl.ds(plane, n1z, stride=FFT_RADIX)
    return jnp.concatenate([ref[rows, :] for ref in refs], axis=1)


def _planes_per_step(n1, dc):
    plane_block = 2 * FFT_RADIX * n1 * dc * 4
    return FFT_RADIX if plane_block <= DFT_BLOCK_BYTES else 1


def _lane_windows(n_rows, dc, row0, row_stride):
    sub = lambda l: _window(n_rows, LANES, lambda i, c, p: (row0 + i * row_stride, c * dc + l * LANES))
    return [sub(l) for l in range(dc // LANES)]


def _paired_dft(f, xa_refs, xb_refs, plane, n1z):
    xa, xb = _plane_rows(xa_refs, plane, n1z), _plane_rows(xb_refs, plane, n1z)
    dc, n1 = xa.shape[1], f.shape[0] // 2
    a = jnp.dot(f, jnp.concatenate([xa, xb], axis=1).astype(_BF16), preferred_element_type=_F32)
    return a[:n1, :dc] - a[n1:, dc:], a[n1:, :dc] + a[:n1, dc:]


def _dft_fwd_kernel(*refs, pp, pair):
    x_refs, f_ref, t_ref, o_ref = refs[:-3], refs[-3], refs[-2], refs[-1]
    n1z, n1 = f_ref.shape[1], o_ref.shape[2]
    half = len(x_refs) // 2
    for q in range(pp):
        plane = pl.program_id(2) * pp + q
        if pair:
            ar, ai = _paired_dft(f_ref[...], x_refs[:half], x_refs[half:], plane, n1z)
        else:
            a = jnp.dot(f_ref[...], _plane_rows(x_refs, plane, n1z).astype(_BF16), preferred_element_type=_F32)
            ar, ai = a[:n1], a[n1:]
        tr, ti = t_ref[q, 0, :, 0:1], t_ref[q, 1, :, 0:1]
        o_ref[0, q] = ar * tr - ai * ti
        o_ref[1, q] = ar * ti + ai * tr


def _dft_fwd(x, dft, n_sig, row0, row_stride, pair=False, lanes=256):
    d = x.shape[1]
    r = FFT_RADIX
    n1, n1z = dft.fwd.shape[0] // 2, dft.fwd.shape[1]
    dc = _pick(d, lanes, LANES)
    pp = _planes_per_step(n1, dc)
    if pair:
        assert n_sig % 2 == 0
        n_sig //= 2
        wins = (_lane_windows(n1z * r, dc, row0, 2 * row_stride)
                + _lane_windows(n1z * r, dc, row0 + row_stride, 2 * row_stride))
    else:
        wins = _lane_windows(n1z * r, dc, row0, row_stride)
    return pl.pallas_call(
        functools.partial(_dft_fwd_kernel, pp=pp, pair=pair),
        grid=(n_sig, d // dc, r // pp),
        in_specs=wins + [pl.BlockSpec((2 * n1, n1z), lambda i, c, p: (0, 0)),
                         pl.BlockSpec((pp, 2, n1, TWIDDLE_LANES), lambda i, c, p: (p, 0, 0, 0))],
        out_specs=pl.BlockSpec((2, pp, None, n1, dc), lambda i, c, p: (0, p, i, 0, c)),
        out_shape=jax.ShapeDtypeStruct((2, r, n_sig, n1, d), _F32),
        compiler_params=_params(("arbitrary", "arbitrary", "arbitrary")),
        name="dft_fwd",
    )(*([x] * len(wins)), dft.fwd, dft.twiddle)


def _cmul_const(x, w):
    re, im = x
    wr, wi = float(w.real), float(w.imag)
    if abs(wi) < 1e-12:
        return (re, im) if wr > 0 else (-re, -im)
    if abs(wr) < 1e-12:
        return (-im, re) if wi > 0 else (im, -re)
    return (re * wr - im * wi, re * wi + im * wr)


def _fft_planes(xs, sign):
    n = len(xs)
    if n == 1:
        return xs
    ev = _fft_planes(xs[0::2], sign)
    od = _fft_planes(xs[1::2], sign)
    out = [None] * n
    for k in range(n // 2):
        w = complex(math.cos(2.0 * math.pi * k / n), sign * math.sin(2.0 * math.pi * k / n))
        tr, ti = _cmul_const(od[k], w)
        out[k] = (ev[k][0] + tr, ev[k][1] + ti)
        out[k + n // 2] = (ev[k][0] - tr, ev[k][1] - ti)
    return out


def _tile_loop(tk, dc, body):
    def step(r, carry):
        rows = pl.ds(pl.multiple_of(r * 8, 8), 8)
        for l in range(dc // LANES):
            body(rows, slice(l * LANES, (l + 1) * LANES))
        return carry
    lax.fori_loop(0, tk // 8, step, 0)


def _filter_spectrum_kernel(a_ref, s_ref, o_ref):
    r, tk, dc = o_ref.shape[1], o_ref.shape[2], o_ref.shape[3]

    def body(rows, lanes):
        inv = 1.0 / s_ref[:, lanes]
        xf = _fft_planes([(a_ref[0, n, 0, rows, lanes], a_ref[1, n, 0, rows, lanes]) for n in range(r)], -1)
        xb = _fft_planes([(a_ref[0, n, 1, rows, lanes], a_ref[1, n, 1, rows, lanes]) for n in range(r)], -1)
        for k in range(r):
            o_ref[0, k, rows, lanes] = (xf[k][0] + xb[k][0]) * inv
            o_ref[1, k, rows, lanes] = (xf[k][1] - xb[k][1]) * inv

    _tile_loop(tk, dc, body)


def _plane_tile(n1):
    return _pick(n1, 352, 8)


def _filter_spectrum(planes, norm):
    _, r, _, n1, d = planes.shape
    tk, dc = _plane_tile(n1), LANES
    return pl.pallas_call(
        _filter_spectrum_kernel,
        grid=(d // dc, n1 // tk),
        in_specs=[pl.BlockSpec((2, r, 2, tk, dc), lambda c, k: (0, 0, 0, k, c)),
                  pl.BlockSpec((1, dc), lambda c, k: (0, c))],
        out_specs=pl.BlockSpec((2, r, tk, dc), lambda c, k: (0, 0, k, c)),
        out_shape=jax.ShapeDtypeStruct((2, r, n1, d), _F32),
        compiler_params=_params(("arbitrary", "arbitrary")),
        name="filter_spectrum",
    )(planes, norm)


def _spectral_product_kernel(a_ref, k_ref, o_ref):
    r, tk, dc = o_ref.shape[1], o_ref.shape[2], o_ref.shape[3]

    def body(rows, lanes):
        x = _fft_planes([(a_ref[0, n, rows, lanes], a_ref[1, n, rows, lanes]) for n in range(r)], -1)
        y = []
        for k in range(r):
            kr, ki = k_ref[0, k, rows, lanes], k_ref[1, k, rows, lanes]
            y.append((x[k][0] * kr - x[k][1] * ki, x[k][0] * ki + x[k][1] * kr))
        z = _fft_planes(y, +1)
        for n in range(r):
            o_ref[0, n, rows, lanes] = z[n][0]
            o_ref[1, n, rows, lanes] = z[n][1]

    _tile_loop(tk, dc, body)


def _spectral_product(planes, kspec):
    _, r, b, n1, d = planes.shape
    tk, dc = _plane_tile(n1), LANES
    spec = pl.BlockSpec((2, r, None, tk, dc), lambda c, k, i: (0, 0, i, k, c))
    return pl.pallas_call(
        _spectral_product_kernel,
        grid=(d // dc, n1 // tk, b),
        in_specs=[spec, pl.BlockSpec((2, r, tk, dc), lambda c, k, i: (0, 0, k, c))],
        out_specs=spec,
        out_shape=jax.ShapeDtypeStruct(planes.shape, _F32),
        compiler_params=_params(("arbitrary", "arbitrary", "arbitrary")),
        name="spectral_product",
    )(planes, kspec)


def _untwiddle(br, bi, tr, ti):
    return br * tr + bi * ti, bi * tr - br * ti


def _dft_inv_kernel(*refs, pp, nsub, pair):
    b_ref, h_ref, t_ref, skip_ref, o_ref = refs[0], refs[1], refs[2], refs[-2], refs[-1]
    sigs = refs[3:-2]
    n1z = h_ref.shape[0]
    dc = skip_ref.shape[1]
    for q in range(pp):
        plane = pl.program_id(2) * pp + q
        cr, ci = _untwiddle(b_ref[0, q], b_ref[1, q], t_ref[q, 0, :, 0:1], t_ref[q, 1, :, 0:1])
        if pair:
            vxa, vxb, x0a, x0b = (sigs[k * nsub:(k + 1) * nsub] for k in range(4))
            bb = jnp.concatenate([jnp.concatenate([cr, ci], axis=0), jnp.concatenate([ci, -cr], axis=0)], axis=1)
            y = jnp.dot(h_ref[...], bb.astype(_BF16), preferred_element_type=_F32)
            o_ref[0, q] = (y[:, :dc] + _plane_rows(vxa, plane, n1z) * skip_ref[...]) * _plane_rows(x0a, plane, n1z)
            o_ref[1, q] = (y[:, dc:] + _plane_rows(vxb, plane, n1z) * skip_ref[...]) * _plane_rows(x0b, plane, n1z)
        else:
            vx_refs, x0_refs = sigs[:nsub], sigs[nsub:]
            bb = jnp.concatenate([cr, ci], axis=0).astype(_BF16)
            y = jnp.dot(h_ref[...], bb, preferred_element_type=_F32)
            o_ref[0, q] = (y + _plane_rows(vx_refs, plane, n1z) * skip_ref[...]) * _plane_rows(x0_refs, plane, n1z)


def _dft_inv(planes, dft, vx, x0, skip, batch, row0, row_stride, pair=False):
    _, r, n_sets, n1, d = planes.shape
    n1z = dft.inv.shape[0]
    per = 2 if pair else 1
    assert n_sets * per == batch
    dc = _pick(d, 256 // per, LANES)
    pp = _planes_per_step(n1, dc)
    wins = [w for k in range(per) for w in _lane_windows(n1z * r, dc, row0 + k * row_stride, per * row_stride)]
    return pl.pallas_call(
        functools.partial(_dft_inv_kernel, pp=pp, nsub=len(wins) // per, pair=pair),
        grid=(n_sets, d // dc, r // pp),
        in_specs=[pl.BlockSpec((2, pp, None, n1, dc), lambda i, c, p: (0, p, i, 0, c)),
                  pl.BlockSpec((n1z, 2 * n1), lambda i, c, p: (0, 0)),
                  pl.BlockSpec((pp, 2, n1, TWIDDLE_LANES), lambda i, c, p: (p, 0, 0, 0))]
        + wins + wins + [pl.BlockSpec((1, dc), lambda i, c, p: (0, c))],
        out_specs=pl.BlockSpec((per, pp, n1z, dc), lambda i, c, p: (i, p, 0, c)),
        out_shape=jax.ShapeDtypeStruct((batch, r, n1z, d), _F32),
        compiler_params=_params(("arbitrary", "arbitrary", "arbitrary")),
        name="dft_inv",
    )(planes, dft.inv, dft.twiddle, *([vx] * len(wins)), *([x0] * len(wins)), skip)


def _fused_conv_kernel(*refs, nsub):
    vxa, vxb, x0a, x0b = (refs[k * nsub:(k + 1) * nsub] for k in range(4))
    f_ref, h_ref, t_ref, k_ref, skip_ref, o_ref, a_ref = refs[4 * nsub:]
    r, n1, dc = a_ref.shape[1], a_ref.shape[2], a_ref.shape[3]
    n1z = f_ref.shape[1]
    for p in range(r):
        ar, ai = _paired_dft(f_ref[...], vxa, vxb, p, n1z)
        tr, ti = t_ref[p, 0, :, 0:1], t_ref[p, 1, :, 0:1]
        a_ref[0, p] = ar * tr - ai * ti
        a_ref[1, p] = ar * ti + ai * tr

    def body(rows, lanes):
        x = _fft_planes([(a_ref[0, n, rows, lanes], a_ref[1, n, rows, lanes]) for n in range(r)], -1)
        y = []
        for k in range(r):
            kr, ki = k_ref[0, k, rows, lanes], k_ref[1, k, rows, lanes]
            y.append((x[k][0] * kr - x[k][1] * ki, x[k][0] * ki + x[k][1] * kr))
        z = _fft_planes(y, +1)
        for n in range(r):
            a_ref[0, n, rows, lanes] = z[n][0]
            a_ref[1, n, rows, lanes] = z[n][1]

    _tile_loop(n1, dc, body)
    for p in range(r):
        cr, ci = _untwiddle(a_ref[0, p], a_ref[1, p], t_ref[p, 0, :, 0:1], t_ref[p, 1, :, 0:1])
        bb = jnp.concatenate([jnp.concatenate([cr, ci], axis=0), jnp.concatenate([ci, -cr], axis=0)], axis=1)
        y = jnp.dot(h_ref[...], bb.astype(_BF16), preferred_element_type=_F32)
        o_ref[0, p] = (y[:, :dc] + _plane_rows(vxa, p, n1z) * skip_ref[...]) * _plane_rows(x0a, p, n1z)
        o_ref[1, p] = (y[:, dc:] + _plane_rows(vxb, p, n1z) * skip_ref[...]) * _plane_rows(x0b, p, n1z)


def _fused_conv(vx, x0, kspec, dft, skip, batch, row0, row_stride):
    assert batch % 2 == 0
    d = vx.shape[1]
    r = FFT_RADIX
    n1, n1z = dft.fwd.shape[0] // 2, dft.fwd.shape[1]
    dc = LANES
    sig = lambda spec: pl.BlockSpec(spec.block_shape, lambda c, i: spec.index_map(i, c, 0))
    wa = [sig(w) for w in _lane_windows(n1z * r, dc, row0, 2 * row_stride)]
    wb = [sig(w) for w in _lane_windows(n1z * r, dc, row0 + row_stride, 2 * row_stride)]
    return pl.pallas_call(
        functools.partial(_fused_conv_kernel, nsub=len(wa)),
        grid=(d // dc, batch // 2),
        in_specs=wa + wb + wa + wb + [
            pl.BlockSpec((2 * n1, n1z), lambda c, i: (0, 0)),
            pl.BlockSpec((n1z, 2 * n1), lambda c, i: (0, 0)),
            pl.BlockSpec((r, 2, n1, TWIDDLE_LANES), lambda c, i: (0, 0, 0, 0)),
            pl.BlockSpec((2, r, n1, dc), lambda c, i: (0, 0, 0, c)),
            pl.BlockSpec((1, dc), lambda c, i: (0, c))],
        out_specs=pl.BlockSpec((2, r, n1z, dc), lambda c, i: (i, 0, 0, c)),
        out_shape=jax.ShapeDtypeStruct((batch, r, n1z, d), _F32),
        scratch_shapes=[pltpu.VMEM((2, r, n1, dc), _F32)],
        compiler_params=_params(("arbitrary", "arbitrary")),
        name="hyena_conv",
    )(*([vx] * (2 * len(wa))), *([x0] * (2 * len(wa))), dft.fwd, dft.inv, dft.twiddle, kspec, skip)


def _filter_mlp_kernel(z_ref, w1_ref, wi_ref, b_ref, f_ref, o_ref):
    hi = lax.Precision.HIGHEST
    h = jnp.sin(f_ref[0:1] * (jnp.dot(z_ref[...], w1_ref[...], precision=hi, preferred_element_type=_F32) + b_ref[0:1]))
    h = jnp.sin(f_ref[1:2] * (jnp.dot(h, wi_ref[0], precision=hi, preferred_element_type=_F32) + b_ref[1:2]))
    h = jnp.sin(f_ref[2:3] * (jnp.dot(h, wi_ref[1], precision=hi, preferred_element_type=_F32) + b_ref[2:3]))
    o_ref[...] = h


def _filter_taps_kernel(hm_ref, z_ref, wf_ref, wb_ref, d_ref, k_ref, s_ref):
    hi = lax.Precision.HIGHEST
    t = z_ref[:, 0:1]
    mf = z_ref[:, HY_EMB:HY_EMB + 1]
    mb = z_ref[:, HY_EMB + 1:HY_EMB + 2]
    hm = hm_ref[...]
    kf = jnp.dot(hm, wf_ref[...], precision=hi, preferred_element_type=_F32) * jnp.exp(-t * jnp.abs(d_ref[0:1])) * mf
    kb = jnp.dot(hm, wb_ref[...], precision=hi, preferred_element_type=_F32) * jnp.exp(-t * jnp.abs(d_ref[1:2])) * mb
    k_ref[0] = kf
    k_ref[1] = kb

    @pl.when(pl.program_id(1) == 0)
    def _():
        s_ref[...] = jnp.zeros_like(s_ref)
    s_ref[...] += jnp.sum(jnp.abs(kf) + jnp.abs(kb), axis=0, keepdims=True)


def _position_features(seq, rows):
    pos = jnp.arange(seq, dtype=_F32)[:, None]
    t = pos / max(seq - 1, 1)
    f = jnp.linspace(1e-4, HY_BANDS - 1, HY_BANDS, dtype=_F32)[None, :]
    ang = f * (2.0 * math.pi / seq) * pos
    fwd = jnp.ones((seq, 1), _F32)
    bwd = (pos >= 1).astype(_F32)
    z = jnp.concatenate([t, jnp.cos(ang), -jnp.sin(ang), fwd, bwd], axis=-1)
    return jnp.pad(z, ((0, rows - seq), (0, LANES - z.shape[1])))


def _hyena_filter(seq, w1, w_inner, b, freq, w_out, delta, dft):
    _, n1z = _fft_sizes(seq)
    lp = n1z * FFT_RADIX
    width = w1.shape[1]
    d = w_out.shape[1] // 2
    z = _position_features(seq, lp)
    w1p = jnp.pad(w1, ((0, LANES - w1.shape[0]), (0, 0)))
    tr = n1z
    hm = pl.pallas_call(
        _filter_mlp_kernel,
        grid=(lp // tr,),
        in_specs=[pl.BlockSpec((tr, LANES), lambda i: (i, 0)),
                  pl.BlockSpec((LANES, width), lambda i: (0, 0)),
                  pl.BlockSpec((2, width, width), lambda i: (0, 0, 0)),
                  pl.BlockSpec((3, width), lambda i: (0, 0)),
                  pl.BlockSpec((3, width), lambda i: (0, 0))],
        out_specs=pl.BlockSpec((tr, width), lambda i: (i, 0)),
        out_shape=jax.ShapeDtypeStruct((lp, width), _F32),
        compiler_params=_params(("arbitrary",)),
        name="filter_mlp",
    )(z, w1p, w_inner, b, freq)
    dc = _pick(d, 512, LANES)
    nc = d // dc
    taps, norm = pl.pallas_call(
        _filter_taps_kernel,
        grid=(nc, lp // tr),
        in_specs=[pl.BlockSpec((tr, width), lambda c, i: (i, 0)),
                  pl.BlockSpec((tr, LANES), lambda c, i: (i, 0)),
                  pl.BlockSpec((width, dc), lambda c, i: (0, c)),
                  pl.BlockSpec((width, dc), lambda c, i: (0, c + nc)),
                  pl.BlockSpec((2, dc), lambda c, i: (0, c))],
        out_specs=[pl.BlockSpec((2, tr, dc), lambda c, i: (0, i, c)), pl.BlockSpec((1, dc), lambda c, i: (0, c))],
        out_shape=[jax.ShapeDtypeStruct((2, lp, d), _F32), jax.ShapeDtypeStruct((1, d), _F32)],
        compiler_params=_params(("arbitrary", "arbitrary")),
        name="filter_taps",
    )(hm, z, w_out, w_out, delta)
    n1 = dft.fwd.shape[0] // 2
    lanes = 256 if _planes_per_step(n1, _pick(d, 256, LANES)) == FFT_RADIX else 512
    return _filter_spectrum(_dft_fwd(taps.reshape(2 * lp, d), dft, 2, 0, lp, lanes=lanes), norm)


def _to_flat(parts, trunks, rows, dtype):
    d = parts[0].shape[-1]
    out = []
    for part, t in zip(parts, trunks):
        out.append(jnp.pad(part.astype(dtype), ((0, 0), (SLOT_PAD, 0), (0, 0))).reshape(t.batch * t.slot, d))
    used = sum(t.batch * t.slot for t in trunks)
    out.append(jnp.zeros((rows - used, d), dtype))
    return jnp.concatenate(out, axis=0)


def _hyena_mixer(h, valid, gamma, trunks, rows, tm, w_in, w_conv, filt, delta, skip, w_out, dft):
    d = h.shape[1]
    tn = _pick(d, 512, LANES)
    nj = d // tn
    tm_in = _pick(rows, _HYENA_TM, 2 * HALO)
    x0, vx = _conv_matmul(_halo_norm(h, valid, gamma, tm_in), [w_in] * 3, [w_conv] * 3, [0, nj, 2 * nj], tn, nj, 0, d,
                          _hyena_epilogue, (_F32, _F32), "hyena_in")
    parts = []
    for t in trunks:
        kspec = _hyena_filter(t.seq, *filt, delta, dft[t.seq])
        row0 = t.base + SLOT_PAD
        n1 = dft[t.seq].fwd.shape[0] // 2
        pair = t.batch % 2 == 0
        if pair and _planes_per_step(n1, _pick(d, 256, LANES)) == FFT_RADIX:
            y = _fused_conv(vx, x0, kspec, dft[t.seq], skip.reshape(1, -1), t.batch, row0, t.slot)
        else:
            planes = _spectral_product(_dft_fwd(vx, dft[t.seq], t.batch, row0, t.slot, pair), kspec)
            y = _dft_inv(planes, dft[t.seq], vx, x0, skip.reshape(1, -1), t.batch, row0, t.slot, pair)
        parts.append(jnp.swapaxes(y, 1, 2).reshape(t.batch, -1, d)[:, :t.seq])
    return _residual_matmul(_to_flat(parts, trunks, rows, _BF16), w_out, h, tm, "hyena_out")


def _attention_bias(rpb, rows_in_grid):
    n_blocks = rows_in_grid // Q_ROWS
    kr_win = min(WIN_ROWS, rows_in_grid)
    qr = np.arange(Q_ROWS)[:, None, None]
    kb = np.arange(3)[None, :, None]
    kr = np.arange(Q_ROWS)[None, None, :]
    dr_idx, row_ok = [], []
    for g in (0, 1, n_blocks - 1):
        r = Q_ROWS * g + qr
        key_row = Q_ROWS * (g + kb - 1) + kr
        rs = np.clip(r - kr_win // 2, 0, rows_in_grid - kr_win)
        ok = (key_row >= rs) & (key_row < rs + kr_win) & (g + kb - 1 >= 0) & (g + kb - 1 < n_blocks)
        dr_idx.append(np.clip(key_row - r + (WIN_ROWS - 1), 0, 2 * WIN_ROWS - 2))
        row_ok.append(ok)
    dr_idx = np.stack(dr_idx)
    row_ok = np.stack(row_ok)
    cols = np.arange(GRID_W)
    col_start = np.clip(cols - WIN_COLS // 2, 0, GRID_W - WIN_COLS)
    col_ok = (cols[None, :] >= col_start[:, None]) & (cols[None, :] < col_start[:, None] + WIN_COLS)
    col_idx = np.clip(cols[None, :] - cols[:, None] + WIN_COLS - 1, 0, 2 * WIN_COLS - 2)
    by_col = rpb[:, :, col_idx]
    by_row = by_col[:, dr_idx]
    ok = row_ok[:, :, :, :, None, None] & col_ok[None, None, None, None]
    bias = jnp.where(ok[None], by_row, MASK_VALUE)
    bias = jnp.transpose(bias, (1, 0, 2, 5, 3, 4, 6))
    return bias.reshape(3, rpb.shape[0], Q_BLOCK, 3 * Q_BLOCK).astype(_F32)


def _na_kernel(q_ref, k0_ref, k1_ref, k2_ref, v0_ref, v1_ref, v2_ref, km_ref, vm_ref, bias_ref, o_ref, *, heads):
    nt = (((1,), (1,)), ((), ()))
    for h in range(heads):
        hs = slice(h * HEAD_DIM, (h + 1) * HEAD_DIM)
        q = q_ref[:, hs]
        s = [lax.dot_general(q, k_ref[:, hs], nt, preferred_element_type=_F32)
             + bias_ref[h, :, i * Q_BLOCK:(i + 1) * Q_BLOCK]
             for i, k_ref in enumerate((k0_ref, k1_ref, k2_ref))]
        sm = lax.dot_general(q, km_ref[:, hs], nt, preferred_element_type=_F32)
        m = jnp.max(sm, axis=-1, keepdims=True)
        for si in s:
            m = jnp.maximum(m, jnp.max(si, axis=-1, keepdims=True))
        pm = jnp.exp(sm - m)
        den = jnp.sum(pm, axis=-1, keepdims=True)
        acc = jnp.dot(pm.astype(_BF16), vm_ref[:, hs], preferred_element_type=_F32)
        for si, v_ref in zip(s, (v0_ref, v1_ref, v2_ref)):
            p = jnp.exp(si - m)
            den = den + jnp.sum(p, axis=-1, keepdims=True)
            acc = acc + jnp.dot(p.astype(_BF16), v_ref[:, hs], preferred_element_type=_F32)
        o_ref[:, hs] = (acc / den).astype(o_ref.dtype)


def _na_meta_kernel(q_ref, k_ref, v_ref, o_ref, *, heads):
    nt = (((1,), (1,)), ((), ()))
    for h in range(heads):
        hs = slice(h * HEAD_DIM, (h + 1) * HEAD_DIM)
        s = lax.dot_general(q_ref[:, hs], k_ref[:, hs], nt, preferred_element_type=_F32)
        p = jnp.exp(s - jnp.max(s, axis=-1, keepdims=True))
        acc = jnp.dot(p.astype(_BF16), v_ref[:, hs], preferred_element_type=_F32)
        o_ref[:, hs] = (acc / jnp.sum(p, axis=-1, keepdims=True)).astype(o_ref.dtype)


def _neighbourhood_attention(qkv, trunk, bias, flat):
    d = qkv.shape[1] // 3
    n_heads = d // HEAD_DIM
    hb = min(8, n_heads)
    wd = hb * HEAD_DIM
    nb = trunk.n_tok // Q_BLOCK
    b = trunk.batch
    tok0 = trunk.base + SLOT_PAD + N_META
    meta0 = (trunk.base + SLOT_PAD) // N_META
    slot_meta = trunk.slot // N_META

    def win(col0, shift):
        return _window(Q_BLOCK, wd, lambda h, i, g: (
            tok0 + i * trunk.slot + jnp.clip(g + shift, 0, nb - 1) * Q_BLOCK, col0 + h * wd))

    def meta(col0):
        return _window(N_META, wd, lambda h, i, g: (trunk.base + SLOT_PAD + i * trunk.slot, col0 + h * wd))

    variant = lambda g: jnp.where(g == 0, 0, jnp.where(g == nb - 1, 2, 1))
    by_row = lambda spec: pl.BlockSpec(spec.block_shape, lambda h, g, i: spec.index_map(h, i, g))
    flat = pl.pallas_call(
        lambda *refs: _na_kernel(*refs[:10], refs[-1], heads=hb),
        grid=(n_heads // hb, nb, b),
        in_specs=[by_row(s) for s in (win(0, 0), win(d, -1), win(d, 0), win(d, 1), win(2 * d, -1), win(2 * d, 0),
                                      win(2 * d, 1), meta(d), meta(2 * d))]
        + [pl.BlockSpec((None, hb, Q_BLOCK, 3 * Q_BLOCK), lambda h, g, i: (variant(g), h, 0, 0)),
           pl.BlockSpec(memory_space=pl.ANY)],
        out_specs=by_row(_window(Q_BLOCK, wd, lambda h, i, g: (tok0 + i * trunk.slot + g * Q_BLOCK, h * wd))),
        out_shape=jax.ShapeDtypeStruct(flat.shape, flat.dtype),
        input_output_aliases={10: 0},
        compiler_params=_params(("arbitrary", "arbitrary", "arbitrary")),
        name="na_grid",
    )(qkv, qkv, qkv, qkv, qkv, qkv, qkv, qkv, qkv, bias, flat)
    mspec = lambda k: pl.BlockSpec((N_META, d), lambda i: (meta0 + i * slot_meta, k))
    return pl.pallas_call(
        lambda q_ref, k_ref, v_ref, flat_ref, o_ref: _na_meta_kernel(q_ref, k_ref, v_ref, o_ref, heads=n_heads),
        grid=(b,),
        in_specs=[mspec(0), mspec(1), mspec(2), pl.BlockSpec(memory_space=pl.ANY)],
        out_specs=mspec(0),
        out_shape=jax.ShapeDtypeStruct(flat.shape, flat.dtype),
        input_output_aliases={3: 0},
        compiler_params=_params(("arbitrary",)),
        name="na_meta",
    )(qkv, qkv, qkv, flat)


def _na_mixer(h, valid, gamma, trunks, rows, tm, w_qkv, rpb, w_out):
    qkv = _qkv(_plain_norm(h, valid, gamma), w_qkv, tm)
    biases = {}
    flat = jnp.zeros((rows, h.shape[1]), _BF16)
    for t in trunks:
        grid_rows = t.n_tok // GRID_W
        key = grid_rows if grid_rows < 3 * Q_ROWS else -1
        if key not in biases:
            biases[key] = _attention_bias(rpb, grid_rows)
        flat = _neighbourhood_attention(qkv, t, biases[key], flat)
    return _residual_matmul(flat, w_out, h, tm, "na_out")


def _valid_rows(trunks, rows):
    r = jnp.arange(rows, dtype=jnp.int32)[:, None]
    v = jnp.zeros((rows, 1), jnp.bool_)
    for t in trunks:
        inside = (r >= t.base) & (r < t.base + t.batch * t.slot)
        v = v | (inside & ((r - t.base) % t.slot >= SLOT_PAD))
    return jnp.broadcast_to(v.astype(_F32), (rows, LANES))


def _ffn_up(xp, w_up, w_conv):
    f = w_up.shape[1] // 2
    ws = [w_up[:, :f].astype(_BF16), w_up[:, f:].astype(_BF16)]
    cs = [w_conv[:, :f], w_conv[:, f:]]
    tn = min(_FFN_TN, f)
    nj = f // tn
    a = _conv_matmul(xp, ws, cs, [0, 0], tn, nj, 0, f, _ffn_epilogue, (_BF16,), "ffn_up")
    tail = f - nj * tn
    if tail:
        assert tail % LANES == 0 and (nj * tn) % tail == 0
        a = _conv_matmul(xp, ws, cs, [0, 0], tail, 1, nj * tn // tail, f, _ffn_epilogue, (_BF16,), "ffn_up_tail",
                         onto=a)
    return a[0]


def kernel(x_prompt, x_sample, meta_tokens, norm_mix, norm_ffn, norm_final, hy_w_in, hy_w_conv, hy_f_w1,
           hy_f_w_inner, hy_f_b, hy_f_freq, hy_f_w_out, hy_delta, hy_skip, hy_w_out, na_w_qkv, na_rpb,
           na_w_out, ffn_w_up, ffn_w_conv, ffn_w_down):
    d = x_prompt.shape[-1]
    depth = norm_mix.shape[0]
    xs = (x_prompt, x_sample)
    tm = _TM
    trunks, rows = _plan([(x.shape[0], x.shape[1]) for x in xs], tm)
    valid = _valid_rows(trunks, rows)
    dft = {t.seq: _dft_matrices(t.seq) for t in trunks}

    meta = meta_tokens.astype(_F32)
    seqs = [jnp.concatenate([jnp.broadcast_to(meta[None], (x.shape[0], N_META, d)), x], axis=1) for x in xs]
    h = _to_flat(seqs, trunks, rows, _F32)

    bf = lambda w: w.astype(_BF16)
    for i in range(depth):
        j = i // 2
        if i % 2 == 0:
            filt = (hy_f_w1[j], hy_f_w_inner[j], hy_f_b[j], hy_f_freq[j], hy_f_w_out[j])
            h = _hyena_mixer(h, valid, norm_mix[i], trunks, rows, tm, bf(hy_w_in[j]), hy_w_conv[j], filt,
                             hy_delta[j], hy_skip[j], bf(hy_w_out[j]), dft)
        else:
            h = _na_mixer(h, valid, norm_mix[i], trunks, rows, tm, bf(na_w_qkv[j]), na_rpb[j], bf(na_w_out[j]))
        a = _ffn_up(_halo_norm(h, valid, norm_ffn[i], tm), ffn_w_up[i], ffn_w_conv[i])
        h = _residual_matmul(a, bf(ffn_w_down[i]), h, tm, "ffn_down")
    return tuple(_final_norm(h, norm_final, t) for t in trunks)
```

```python
import functools
import math
from typing import NamedTuple

import numpy as np
import jax
import jax.numpy as jnp
from jax import lax
from jax.experimental import pallas as pl
from jax.experimental.pallas import tpu as pltpu

N_META = 16
GRID_W = 64
HEAD_DIM = 128
WIN_ROWS = 8
WIN_COLS = 16
HY_EMB = 33
HY_BANDS = (HY_EMB - 1) // 2
RMS_EPS = 1e-6

SLOT_PAD = 16
HALO = 8
EPILOGUE_ROWS = 32
MXU_COLS = 256
RESIDENT_WEIGHT_K = 4096
FFT_RADIX = 16
Q_ROWS = 4
Q_BLOCK = Q_ROWS * GRID_W
MASK_VALUE = -1e30
LANES = 128
V7X_VMEM_LIMIT = 56 * 1024 * 1024
DFT_BLOCK_BYTES = 10 * 1024 * 1024
TWIDDLE_LANES = 8

_TM = 832
_HYENA_TM = 832
_FFN_TN = 512
_F32 = jnp.float32
_BF16 = jnp.bfloat16


class _Trunk(NamedTuple):
    batch: int
    n_tok: int
    seq: int
    slot: int
    base: int


def _round_up(x, m):
    return (x + m - 1) // m * m


def _pick(n, target, mult):
    best = None
    for d in range(mult, min(n, target) + 1, mult):
        if n % d == 0:
            best = d
    return best if best is not None else n


def _fft_sizes(seq):
    n1 = _round_up(-(-(2 * seq - 1) // FFT_RADIX), 8)
    n1z = _round_up(-(-seq // FFT_RADIX), 8)
    return n1, n1z


def _plan(shapes, tm):
    trunks, base = [], 0
    for batch, n_tok in shapes:
        seq = N_META + n_tok
        slot = SLOT_PAD + seq
        trunks.append(_Trunk(batch, n_tok, seq, slot, base))
        base += batch * slot
    reach = max(t.base + (t.batch - 1) * t.slot + SLOT_PAD + FFT_RADIX * _fft_sizes(t.seq)[1] for t in trunks)
    rows = _round_up(max(base + SLOT_PAD, reach), tm)
    return trunks, rows


def _params(sem, vmem=V7X_VMEM_LIMIT):
    return pltpu.CompilerParams(dimension_semantics=sem, vmem_limit_bytes=vmem)


def _window(rows, cols, index):
    def aligned(*args):
        r, c = index(*args)
        hint = lambda x, m: x if isinstance(x, int) else pl.multiple_of(x, m)
        return hint(r, SLOT_PAD), hint(c, LANES)
    return pl.BlockSpec((pl.Element(rows), pl.Element(cols)), aligned)


def _norm_rows(h, g, valid):
    y = h * lax.rsqrt(jnp.mean(h * h, axis=-1, keepdims=True) + RMS_EPS)
    return jnp.where(valid[:, 0:1] > 0, y * g, 0.0)


def _final_norm_kernel(h_ref, g_ref, o_ref):
    x = h_ref[...]
    o_ref[...] = x * lax.rsqrt(jnp.mean(x * x, axis=-1, keepdims=True) + RMS_EPS) * g_ref[...]


def _final_norm(h, g, trunk):
    d = h.shape[1]
    tr = _pick(trunk.n_tok, 256, 16)
    nr = trunk.n_tok // tr
    tok0 = trunk.base + SLOT_PAD + N_META
    out = pl.pallas_call(
        _final_norm_kernel,
        grid=(trunk.batch, nr),
        in_specs=[_window(tr, d, lambda i, r: (tok0 + i * trunk.slot + r * tr, 0)),
                  pl.BlockSpec((1, d), lambda i, r: (0, 0))],
        out_specs=pl.BlockSpec((tr, d), lambda i, r: (i * nr + r, 0)),
        out_shape=jax.ShapeDtypeStruct((trunk.batch * trunk.n_tok, d), _F32),
        compiler_params=_params(("arbitrary", "arbitrary")),
        name="final_norm",
    )(h, g.reshape(1, d))
    return out.reshape(trunk.batch, trunk.n_tok, d)


def _conv3(p_ref, k, c, r0, nr):
    at = lambda shift: p_ref[k, HALO + shift + r0:HALO + shift + r0 + nr, :]
    return c[0:1] * at(-1) + c[1:2] * at(0) + c[2:3] * at(1)


def _ffn_epilogue(vals, outs, rows):
    g, v = vals
    outs[0][rows, :] = (g * (1.0 / (1.0 + jnp.exp(-g))) * v).astype(outs[0].dtype)


def _hyena_epilogue(vals, outs, rows):
    x0, x1, v = vals
    outs[0][rows, :] = x0
    outs[1][rows, :] = v * x1


def _halo_norm_kernel(hp_ref, hc_ref, hn_ref, vp_ref, vc_ref, vn_ref, g_ref, o_ref):
    g = g_ref[...]
    rows = jnp.concatenate([_norm_rows(hp_ref[...], g, vp_ref[...]), _norm_rows(hc_ref[...], g, vc_ref[...]),
                            _norm_rows(hn_ref[...], g, vn_ref[...])], axis=0)
    o_ref[...] = pltpu.bitcast(rows.astype(_BF16), jnp.uint32)


def _halo_norm(h, valid, gamma, tm):
    rows, d = h.shape
    per, last = tm // HALO, rows // HALO - 1

    def panel(width):
        return [pl.BlockSpec((HALO, width), lambda i: (jnp.maximum(i * per - 1, 0), 0)),
                pl.BlockSpec((tm, width), lambda i: (i, 0)),
                pl.BlockSpec((HALO, width), lambda i: (jnp.minimum((i + 1) * per, last), 0))]

    return pl.pallas_call(
        _halo_norm_kernel,
        grid=(rows // tm,),
        in_specs=panel(d) + panel(LANES) + [pl.BlockSpec((1, d), lambda i: (0, 0))],
        out_specs=pl.BlockSpec((None, tm // 2 + HALO, d), lambda i: (i, 0, 0)),
        out_shape=jax.ShapeDtypeStruct((rows // tm, tm // 2 + HALO, d), jnp.uint32),
        compiler_params=_params(("arbitrary",)),
        name="halo_norm",
    )(h, h, h, valid, valid, valid, gamma.reshape(1, d))


def _plain_norm_kernel(h_ref, v_ref, g_ref, o_ref):
    o_ref[...] = _norm_rows(h_ref[...], g_ref[...], v_ref[...]).astype(o_ref.dtype)


def _plain_norm(h, valid, gamma):
    rows, d = h.shape
    tr = _pick(rows, 256, 16)
    return pl.pallas_call(
        _plain_norm_kernel,
        grid=(rows // tr,),
        in_specs=[pl.BlockSpec((tr, d), lambda i: (i, 0)),
                  pl.BlockSpec((tr, LANES), lambda i: (i, 0)),
                  pl.BlockSpec((1, d), lambda i: (0, 0))],
        out_specs=pl.BlockSpec((tr, d), lambda i: (i, 0)),
        out_shape=jax.ShapeDtypeStruct((rows, d), _BF16),
        compiler_params=_params(("arbitrary",)),
        name="rmsnorm",
    )(h, valid, gamma.reshape(1, d))


def _conv_matmul_kernel(*refs, tm, nw, n_onto, epilogue):
    x_ref = refs[0]
    w_refs = refs[1:1 + nw]
    c_refs = refs[1 + nw:1 + 2 * nw]
    out_refs = refs[1 + 2 * nw + n_onto:-2]
    pa_ref, pb_ref = refs[-2:]
    s = pl.program_id(0)

    @pl.when(s == 0)
    def _():
        pb_ref[...] = jnp.zeros_like(pb_ref)

    def step(prev_ref, cur_ref):
        tn = cur_ref.shape[2]
        pieces = [(k, c0) for k in range(nw) for c0 in range(0, tn, MXU_COLS)]
        chunks = list(range(0, tm, EPILOGUE_ROWS))
        per_piece = -(-len(chunks) // len(pieces))
        cs = [c_refs[k][...] for k in range(nw)]
        x = pltpu.bitcast(x_ref[...], _BF16)
        for n, (k, c0) in enumerate(pieces):
            cols = slice(c0, min(c0 + MXU_COLS, tn))
            cur_ref[k, :, cols] = jnp.dot(x, w_refs[k][:, cols], preferred_element_type=_F32)
            for r0 in chunks[n * per_piece:(n + 1) * per_piece]:
                epilogue([_conv3(prev_ref, j, cs[j], r0, EPILOGUE_ROWS) for j in range(nw)], out_refs,
                         slice(r0, r0 + EPILOGUE_ROWS))

    @pl.when(s % 2 == 0)
    def _():
        step(pb_ref, pa_ref)

    @pl.when(s % 2 == 1)
    def _():
        step(pa_ref, pb_ref)


def _conv_matmul(xp, ws, cs, group_tiles, tn, nj, tile0, out_cols, epilogue, out_dtypes, name, onto=()):
    ni, packed_rows, d = xp.shape
    tmh = 2 * packed_rows
    tm = tmh - 2 * HALO
    nw = len(ws)
    steps = ni * nj
    cur = lambda s: jnp.minimum(s, steps - 1)
    lag = lambda s: jnp.maximum(s - 1, 0)
    wspec = lambda k: pl.BlockSpec((d, tn), lambda s: (0, cur(s) // ni + tile0 + group_tiles[k]),
                                   pipeline_mode=pl.Buffered(1))
    cspec = lambda k: pl.BlockSpec((3, tn), lambda s: (0, lag(s) // ni + tile0 + group_tiles[k]))
    out = pl.BlockSpec((tm, tn), lambda s: (lag(s) % ni, lag(s) // ni + tile0))
    return pl.pallas_call(
        functools.partial(_conv_matmul_kernel, tm=tm, nw=nw, n_onto=len(onto), epilogue=epilogue),
        grid=(steps + 1,),
        in_specs=[pl.BlockSpec((None, packed_rows, d), lambda s: (cur(s) % ni, 0, 0))]
        + [wspec(k) for k in range(nw)] + [cspec(k) for k in range(nw)]
        + [pl.BlockSpec(memory_space=pl.ANY)] * len(onto),
        out_specs=[out] * len(out_dtypes),
        out_shape=[jax.ShapeDtypeStruct((ni * tm, out_cols), dt) for dt in out_dtypes],
        input_output_aliases={1 + 2 * nw + k: k for k in range(len(onto))},
        scratch_shapes=[pltpu.VMEM((nw, tmh, tn), _F32), pltpu.VMEM((nw, tmh, tn), _F32)],
        compiler_params=_params(("arbitrary",)),
        name=name,
    )(xp, *ws, *cs, *onto)


def _qkv_kernel(x_ref, w_ref, o_ref, *, n_q_tiles, scale):
    acc = jnp.dot(x_ref[...], w_ref[...], preferred_element_type=_F32)
    s = jnp.where(pl.program_id(0) < n_q_tiles, scale, 1.0).astype(_F32)
    o_ref[...] = (acc * s).astype(o_ref.dtype)


def _qkv(hn, w_qkv, tm):
    rows, d = hn.shape
    n = w_qkv.shape[1]
    tn = _pick(d, 1024, LANES)
    return pl.pallas_call(
        functools.partial(_qkv_kernel, n_q_tiles=d // tn, scale=HEAD_DIM ** -0.5),
        grid=(n // tn, rows // tm),
        in_specs=[pl.BlockSpec((tm, d), lambda j, i: (i, 0)),
                  pl.BlockSpec((d, tn), lambda j, i: (0, j))],
        out_specs=pl.BlockSpec((tm, tn), lambda j, i: (i, j)),
        out_shape=jax.ShapeDtypeStruct((rows, n), _BF16),
        compiler_params=_params(("arbitrary", "arbitrary")),
        name="qkv_proj",
    )(hn, w_qkv)


def _residual_kernel(x_ref, w_ref, h_ref, o_ref):
    o_ref[...] = h_ref[...] + jnp.dot(x_ref[...], w_ref[...], preferred_element_type=_F32)


def _residual_matmul(x, w, h, tm, name):
    rows, k = x.shape
    d = w.shape[1]
    weights_resident = k <= RESIDENT_WEIGHT_K
    tn = _pick(d, 1024 if weights_resident else 256, LANES)
    grid = (d // tn, rows // tm) if weights_resident else (rows // tm, d // tn)
    ij = (lambda a, b: (b, a)) if weights_resident else (lambda a, b: (a, b))
    return pl.pallas_call(
        _residual_kernel,
        grid=grid,
        in_specs=[pl.BlockSpec((tm, k), lambda a, b: (ij(a, b)[0], 0)),
                  pl.BlockSpec((k, tn), lambda a, b: (0, ij(a, b)[1])),
                  pl.BlockSpec((tm, tn), lambda a, b: ij(a, b))],
        out_specs=pl.BlockSpec((tm, tn), lambda a, b: ij(a, b)),
        out_shape=jax.ShapeDtypeStruct((rows, d), _F32),
        input_output_aliases={2: 0},
        compiler_params=_params(("arbitrary", "arbitrary")),
        name=name,
    )(x, w, h)


class _Dft(NamedTuple):
    fwd: jax.Array
    inv: jax.Array
    twiddle: jax.Array


def _dft_matrices(seq):
    assert seq % FFT_RADIX == 0
    n1, n1z = _fft_sizes(seq)
    n = FFT_RADIX * n1
    k1 = jnp.arange(n1, dtype=jnp.int32)[:, None]
    m1 = jnp.arange(n1z, dtype=jnp.int32)[None, :]
    phase = ((m1 * k1) % n1).astype(_F32) * (2.0 * math.pi / n1)
    f = jnp.concatenate([jnp.cos(phase), -jnp.sin(phase)], axis=0)
    f = jnp.where(m1 * FFT_RADIX < seq, f, 0.0)
    p = jnp.arange(FFT_RADIX, dtype=jnp.int32)[:, None]
    tphase = (p * k1[:, 0][None, :]).astype(_F32) * (2.0 * math.pi / n)
    tw = jnp.stack([jnp.cos(tphase), -jnp.sin(tphase)], axis=1)
    tw = jnp.broadcast_to(tw[..., None], tw.shape + (TWIDDLE_LANES,))
    return _Dft(f.astype(_BF16), (f.T * (1.0 / n)).astype(_BF16), tw)


def _plane_rows(refs, plane, n1z):
    rows = pl.ds(plane, n1z, stride=FFT_RADIX)
    return jnp.concatenate([ref[rows, :] for ref in refs], axis=1)


def _planes_per_step(n1, dc):
    plane_block = 2 * FFT_RADIX * n1 * dc * 4
    return FFT_RADIX if plane_block <= DFT_BLOCK_BYTES else 1


def _lane_windows(n_rows, dc, row0, row_stride):
    sub = lambda l: _window(n_rows, LANES, lambda i, c, p: (row0 + i * row_stride, c * dc + l * LANES))
    return [sub(l) for l in range(dc // LANES)]


def _paired_dft(f, xa_refs, xb_refs, plane, n1z):
    xa, xb = _plane_rows(xa_refs, plane, n1z), _plane_rows(xb_refs, plane, n1z)
    dc, n1 = xa.shape[1], f.shape[0] // 2
    a = jnp.dot(f, jnp.concatenate([xa, xb], axis=1).astype(_BF16), preferred_element_type=_F32)
    return a[:n1, :dc] - a[n1:, dc:], a[n1:, :dc] + a[:n1, dc:]


def _dft_fwd_kernel(*refs, pp, pair):
    x_refs, f_ref, t_ref, o_ref = refs[:-3], refs[-3], refs[-2], refs[-1]
    n1z, n1 = f_ref.shape[1], o_ref.shape[2]
    half = len(x_refs) // 2
    for q in range(pp):
        plane = pl.program_id(2) * pp + q
        if pair:
            ar, ai = _paired_dft(f_ref[...], x_refs[:half], x_refs[half:], plane, n1z)
        else:
            a = jnp.dot(f_ref[...], _plane_rows(x_refs, plane, n1z).astype(_BF16), preferred_element_type=_F32)
            ar, ai = a[:n1], a[n1:]
        tr, ti = t_ref[q, 0, :, 0:1], t_ref[q, 1, :, 0:1]
        o_ref[0, q] = ar * tr - ai * ti
        o_ref[1, q] = ar * ti + ai * tr


def _dft_fwd(x, dft, n_sig, row0, row_stride, pair=False, lanes=256):
    d = x.shape[1]
    r = FFT_RADIX
    n1, n1z = dft.fwd.shape[0] // 2, dft.fwd.shape[1]
    dc = _pick(d, lanes, LANES)
    pp = _planes_per_step(n1, dc)
    if pair:
        assert n_sig % 2 == 0
        n_sig //= 2
        wins = (_lane_windows(n1z * r, dc, row0, 2 * row_stride)
                + _lane_windows(n1z * r, dc, row0 + row_stride, 2 * row_stride))
    else:
        wins = _lane_windows(n1z * r, dc, row0, row_stride)
    return pl.pallas_call(
        functools.partial(_dft_fwd_kernel, pp=pp, pair=pair),
        grid=(n_sig, d // dc, r // pp),
        in_specs=wins + [pl.BlockSpec((2 * n1, n1z), lambda i, c, p: (0, 0)),
                         pl.BlockSpec((pp, 2, n1, TWIDDLE_LANES), lambda i, c, p: (p, 0, 0, 0))],
        out_specs=pl.BlockSpec((2, pp, None, n1, dc), lambda i, c, p: (0, p, i, 0, c)),
        out_shape=jax.ShapeDtypeStruct((2, r, n_sig, n1, d), _F32),
        compiler_params=_params(("arbitrary", "arbitrary", "arbitrary")),
        name="dft_fwd",
    )(*([x] * len(wins)), dft.fwd, dft.twiddle)


def _cmul_const(x, w):
    re, im = x
    wr, wi = float(w.real), float(w.imag)
    if abs(wi) < 1e-12:
        return (re, im) if wr > 0 else (-re, -im)
    if abs(wr) < 1e-12:
        return (-im, re) if wi > 0 else (im, -re)
    return (re * wr - im * wi, re * wi + im * wr)


def _fft_planes(xs, sign):
    n = len(xs)
    if n == 1:
        return xs
    ev = _fft_planes(xs[0::2], sign)
    od = _fft_planes(xs[1::2], sign)
    out = [None] * n
    for k in range(n // 2):
        w = complex(math.cos(2.0 * math.pi * k / n), sign * math.sin(2.0 * math.pi * k / n))
        tr, ti = _cmul_const(od[k], w)
        out[k] = (ev[k][0] + tr, ev[k][1] + ti)
        out[k + n // 2] = (ev[k][0] - tr, ev[k][1] - ti)
    return out


def _tile_loop(tk, dc, body):
    def step(r, carry):
        rows = pl.ds(pl.multiple_of(r * 8, 8), 8)
        for l in range(dc // LANES):
            body(rows, slice(l * LANES, (l + 1) * LANES))
        return carry
    lax.fori_loop(0, tk // 8, step, 0)


def _filter_spectrum_kernel(a_ref, s_ref, o_ref):
    r, tk, dc = o_ref.shape[1], o_ref.shape[2], o_ref.shape[3]

    def body(rows, lanes):
        inv = 1.0 / s_ref[:, lanes]
        xf = _fft_planes([(a_ref[0, n, 0, rows, lanes], a_ref[1, n, 0, rows, lanes]) for n in range(r)], -1)
        xb = _fft_planes([(a_ref[0, n, 1, rows, lanes], a_ref[1, n, 1, rows, lanes]) for n in range(r)], -1)
        for k in range(r):
            o_ref[0, k, rows, lanes] = (xf[k][0] + xb[k][0]) * inv
            o_ref[1, k, rows, lanes] = (xf[k][1] - xb[k][1]) * inv

    _tile_loop(tk, dc, body)


def _plane_tile(n1):
    return _pick(n1, 352, 8)


def _filter_spectrum(planes, norm):
    _, r, _, n1, d = planes.shape
    tk, dc = _plane_tile(n1), LANES
    return pl.pallas_call(
        _filter_spectrum_kernel,
        grid=(d // dc, n1 // tk),
        in_specs=[pl.BlockSpec((2, r, 2, tk, dc), lambda c, k: (0, 0, 0, k, c)),
                  pl.BlockSpec((1, dc), lambda c, k: (0, c))],
        out_specs=pl.BlockSpec((2, r, tk, dc), lambda c, k: (0, 0, k, c)),
        out_shape=jax.ShapeDtypeStruct((2, r, n1, d), _F32),
        compiler_params=_params(("arbitrary", "arbitrary")),
        name="filter_spectrum",
    )(planes, norm)


def _spectral_product_kernel(a_ref, k_ref, o_ref):
    r, tk, dc = o_ref.shape[1], o_ref.shape[2], o_ref.shape[3]

    def body(rows, lanes):
        x = _fft_planes([(a_ref[0, n, rows, lanes], a_ref[1, n, rows, lanes]) for n in range(r)], -1)
        y = []
        for k in range(r):
            kr, ki = k_ref[0, k, rows, lanes], k_ref[1, k, rows, lanes]
            y.append((x[k][0] * kr - x[k][1] * ki, x[k][0] * ki + x[k][1] * kr))
        z = _fft_planes(y, +1)
        for n in range(r):
            o_ref[0, n, rows, lanes] = z[n][0]
            o_ref[1, n, rows, lanes] = z[n][1]

    _tile_loop(tk, dc, body)


def _spectral_product(planes, kspec):
    _, r, b, n1, d = planes.shape
    tk, dc = _plane_tile(n1), LANES
    spec = pl.BlockSpec((2, r, None, tk, dc), lambda c, k, i: (0, 0, i, k, c))
    return pl.pallas_call(
        _spectral_product_kernel,
        grid=(d // dc, n1 // tk, b),
        in_specs=[spec, pl.BlockSpec((2, r, tk, dc), lambda c, k, i: (0, 0, k, c))],
        out_specs=spec,
        out_shape=jax.ShapeDtypeStruct(planes.shape, _F32),
        compiler_params=_params(("arbitrary", "arbitrary", "arbitrary")),
        name="spectral_product",
    )(planes, kspec)


def _untwiddle(br, bi, tr, ti):
    return br * tr + bi * ti, bi * tr - br * ti


def _dft_inv_kernel(*refs, pp, nsub, pair):
    b_ref, h_ref, t_ref, skip_ref, o_ref = refs[0], refs[1], refs[2], refs[-2], refs[-1]
    sigs = refs[3:-2]
    n1z = h_ref.shape[0]
    dc = skip_ref.shape[1]
    for q in range(pp):
        plane = pl.program_id(2) * pp + q
        cr, ci = _untwiddle(b_ref[0, q], b_ref[1, q], t_ref[q, 0, :, 0:1], t_ref[q, 1, :, 0:1])
        if pair:
            vxa, vxb, x0a, x0b = (sigs[k * nsub:(k + 1) * nsub] for k in range(4))
            bb = jnp.concatenate([jnp.concatenate([cr, ci], axis=0), jnp.concatenate([ci, -cr], axis=0)], axis=1)
            y = jnp.dot(h_ref[...], bb.astype(_BF16), preferred_element_type=_F32)
            o_ref[0, q] = (y[:, :dc] + _plane_rows(vxa, plane, n1z) * skip_ref[...]) * _plane_rows(x0a, plane, n1z)
            o_ref[1, q] = (y[:, dc:] + _plane_rows(vxb, plane, n1z) * skip_ref[...]) * _plane_rows(x0b, plane, n1z)
        else:
            vx_refs, x0_refs = sigs[:nsub], sigs[nsub:]
            bb = jnp.concatenate([cr, ci], axis=0).astype(_BF16)
            y = jnp.dot(h_ref[...], bb, preferred_element_type=_F32)
            o_ref[0, q] = (y + _plane_rows(vx_refs, plane, n1z) * skip_ref[...]) * _plane_rows(x0_refs, plane, n1z)


def _dft_inv(planes, dft, vx, x0, skip, batch, row0, row_stride, pair=False):
    _, r, n_sets, n1, d = planes.shape
    n1z = dft.inv.shape[0]
    per = 2 if pair else 1
    assert n_sets * per == batch
    dc = _pick(d, 256 // per, LANES)
    pp = _planes_per_step(n1, dc)
    wins = [w for k in range(per) for w in _lane_windows(n1z * r, dc, row0 + k * row_stride, per * row_stride)]
    return pl.pallas_call(
        functools.partial(_dft_inv_kernel, pp=pp, nsub=len(wins) // per, pair=pair),
        grid=(n_sets, d // dc, r // pp),
        in_specs=[pl.BlockSpec((2, pp, None, n1, dc), lambda i, c, p: (0, p, i, 0, c)),
                  pl.BlockSpec((n1z, 2 * n1), lambda i, c, p: (0, 0)),
                  pl.BlockSpec((pp, 2, n1, TWIDDLE_LANES), lambda i, c, p: (p, 0, 0, 0))]
        + wins + wins + [pl.BlockSpec((1, dc), lambda i, c, p: (0, c))],
        out_specs=pl.BlockSpec((per, pp, n1z, dc), lambda i, c, p: (i, p, 0, c)),
        out_shape=jax.ShapeDtypeStruct((batch, r, n1z, d), _F32),
        compiler_params=_params(("arbitrary", "arbitrary", "arbitrary")),
        name="dft_inv",
    )(planes, dft.inv, dft.twiddle, *([vx] * len(wins)), *([x0] * len(wins)), skip)


def _fused_conv_kernel(*refs, nsub):
    vxa, vxb, x0a, x0b = (refs[k * nsub:(k + 1) * nsub] for k in range(4))
    f_ref, h_ref, t_ref, k_ref, skip_ref, o_ref, a_ref = refs[4 * nsub:]
    r, n1, dc = a_ref.shape[1], a_ref.shape[2], a_ref.shape[3]
    n1z = f_ref.shape[1]
    for p in range(r):
        ar, ai = _paired_dft(f_ref[...], vxa, vxb, p, n1z)
        tr, ti = t_ref[p, 0, :, 0:1], t_ref[p, 1, :, 0:1]
        a_ref[0, p] = ar * tr - ai * ti
        a_ref[1, p] = ar * ti + ai * tr

    def body(rows, lanes):
        x = _fft_planes([(a_ref[0, n, rows, lanes], a_ref[1, n, rows, lanes]) for n in range(r)], -1)
        y = []
        for k in range(r):
            kr, ki = k_ref[0, k, rows, lanes], k_ref[1, k, rows, lanes]
            y.append((x[k][0] * kr - x[k][1] * ki, x[k][0] * ki + x[k][1] * kr))
        z = _fft_planes(y, +1)
        for n in range(r):
            a_ref[0, n, rows, lanes] = z[n][0]
            a_ref[1, n, rows, lanes] = z[n][1]

    _tile_loop(n1, dc, body)
    for p in range(r):
        cr, ci = _untwiddle(a_ref[0, p], a_ref[1, p], t_ref[p, 0, :, 0:1], t_ref[p, 1, :, 0:1])
        bb = jnp.concatenate([jnp.concatenate([cr, ci], axis=0), jnp.concatenate([ci, -cr], axis=0)], axis=1)
        y = jnp.dot(h_ref[...], bb.astype(_BF16), preferred_element_type=_F32)
        o_ref[0, p] = (y[:, :dc] + _plane_rows(vxa, p, n1z) * skip_ref[...]) * _plane_rows(x0a, p, n1z)
        o_ref[1, p] = (y[:, dc:] + _plane_rows(vxb, p, n1z) * skip_ref[...]) * _plane_rows(x0b, p, n1z)


def _fused_conv(vx, x0, kspec, dft, skip, batch, row0, row_stride):
    assert batch % 2 == 0
    d = vx.shape[1]
    r = FFT_RADIX
    n1, n1z = dft.fwd.shape[0] // 2, dft.fwd.shape[1]
    dc = LANES
    sig = lambda spec: pl.BlockSpec(spec.block_shape, lambda c, i: spec.index_map(i, c, 0))
    wa = [sig(w) for w in _lane_windows(n1z * r, dc, row0, 2 * row_stride)]
    wb = [sig(w) for w in _lane_windows(n1z * r, dc, row0 + row_stride, 2 * row_stride)]
    return pl.pallas_call(
        functools.partial(_fused_conv_kernel, nsub=len(wa)),
        grid=(d // dc, batch // 2),
        in_specs=wa + wb + wa + wb + [
            pl.BlockSpec((2 * n1, n1z), lambda c, i: (0, 0)),
            pl.BlockSpec((n1z, 2 * n1), lambda c, i: (0, 0)),
            pl.BlockSpec((r, 2, n1, TWIDDLE_LANES), lambda c, i: (0, 0, 0, 0)),
            pl.BlockSpec((2, r, n1, dc), lambda c, i: (0, 0, 0, c)),
            pl.BlockSpec((1, dc), lambda c, i: (0, c))],
        out_specs=pl.BlockSpec((2, r, n1z, dc), lambda c, i: (i, 0, 0, c)),
        out_shape=jax.ShapeDtypeStruct((batch, r, n1z, d), _F32),
        scratch_shapes=[pltpu.VMEM((2, r, n1, dc), _F32)],
        compiler_params=_params(("arbitrary", "arbitrary")),
        name="hyena_conv",
    )(*([vx] * (2 * len(wa))), *([x0] * (2 * len(wa))), dft.fwd, dft.inv, dft.twiddle, kspec, skip)


def _filter_mlp_kernel(z_ref, w1_ref, wi_ref, b_ref, f_ref, o_ref):
    hi = lax.Precision.HIGHEST
    h = jnp.sin(f_ref[0:1] * (jnp.dot(z_ref[...], w1_ref[...], precision=hi, preferred_element_type=_F32) + b_ref[0:1]))
    h = jnp.sin(f_ref[1:2] * (jnp.dot(h, wi_ref[0], precision=hi, preferred_element_type=_F32) + b_ref[1:2]))
    h = jnp.sin(f_ref[2:3] * (jnp.dot(h, wi_ref[1], precision=hi, preferred_element_type=_F32) + b_ref[2:3]))
    o_ref[...] = h


def _filter_taps_kernel(hm_ref, z_ref, wf_ref, wb_ref, d_ref, k_ref, s_ref):
    hi = lax.Precision.HIGHEST
    t = z_ref[:, 0:1]
    mf = z_ref[:, HY_EMB:HY_EMB + 1]
    mb = z_ref[:, HY_EMB + 1:HY_EMB + 2]
    hm = hm_ref[...]
    kf = jnp.dot(hm, wf_ref[...], precision=hi, preferred_element_type=_F32) * jnp.exp(-t * jnp.abs(d_ref[0:1])) * mf
    kb = jnp.dot(hm, wb_ref[...], precision=hi, preferred_element_type=_F32) * jnp.exp(-t * jnp.abs(d_ref[1:2])) * mb
    k_ref[0] = kf
    k_ref[1] = kb

    @pl.when(pl.program_id(1) == 0)
    def _():
        s_ref[...] = jnp.zeros_like(s_ref)
    s_ref[...] += jnp.sum(jnp.abs(kf) + jnp.abs(kb), axis=0, keepdims=True)


def _position_features(seq, rows):
    pos = jnp.arange(seq, dtype=_F32)[:, None]
    t = pos / max(seq - 1, 1)
    f = jnp.linspace(1e-4, HY_BANDS - 1, HY_BANDS, dtype=_F32)[None, :]
    ang = f * (2.0 * math.pi / seq) * pos
    fwd = jnp.ones((seq, 1), _F32)
    bwd = (pos >= 1).astype(_F32)
    z = jnp.concatenate([t, jnp.cos(ang), -jnp.sin(ang), fwd, bwd], axis=-1)
    return jnp.pad(z, ((0, rows - seq), (0, LANES - z.shape[1])))


def _hyena_filter(seq, w1, w_inner, b, freq, w_out, delta, dft):
    _, n1z = _fft_sizes(seq)
    lp = n1z * FFT_RADIX
    width = w1.shape[1]
    d = w_out.shape[1] // 2
    z = _position_features(seq, lp)
    w1p = jnp.pad(w1, ((0, LANES - w1.shape[0]), (0, 0)))
    tr = n1z
    hm = pl.pallas_call(
        _filter_mlp_kernel,
        grid=(lp // tr,),
        in_specs=[pl.BlockSpec((tr, LANES), lambda i: (i, 0)),
                  pl.BlockSpec((LANES, width), lambda i: (0, 0)),
                  pl.BlockSpec((2, width, width), lambda i: (0, 0, 0)),
                  pl.BlockSpec((3, width), lambda i: (0, 0)),
                  pl.BlockSpec((3, width), lambda i: (0, 0))],
        out_specs=pl.BlockSpec((tr, width), lambda i: (i, 0)),
        out_shape=jax.ShapeDtypeStruct((lp, width), _F32),
        compiler_params=_params(("arbitrary",)),
        name="filter_mlp",
    )(z, w1p, w_inner, b, freq)
    dc = _pick(d, 512, LANES)
    nc = d // dc
    taps, norm = pl.pallas_call(
        _filter_taps_kernel,
        grid=(nc, lp // tr),
        in_specs=[pl.BlockSpec((tr, width), lambda c, i: (i, 0)),
                  pl.BlockSpec((tr, LANES), lambda c, i: (i, 0)),
                  pl.BlockSpec((width, dc), lambda c, i: (0, c)),
                  pl.BlockSpec((width, dc), lambda c, i: (0, c + nc)),
                  pl.BlockSpec((2, dc), lambda c, i: (0, c))],
        out_specs=[pl.BlockSpec((2, tr, dc), lambda c, i: (0, i, c)), pl.BlockSpec((1, dc), lambda c, i: (0, c))],
        out_shape=[jax.ShapeDtypeStruct((2, lp, d), _F32), jax.ShapeDtypeStruct((1, d), _F32)],
        compiler_params=_params(("arbitrary", "arbitrary")),
        name="filter_taps",
    )(hm, z, w_out, w_out, delta)
    n1 = dft.fwd.shape[0] // 2
    lanes = 256 if _planes_per_step(n1, _pick(d, 256, LANES)) == FFT_RADIX else 512
    return _filter_spectrum(_dft_fwd(taps.reshape(2 * lp, d), dft, 2, 0, lp, lanes=lanes), norm)


def _to_flat(parts, trunks, rows, dtype):
    d = parts[0].shape[-1]
    out = []
    for part, t in zip(parts, trunks):
        out.append(jnp.pad(part.astype(dtype), ((0, 0), (SLOT_PAD, 0), (0, 0))).reshape(t.batch * t.slot, d))
    used = sum(t.batch * t.slot for t in trunks)
    out.append(jnp.zeros((rows - used, d), dtype))
    return jnp.concatenate(out, axis=0)


def _hyena_mixer(h, valid, gamma, trunks, rows, tm, w_in, w_conv, filt, delta, skip, w_out, dft):
    d = h.shape[1]
    tn = _pick(d, 512, LANES)
    nj = d // tn
    tm_in = _pick(rows, _HYENA_TM, 2 * HALO)
    x0, vx = _conv_matmul(_halo_norm(h, valid, gamma, tm_in), [w_in] * 3, [w_conv] * 3, [0, nj, 2 * nj], tn, nj, 0, d,
                          _hyena_epilogue, (_F32, _F32), "hyena_in")
    parts = []
    for t in trunks:
        kspec = _hyena_filter(t.seq, *filt, delta, dft[t.seq])
        row0 = t.base + SLOT_PAD
        n1 = dft[t.seq].fwd.shape[0] // 2
        pair = t.batch % 2 == 0
        if pair and _planes_per_step(n1, _pick(d, 256, LANES)) == FFT_RADIX:
            y = _fused_conv(vx, x0, kspec, dft[t.seq], skip.reshape(1, -1), t.batch, row0, t.slot)
        else:
            planes = _spectral_product(_dft_fwd(vx, dft[t.seq], t.batch, row0, t.slot, pair), kspec)
            y = _dft_inv(planes, dft[t.seq], vx, x0, skip.reshape(1, -1), t.batch, row0, t.slot, pair)
        parts.append(jnp.swapaxes(y, 1, 2).reshape(t.batch, -1, d)[:, :t.seq])
    return _residual_matmul(_to_flat(parts, trunks, rows, _BF16), w_out, h, tm, "hyena_out")


def _attention_bias(rpb, rows_in_grid):
    n_blocks = rows_in_grid // Q_ROWS
    kr_win = min(WIN_ROWS, rows_in_grid)
    qr = np.arange(Q_ROWS)[:, None, None]
    kb = np.arange(3)[None, :, None]
    kr = np.arange(Q_ROWS)[None, None, :]
    dr_idx, row_ok = [], []
    for g in (0, 1, n_blocks - 1):
        r = Q_ROWS * g + qr
        key_row = Q_ROWS * (g + kb - 1) + kr
        rs = np.clip(r - kr_win // 2, 0, rows_in_grid - kr_win)
        ok = (key_row >= rs) & (key_row < rs + kr_win) & (g + kb - 1 >= 0) & (g + kb - 1 < n_blocks)
        dr_idx.append(np.clip(key_row - r + (WIN_ROWS - 1), 0, 2 * WIN_ROWS - 2))
        row_ok.append(ok)
    dr_idx = np.stack(dr_idx)
    row_ok = np.stack(row_ok)
    cols = np.arange(GRID_W)
    col_start = np.clip(cols - WIN_COLS // 2, 0, GRID_W - WIN_COLS)
    col_ok = (cols[None, :] >= col_start[:, None]) & (cols[None, :] < col_start[:, None] + WIN_COLS)
    col_idx = np.clip(cols[None, :] - cols[:, None] + WIN_COLS - 1, 0, 2 * WIN_COLS - 2)
    by_col = rpb[:, :, col_idx]
    by_row = by_col[:, dr_idx]
    ok = row_ok[:, :, :, :, None, None] & col_ok[None, None, None, None]
    bias = jnp.where(ok[None], by_row, MASK_VALUE)
    bias = jnp.transpose(bias, (1, 0, 2, 5, 3, 4, 6))
    return bias.reshape(3, rpb.shape[0], Q_BLOCK, 3 * Q_BLOCK).astype(_F32)


def _na_kernel(q_ref, k0_ref, k1_ref, k2_ref, v0_ref, v1_ref, v2_ref, km_ref, vm_ref, bias_ref, o_ref, *, heads):
    nt = (((1,), (1,)), ((), ()))
    for h in range(heads):
        hs = slice(h * HEAD_DIM, (h + 1) * HEAD_DIM)
        q = q_ref[:, hs]
        s = [lax.dot_general(q, k_ref[:, hs], nt, preferred_element_type=_F32)
             + bias_ref[h, :, i * Q_BLOCK:(i + 1) * Q_BLOCK]
             for i, k_ref in enumerate((k0_ref, k1_ref, k2_ref))]
        sm = lax.dot_general(q, km_ref[:, hs], nt, preferred_element_type=_F32)
        m = jnp.max(sm, axis=-1, keepdims=True)
        for si in s:
            m = jnp.maximum(m, jnp.max(si, axis=-1, keepdims=True))
        pm = jnp.exp(sm - m)
        den = jnp.sum(pm, axis=-1, keepdims=True)
        acc = jnp.dot(pm.astype(_BF16), vm_ref[:, hs], preferred_element_type=_F32)
        for si, v_ref in zip(s, (v0_ref, v1_ref, v2_ref)):
            p = jnp.exp(si - m)
            den = den + jnp.sum(p, axis=-1, keepdims=True)
            acc = acc + jnp.dot(p.astype(_BF16), v_ref[:, hs], preferred_element_type=_F32)
        o_ref[:, hs] = (acc / den).astype(o_ref.dtype)


def _na_meta_kernel(q_ref, k_ref, v_ref, o_ref, *, heads):
    nt = (((1,), (1,)), ((), ()))
    for h in range(heads):
        hs = slice(h * HEAD_DIM, (h + 1) * HEAD_DIM)
        s = lax.dot_general(q_ref[:, hs], k_ref[:, hs], nt, preferred_element_type=_F32)
        p = jnp.exp(s - jnp.max(s, axis=-1, keepdims=True))
        acc = jnp.dot(p.astype(_BF16), v_ref[:, hs], preferred_element_type=_F32)
        o_ref[:, hs] = (acc / jnp.sum(p, axis=-1, keepdims=True)).astype(o_ref.dtype)


def _neighbourhood_attention(qkv, trunk, bias, flat):
    d = qkv.shape[1] // 3
    n_heads = d // HEAD_DIM
    hb = min(8, n_heads)
    wd = hb * HEAD_DIM
    nb = trunk.n_tok // Q_BLOCK
    b = trunk.batch
    tok0 = trunk.base + SLOT_PAD + N_META
    meta0 = (trunk.base + SLOT_PAD) // N_META
    slot_meta = trunk.slot // N_META

    def win(col0, shift):
        return _window(Q_BLOCK, wd, lambda h, i, g: (
            tok0 + i * trunk.slot + jnp.clip(g + shift, 0, nb - 1) * Q_BLOCK, col0 + h * wd))

    def meta(col0):
        return _window(N_META, wd, lambda h, i, g: (trunk.base + SLOT_PAD + i * trunk.slot, col0 + h * wd))

    variant = lambda g: jnp.where(g == 0, 0, jnp.where(g == nb - 1, 2, 1))
    by_row = lambda spec: pl.BlockSpec(spec.block_shape, lambda h, g, i: spec.index_map(h, i, g))
    flat = pl.pallas_call(
        lambda *refs: _na_kernel(*refs[:10], refs[-1], heads=hb),
        grid=(n_heads // hb, nb, b),
        in_specs=[by_row(s) for s in (win(0, 0), win(d, -1), win(d, 0), win(d, 1), win(2 * d, -1), win(2 * d, 0),
                                      win(2 * d, 1), meta(d), meta(2 * d))]
        + [pl.BlockSpec((None, hb, Q_BLOCK, 3 * Q_BLOCK), lambda h, g, i: (variant(g), h, 0, 0)),
           pl.BlockSpec(memory_space=pl.ANY)],
        out_specs=by_row(_window(Q_BLOCK, wd, lambda h, i, g: (tok0 + i * trunk.slot + g * Q_BLOCK, h * wd))),
        out_shape=jax.ShapeDtypeStruct(flat.shape, flat.dtype),
        input_output_aliases={10: 0},
        compiler_params=_params(("arbitrary", "arbitrary", "arbitrary")),
        name="na_grid",
    )(qkv, qkv, qkv, qkv, qkv, qkv, qkv, qkv, qkv, bias, flat)
    mspec = lambda k: pl.BlockSpec((N_META, d), lambda i: (meta0 + i * slot_meta, k))
    return pl.pallas_call(
        lambda q_ref, k_ref, v_ref, flat_ref, o_ref: _na_meta_kernel(q_ref, k_ref, v_ref, o_ref, heads=n_heads),
        grid=(b,),
        in_specs=[mspec(0), mspec(1), mspec(2), pl.BlockSpec(memory_space=pl.ANY)],
        out_specs=mspec(0),
        out_shape=jax.ShapeDtypeStruct(flat.shape, flat.dtype),
        input_output_aliases={3: 0},
        compiler_params=_params(("arbitrary",)),
        name="na_meta",
    )(qkv, qkv, qkv, flat)


def _na_mixer(h, valid, gamma, trunks, rows, tm, w_qkv, rpb, w_out):
    qkv = _qkv(_plain_norm(h, valid, gamma), w_qkv, tm)
    biases = {}
    flat = jnp.zeros((rows, h.shape[1]), _BF16)
    for t in trunks:
        grid_rows = t.n_tok // GRID_W
        key = grid_rows if grid_rows < 3 * Q_ROWS else -1
        if key not in biases:
            biases[key] = _attention_bias(rpb, grid_rows)
        flat = _neighbourhood_attention(qkv, t, biases[key], flat)
    return _residual_matmul(flat, w_out, h, tm, "na_out")


def _valid_rows(trunks, rows):
    r = jnp.arange(rows, dtype=jnp.int32)[:, None]
    v = jnp.zeros((rows, 1), jnp.bool_)
    for t in trunks:
        inside = (r >= t.base) & (r < t.base + t.batch * t.slot)
        v = v | (inside & ((r - t.base) % t.slot >= SLOT_PAD))
    return jnp.broadcast_to(v.astype(_F32), (rows, LANES))


def _ffn_up(xp, w_up, w_conv):
    f = w_up.shape[1] // 2
    ws = [w_up[:, :f].astype(_BF16), w_up[:, f:].astype(_BF16)]
    cs = [w_conv[:, :f], w_conv[:, f:]]
    tn = min(_FFN_TN, f)
    nj = f // tn
    a = _conv_matmul(xp, ws, cs, [0, 0], tn, nj, 0, f, _ffn_epilogue, (_BF16,), "ffn_up")
    tail = f - nj * tn
    if tail:
        assert tail % LANES == 0 and (nj * tn) % tail == 0
        a = _conv_matmul(xp, ws, cs, [0, 0], tail, 1, nj * tn // tail, f, _ffn_epilogue, (_BF16,), "ffn_up_tail",
                         onto=a)
    return a[0]


def kernel(x_prompt, x_sample, meta_tokens, norm_mix, norm_ffn, norm_final, hy_w_in, hy_w_conv, hy_f_w1,
           hy_f_w_inner, hy_f_b, hy_f_freq, hy_f_w_out, hy_delta, hy_skip, hy_w_out, na_w_qkv, na_rpb,
           na_w_out, ffn_w_up, ffn_w_conv, ffn_w_down):
    d = x_prompt.shape[-1]
    depth = norm_mix.shape[0]
    xs = (x_prompt, x_sample)
    tm = _TM
    trunks, rows = _plan([(x.shape[0], x.shape[1]) for x in xs], tm)
    valid = _valid_rows(trunks, rows)
    dft = {t.seq: _dft_matrices(t.seq) for t in trunks}

    meta = meta_tokens.astype(_F32)
    seqs = [jnp.concatenate([jnp.broadcast_to(meta[None], (x.shape[0], N_META, d)), x], axis=1) for x in xs]
    h = _to_flat(seqs, trunks, rows, _F32)

    bf = lambda w: w.astype(_BF16)
    for i in range(depth):
        j = i // 2
        if i % 2 == 0:
            filt = (hy_f_w1[j], hy_f_w_inner[j], hy_f_b[j], hy_f_freq[j], hy_f_w_out[j])
            h = _hyena_mixer(h, valid, norm_mix[i], trunks, rows, tm, bf(hy_w_in[j]), hy_w_conv[j], filt,
                             hy_delta[j], hy_skip[j], bf(hy_w_out[j]), dft)
        else:
            h = _na_mixer(h, valid, norm_mix[i], trunks, rows, tm, bf(na_w_qkv[j]), na_rpb[j], bf(na_w_out[j]))
        a = _ffn_up(_halo_norm(h, valid, norm_ffn[i], tm), ffn_w_up[i], ffn_w_conv[i])
        h = _residual_matmul(a, bf(ffn_w_down[i]), h, tm, "ffn_down")
    return tuple(_final_norm(h, norm_final, t) for t in trunks)
```
